```python
import jax, jax.numpy as jnp
from jax import lax
import numpy as np

D_MODEL = 1024
BATCH = 4
SEQ = 4096
DEPTH = 2
DEC_BATCH = 128
DEC_SEQ = 4
PAST_LEN = 16384
PAGE_SIZE = 128

N_EVEN = (DEPTH + 1) // 2
N_ODD = DEPTH // 2
MLA_HEADS = 8
MLA_NOPE = 64
MLA_ROPE = 32
MLA_V = 64
KV_RANK = 256
MLA_Q_W = MLA_HEADS * (MLA_NOPE + MLA_ROPE)
MLA_KV_W = KV_RANK + MLA_ROPE
MLA_SCALE = (MLA_NOPE + MLA_ROPE) ** -0.5
Q_BLOCK = 128
RW_HEADS = 8
RW_N = 64
RW_W = RW_HEADS * RW_N
RW_DECAY_LORA = 64
RW_A_LORA = 64
RW_G_LORA = 128
RW_SHIFT_W = 3 * RW_W + RW_DECAY_LORA + RW_A_LORA + RW_G_LORA
RW_GN_EPS = 64e-5
EVEN_IN_W = MLA_Q_W + MLA_KV_W + RW_SHIFT_W
EVEN_MIX_W = MLA_HEADS * MLA_V + RW_W
RET_HEADS = 4
RET_DK = 256
RET_DV = 512
RET_CHUNK = 128
ODD_IN_W = 2 * RET_HEADS * RET_DK + 2 * RET_HEADS * RET_DV
ODD_MIX_W = RET_HEADS * RET_DV
D_FF = 4 * D_MODEL
ROPE_BASE = 10000.0
NORM_EPS = 1e-6

kernel_name = 'hybrid_mla_rwkv7_retention_step'


def rmsnorm(x, g):
    xf = x.astype(jnp.float32)
    y = xf * lax.rsqrt(jnp.mean(xf * xf, axis=-1, keepdims=True) + NORM_EPS)
    return (y * g.astype(jnp.float32)).astype(x.dtype)


def ada_modulate(x, g, shift, scale):
    return rmsnorm(x, g) * (1 + scale[:, None, :]) + shift[:, None, :]


def rope(x, pos):
    half = x.shape[-1] // 2
    inv = ROPE_BASE ** (-jnp.arange(half, dtype=jnp.float32) / half)
    ang = pos.astype(jnp.float32)[:, None] * inv[None, :]
    shape = (1, pos.shape[0]) + (1,) * (x.ndim - 3) + (half,)
    cos = jnp.cos(ang).reshape(shape).astype(x.dtype)
    sin = jnp.sin(ang).reshape(shape).astype(x.dtype)
    x1, x2 = x[..., :half], x[..., half:]
    return jnp.concatenate([x1 * cos - x2 * sin, x1 * sin + x2 * cos], axis=-1)


def mla_prompt(q_lat, q_rope, lat, kr):
    b, t, h, r = q_lat.shape
    nb = t // Q_BLOCK
    qb = q_lat.reshape(b, nb, Q_BLOCK, h, r).swapaxes(0, 1)
    qrb = q_rope.reshape(b, nb, Q_BLOCK, h, MLA_ROPE).swapaxes(0, 1)
    kpos = jnp.arange(t)

    def block(args):
        ql, qr, i = args
        s = jnp.einsum('bqhr,bkr->bhqk', ql, lat) + jnp.einsum('bqhd,bkd->bhqk', qr, kr)
        s = s.astype(jnp.float32) * MLA_SCALE
        qpos = i * Q_BLOCK + jnp.arange(Q_BLOCK)
        s = jnp.where(kpos[None, :] <= qpos[:, None], s, -jnp.inf)
        p = jax.nn.softmax(s, axis=-1)
        return jnp.einsum('bhqk,bkr->bqhr', p.astype(lat.dtype), lat)

    o = lax.map(block, (qb, qrb, jnp.arange(nb)))
    return o.swapaxes(0, 1).reshape(b, t, h, r)


def mla_sample(q_lat, q_rope, lat, kr, cache_lat, cache_kr, layer, page_table):
    f32 = jnp.float32
    t = q_lat.shape[1]
    ql = q_lat.astype(f32)
    qr = q_rope.astype(f32)

    def scores(kl, kp):
        return (jnp.einsum('bqhr,bkr->bhqk', ql, kl) + jnp.einsum('bqhd,bkd->bhqk', qr, kp)) * MLA_SCALE

    latf = lat.astype(f32)
    idx = jnp.arange(t)
    s = jnp.where(idx[:, None] >= idx[None, :], scores(latf, kr.astype(f32)), -jnp.inf)
    m = jnp.max(s, axis=-1)
    p = jnp.exp(s - m[..., None])
    l = jnp.sum(p, axis=-1)
    acc = jnp.einsum('bhqk,bkr->bhqr', p, latf)

    def page_step(carry, pidx):
        m, l, acc = carry
        kl = cache_lat[layer, pidx].astype(f32)
        kp = cache_kr[layer, pidx].astype(f32)
        s = scores(kl, kp)
        m_new = jnp.maximum(m, jnp.max(s, axis=-1))
        alpha = jnp.exp(m - m_new)
        p = jnp.exp(s - m_new[..., None])
        l = l * alpha + jnp.sum(p, axis=-1)
        acc = acc * alpha[..., None] + jnp.einsum('bhqk,bkr->bhqr', p, kl)
        return (m_new, l, acc), None

    (m, l, acc), _ = lax.scan(page_step, (m, l, acc), page_table.T)
    return (acc / l[..., None]).swapaxes(1, 2)


def rwkv_scan(r, w, k, v, kk, a, s0):
    def step(s, inp):
        r_t, w_t, k_t, v_t, kk_t, a_t = inp
        sa = jnp.einsum('bhvk,bhk->bhv', s, -kk_t)
        s = s * w_t[:, :, None, :] + sa[..., None] * (kk_t * a_t)[:, :, None, :] + v_t[..., None] * k_t[:, :, None, :]
        return s, jnp.einsum('bhvk,bhk->bhv', s, r_t)

    xs = tuple(jnp.moveaxis(u, 1, 0) for u in (r, w, k, v, kk, a))
    s, ys = lax.scan(step, s0, xs)
    return jnp.moveaxis(ys, 0, 1), s


def even_mixer(h, pos, prm, past):
    (w_in, g_kv, w_uk, w_uv, mu, w0, w_w2, a0, w_a2, w_g2, k_k, k_a, r_k, ln_w, ln_b, w_out) = prm
    f32 = jnp.float32
    b, t, _ = h.shape
    z = h @ w_in
    zq = z[..., :MLA_Q_W]
    zkv = z[..., MLA_Q_W:MLA_Q_W + MLA_KV_W]
    zr = z[..., MLA_Q_W + MLA_KV_W:]
    q = zq.reshape(b, t, MLA_HEADS, MLA_NOPE + MLA_ROPE)
    q_lat = jnp.einsum('bthn,rhn->bthr', q[..., :MLA_NOPE], w_uk)
    q_rope = rope(q[..., MLA_NOPE:], pos)
    lat = rmsnorm(zkv[..., :KV_RANK], g_kv)
    kr = rope(zkv[..., KV_RANK:], pos)
    if past is None:
        o_lat = mla_prompt(q_lat, q_rope, lat, kr)
        s0 = jnp.zeros((b, RW_HEADS, RW_N, RW_N), f32)
        shift_prev = jnp.zeros((b, RW_SHIFT_W), zr.dtype)
    else:
        cache_lat, cache_kr, layer, page_table, s0, shift_prev = past
        o_lat = mla_sample(q_lat, q_rope, lat, kr, cache_lat, cache_kr, layer, page_table)
    mla_out = jnp.einsum('bthr,rhv->bthv', o_lat.astype(h.dtype), w_uv).reshape(b, t, MLA_HEADS * MLA_V)
    prev = jnp.concatenate([shift_prev[:, None, :].astype(zr.dtype), zr[:, :-1]], axis=1)
    zs = zr + (prev - zr) * mu
    o3 = 3 * RW_W
    o4 = o3 + RW_DECAY_LORA
    o5 = o4 + RW_A_LORA
    xr, xk, xv = zs[..., :RW_W], zs[..., RW_W:2 * RW_W], zs[..., 2 * RW_W:o3]
    xw, xa, xg = zs[..., o3:o4], zs[..., o4:o5], zs[..., o5:]
    w_log = -jax.nn.softplus(-(w0 + jnp.tanh(xw) @ w_w2).astype(f32)) - 0.5
    decay = jnp.exp(-jnp.exp(w_log))
    a = jax.nn.sigmoid((a0 + xa @ w_a2).astype(f32))
    g = (jax.nn.sigmoid(xg) @ w_g2).astype(f32)

    def heads(u):
        return u.astype(f32).reshape(b, t, RW_HEADS, RW_N)

    kk = heads(xk * k_k)
    kk = kk / jnp.maximum(jnp.sqrt(jnp.sum(kk * kk, axis=-1, keepdims=True)), 1e-12)
    k = heads(xk.astype(f32) * (1 + (a - 1) * k_a))
    r = heads(xr)
    v = heads(xv)
    y, s_new = rwkv_scan(r, heads(decay), k, v, kk, heads(a), s0.astype(f32))
    mean = jnp.mean(y, axis=-1, keepdims=True)
    var = jnp.mean(jnp.square(y - mean), axis=-1, keepdims=True)
    yn = ((y - mean) * lax.rsqrt(var + RW_GN_EPS)).reshape(b, t, RW_W) * ln_w + ln_b
    bonus = (jnp.sum(r * k * r_k, axis=-1, keepdims=True) * v).reshape(b, t, RW_W)
    rw_out = ((yn + bonus) * g).astype(h.dtype)
    out = jnp.concatenate([mla_out, rw_out], axis=-1) @ w_out
    return out, (lat, kr, s_new, zr[:, -1])


def ret_log_gamma():
    return jnp.log(1 - 2.0 ** (-5.0 - jnp.arange(RET_HEADS, dtype=jnp.float32)))


def retention_chunk(q, k, v, s):
    f32 = jnp.float32
    L = q.shape[1]
    lg = ret_log_gamma()
    idx = jnp.arange(L, dtype=f32)
    diff = idx[:, None] - idx[None, :]
    dmask = jnp.where(diff >= 0, jnp.exp(jnp.maximum(diff, 0.0)[None] * lg[:, None, None]), 0.0)
    qf, kf, vf = q.astype(f32), k.astype(f32), v.astype(f32)
    sc = jnp.einsum('bqhd,bkhd->bhqk', qf, kf) * dmask[None]
    inner = jnp.einsum('bhqk,bkhv->bqhv', sc, vf)
    cross = jnp.einsum('bqhd,bhdv->bqhv', qf, s) * jnp.exp((idx + 1)[:, None] * lg[None, :])[None, :, :, None]
    kdec = kf * jnp.exp((L - 1 - idx)[:, None] * lg[None, :])[None, :, :, None]
    s_new = s * jnp.exp(L * lg)[None, :, None, None] + jnp.einsum('bkhd,bkhv->bhdv', kdec, vf)
    return inner + cross, s_new


def retention_prompt(q, k, v):
    b, t = q.shape[0], q.shape[1]
    nc = t // RET_CHUNK
    xs = tuple(u.reshape(b, nc, RET_CHUNK, RET_HEADS, u.shape[-1]).swapaxes(0, 1) for u in (q, k, v))

    def chunk_step(s, inp):
        o, s = retention_chunk(inp[0], inp[1], inp[2], s)
        return s, o

    s0 = jnp.zeros((b, RET_HEADS, RET_DK, RET_DV), jnp.float32)
    s, o = lax.scan(chunk_step, s0, xs)
    return o.swapaxes(0, 1).reshape(b, t, RET_HEADS, RET_DV), s


def odd_mixer(h, pos, prm, s0):
    w_in, w_out = prm
    b, t, _ = h.shape
    qw = RET_HEADS * RET_DK
    vw = RET_HEADS * RET_DV
    z = h @ w_in
    q = rope(z[..., :qw].reshape(b, t, RET_HEADS, RET_DK), pos)
    k = rope(z[..., qw:2 * qw].reshape(b, t, RET_HEADS, RET_DK), pos) * RET_DK ** -0.5
    v = z[..., 2 * qw:2 * qw + vw].reshape(b, t, RET_HEADS, RET_DV)
    g = z[..., 2 * qw + vw:]
    if s0 is None:
        o, s_new = retention_prompt(q, k, v)
    else:
        o, s_new = retention_chunk(q, k, v, s0.astype(jnp.float32))
    o = o * lax.rsqrt(jnp.mean(o * o, axis=-1, keepdims=True) + NORM_EPS)
    out = (o.reshape(b, t, vw).astype(h.dtype) * jax.nn.silu(g)) @ w_out
    return out, s_new


def sq_relu_mlp(h, w1, w2):
    return jnp.square(jax.nn.relu(h @ w1)) @ w2


def setup_inputs(seed: int = 0) -> dict:
    key = jax.random.key(seed)
    ks = iter(jax.random.split(key, 48))
    f32 = jnp.float32
    D = D_MODEL

    def nrm(shape, scale):
        return scale * jax.random.normal(next(ks), shape, f32)

    def near_one(shape):
        return 1.0 + 0.02 * jax.random.normal(next(ks), shape, f32)

    n_pages = PAST_LEN // PAGE_SIZE
    n_used = DEC_BATCH * n_pages
    n_pool = (5 * n_used + 3) // 4
    page_table = jax.random.permutation(next(ks), n_pool)[:n_used].reshape(DEC_BATCH, n_pages).astype(jnp.int32)
    return {
        'x_prompt': nrm((BATCH, SEQ, D), 1.0),
        'x_sample': nrm((DEC_BATCH, DEC_SEQ, D), 1.0),
        'c_prompt': nrm((BATCH, D), 1.0),
        'c_sample': nrm((DEC_BATCH, D), 1.0),
        'cache_kv_latent': nrm((N_EVEN, n_pool, PAGE_SIZE, KV_RANK), 1.0),
        'cache_k_rope': nrm((N_EVEN, n_pool, PAGE_SIZE, MLA_ROPE), 1.0),
        'page_table': page_table,
        'state_rwkv': nrm((N_EVEN, DEC_BATCH, RW_HEADS, RW_N, RW_N), 0.5),
        'state_rwkv_shift': nrm((N_EVEN, DEC_BATCH, RW_SHIFT_W), 1.0),
        'state_ret': nrm((N_ODD, DEC_BATCH, RET_HEADS, RET_DK, RET_DV), 0.5),
        'w_ada': nrm((DEPTH, D, 6 * D), 0.3 * D ** -0.5),
        'b_ada': nrm((DEPTH, 6 * D), 0.02),
        'g_norm_mix': near_one((DEPTH, D)),
        'g_norm_mlp': near_one((DEPTH, D)),
        'g_final': near_one((D,)),
        'w_in_even': nrm((N_EVEN, D, EVEN_IN_W), D ** -0.5),
        'g_kv': near_one((N_EVEN, KV_RANK)),
        'w_uk': nrm((N_EVEN, KV_RANK, MLA_HEADS, MLA_NOPE), KV_RANK ** -0.5),
        'w_uv': nrm((N_EVEN, KV_RANK, MLA_HEADS, MLA_V), KV_RANK ** -0.5),
        'rw_mu': jax.random.uniform(next(ks), (N_EVEN, RW_SHIFT_W), f32),
        'rw_w0': jax.random.uniform(next(ks), (N_EVEN, RW_W), f32, -3.0, 1.0),
        'rw_w2': nrm((N_EVEN, RW_DECAY_LORA, RW_W), 0.1),
        'rw_a0': nrm((N_EVEN, RW_W), 0.1),
        'rw_a2': nrm((N_EVEN, RW_A_LORA, RW_W), 0.1),
        'rw_g2': nrm((N_EVEN, RW_G_LORA, RW_W), RW_G_LORA ** -0.5),
        'rw_k_k': 0.85 + nrm((N_EVEN, RW_W), 0.02),
        'rw_k_a': near_one((N_EVEN, RW_W)),
        'rw_r_k': nrm((N_EVEN, RW_HEADS, RW_N), 0.1),
        'rw_ln_w': near_one((N_EVEN, RW_W)),
        'rw_ln_b': nrm((N_EVEN, RW_W), 0.02),
        'w_out_even': nrm((N_EVEN, EVEN_MIX_W, D), EVEN_MIX_W ** -0.5),
        'w_in_odd': nrm((N_ODD, D, ODD_IN_W), D ** -0.5),
        'w_out_odd': nrm((N_ODD, ODD_MIX_W, D), ODD_MIX_W ** -0.5),
        'w_ff1': nrm((DEPTH, D, D_FF), D ** -0.5),
        'w_ff2': nrm((DEPTH, D_FF, D), D_FF ** -0.5),
    }


def reference(x_prompt, x_sample, c_prompt, c_sample, cache_kv_latent, cache_k_rope, page_table,
              state_rwkv, state_rwkv_shift, state_ret, w_ada, b_ada, g_norm_mix, g_norm_mlp, g_final,
              w_in_even, g_kv, w_uk, w_uv, rw_mu, rw_w0, rw_w2, rw_a0, rw_a2, rw_g2, rw_k_k, rw_k_a,
              rw_r_k, rw_ln_w, rw_ln_b, w_out_even, w_in_odd, w_out_odd, w_ff1, w_ff2):
    pos_p = jnp.arange(x_prompt.shape[1])
    pos_s = PAST_LEN + jnp.arange(x_sample.shape[1])
    xp, xs = x_prompt, x_sample
    lat_p, kr_p, rw_p, sh_p, ret_p = [], [], [], [], []
    lat_s, kr_s, rw_s, sh_s, ret_s = [], [], [], [], []
    for l in range(DEPTH):
        sh1p, sc1p, gt1p, sh2p, sc2p, gt2p = jnp.split(c_prompt @ w_ada[l] + b_ada[l], 6, axis=-1)
        sh1s, sc1s, gt1s, sh2s, sc2s, gt2s = jnp.split(c_sample @ w_ada[l] + b_ada[l], 6, axis=-1)
        hp = ada_modulate(xp, g_norm_mix[l], sh1p, sc1p)
        hs = ada_modulate(xs, g_norm_mix[l], sh1s, sc1s)
        i = l // 2
        if l % 2 == 0:
            prm = (w_in_even[i], g_kv[i], w_uk[i], w_uv[i], rw_mu[i], rw_w0[i], rw_w2[i], rw_a0[i], rw_a2[i],
                   rw_g2[i], rw_k_k[i], rw_k_a[i], rw_r_k[i], rw_ln_w[i], rw_ln_b[i], w_out_even[i])
            op, (la, kr, st, sh) = even_mixer(hp, pos_p, prm, None)
            lat_p.append(la); kr_p.append(kr); rw_p.append(st); sh_p.append(sh)
            past = (cache_kv_latent, cache_k_rope, i, page_table, state_rwkv[i], state_rwkv_shift[i])
            os_, (la, kr, st, sh) = even_mixer(hs, pos_s, prm, past)
            lat_s.append(la); kr_s.append(kr); rw_s.append(st); sh_s.append(sh)
        else:
            prm = (w_in_odd[i], w_out_odd[i])
            op, st = odd_mixer(hp, pos_p, prm, None)
            ret_p.append(st)
            os_, st = odd_mixer(hs, pos_s, prm, state_ret[i])
            ret_s.append(st)
        xp = xp + gt1p[:, None, :] * op
        xs = xs + gt1s[:, None, :] * os_
        xp = xp + gt2p[:, None, :] * sq_relu_mlp(ada_modulate(xp, g_norm_mlp[l], sh2p, sc2p), w_ff1[l], w_ff2[l])
        xs = xs + gt2s[:, None, :] * sq_relu_mlp(ada_modulate(xs, g_norm_mlp[l], sh2s, sc2s), w_ff1[l], w_ff2[l])
    y_prompt = rmsnorm(xp, g_final)
    y_sample = rmsnorm(xs, g_final)
    return (y_prompt, y_sample,
            jnp.stack(lat_p), jnp.stack(kr_p), jnp.stack(rw_p), jnp.stack(sh_p), jnp.stack(ret_p),
            jnp.stack(lat_s), jnp.stack(kr_s), jnp.stack(rw_s), jnp.stack(sh_s), jnp.stack(ret_s))
```

```python
import functools
import math

import jax
import jax.numpy as jnp
from jax import lax
from jax.experimental import pallas as pl
from jax.experimental.pallas import tpu as pltpu

F32 = jnp.float32
BF16 = jnp.bfloat16
HIGHEST = lax.Precision.HIGHEST

D_MODEL = 1024
PAGE_SIZE = 128
MLA_HEADS = 8
MLA_NOPE = 64
MLA_ROPE = 32
MLA_V = 64
KV_RANK = 256
MLA_QK = KV_RANK + MLA_ROPE
MLA_SCALE = (MLA_NOPE + MLA_ROPE) ** -0.5
RW_HEADS = 8
RW_N = 64
RW_W = RW_HEADS * RW_N
RW_DECAY_LORA = 64
RW_A_LORA = 64
RW_G_LORA = 128
RW_SHIFT_W = 3 * RW_W + RW_DECAY_LORA + RW_A_LORA + RW_G_LORA
RW_GN_EPS = 64e-5
RW_CHUNK = 64
RET_HEADS = 4
RET_DK = 256
RET_DV = 512
RET_CHUNK = 128
D_FF = 4 * D_MODEL
ROPE_BASE = 10000.0
NORM_EPS = 1e-6
MIB = 1024 * 1024


def _params(sem, vmem_mib=48):
    return pltpu.CompilerParams(dimension_semantics=sem, vmem_limit_bytes=vmem_mib * MIB)


def _rms(x, g):
    return x * lax.rsqrt(jnp.mean(x * x, axis=-1, keepdims=True) + NORM_EPS) * g


def _nt_dot(a, b, precision=None):
    return lax.dot_general(a, b, (((1,), (1,)), ((), ())), precision=precision, preferred_element_type=F32)


def _tn_dot(a, b, precision=None):
    return lax.dot_general(a, b, (((0,), (0,)), ((), ())), precision=precision, preferred_element_type=F32)


def _dot(a, b, precision=None):
    return jnp.dot(a, b, precision=precision, preferred_element_type=F32)


def _mod_spec(mod, tm, nmid):
    if mod.shape[1] == 1:
        if nmid == 2:
            return pl.BlockSpec((1, 1, mod.shape[2]), lambda b, m, j: (b, 0, 0))
        return pl.BlockSpec((1, 1, mod.shape[2]), lambda b, m: (b, 0, 0))
    if nmid == 2:
        return pl.BlockSpec((1, tm, mod.shape[2]), lambda b, m, j: (b, m, 0))
    return pl.BlockSpec((1, tm, mod.shape[2]), lambda b, m: (b, m, 0))


def _ada_kernel(c_ref, w_ref, b_ref, o_ref):
    o_ref[0] = _dot(c_ref[...].astype(BF16), w_ref[0].astype(BF16)) + b_ref[0]


def ada_proj(c, w_ada, b_ada):
    nl, d, n = w_ada.shape
    r = c.shape[0]
    tn = 1536
    return pl.pallas_call(
        _ada_kernel,
        grid=(nl, n // tn),
        in_specs=[pl.BlockSpec((r, d), lambda l, j: (0, 0)),
                  pl.BlockSpec((1, d, tn), lambda l, j: (l, 0, j)),
                  pl.BlockSpec((1, 1, tn), lambda l, j: (l, 0, j))],
        out_specs=pl.BlockSpec((1, r, tn), lambda l, j: (l, 0, j)),
        out_shape=jax.ShapeDtypeStruct((nl, r, n), F32),
        compiler_params=_params(("parallel", "parallel")),
        name="ada_proj",
    )(c, w_ada, b_ada.reshape(nl, 1, n))


def _nmm_split_kernel(x_ref, g_ref, sh_ref, sc_ref, w_ref, *o_refs, splits):
    h = (_rms(x_ref[0], g_ref[...]) * (1.0 + sc_ref[0]) + sh_ref[0]).astype(BF16)
    off = 0
    for o_ref, n in zip(o_refs, splits):
        o_ref[0] = _dot(h, w_ref[:, off:off + n])
        off += n


def norm_mod_matmul_split(x, g, shift, scale, w, splits, tm):
    nb, m, d = x.shape
    n = w.shape[1]
    return pl.pallas_call(
        functools.partial(_nmm_split_kernel, splits=splits),
        grid=(nb, m // tm),
        in_specs=[pl.BlockSpec((1, tm, d), lambda b, i: (b, i, 0)),
                  pl.BlockSpec((1, d), lambda b, i: (0, 0)),
                  _mod_spec(shift, tm, 1), _mod_spec(scale, tm, 1),
                  pl.BlockSpec((d, n), lambda b, i: (0, 0))],
        out_specs=[pl.BlockSpec((1, tm, s), lambda b, i: (b, i, 0)) for s in splits],
        out_shape=[jax.ShapeDtypeStruct((nb, m, s), F32) for s in splits],
        compiler_params=_params(("parallel", "parallel")),
        name="norm_mod_matmul_split",
    )(x, g.reshape(1, d), shift, scale, w)


def _nmm_kernel(x_ref, g_ref, sh_ref, sc_ref, w_ref, o_ref):
    h = (_rms(x_ref[0], g_ref[...]) * (1.0 + sc_ref[0]) + sh_ref[0]).astype(BF16)
    o_ref[0] = _dot(h, w_ref[...])


def norm_mod_matmul(x, g, shift, scale, w, tm, tn):
    nb, m, d = x.shape
    n = w.shape[1]
    return pl.pallas_call(
        _nmm_kernel,
        grid=(nb, m // tm, n // tn),
        in_specs=[pl.BlockSpec((1, tm, d), lambda b, i, j: (b, i, 0)),
                  pl.BlockSpec((1, d), lambda b, i, j: (0, 0)),
                  _mod_spec(shift, tm, 2), _mod_spec(scale, tm, 2),
                  pl.BlockSpec((d, tn), lambda b, i, j: (0, j))],
        out_specs=pl.BlockSpec((1, tm, tn), lambda b, i, j: (b, i, j)),
        out_shape=jax.ShapeDtypeStruct((nb, m, n), F32),
        compiler_params=_params(("parallel", "parallel", "arbitrary")),
        name="norm_mod_matmul",
    )(x, g.reshape(1, d), shift, scale, w)


def _mgr_kernel(*refs, n_pairs):
    a_refs = refs[:n_pairs]
    w_refs = refs[n_pairs:2 * n_pairs]
    res_ref, gt_ref, o_ref = refs[2 * n_pairs:]
    acc = _dot(a_refs[0][0], w_refs[0][...])
    for a_ref, w_ref in zip(a_refs[1:], w_refs[1:]):
        acc = acc + _dot(a_ref[0], w_ref[...])
    o_ref[0] = res_ref[0] + gt_ref[0] * acc


def matmul_gate_res(a_list, w_list, res, gate, tm):
    nb, m, d = res.shape
    n_pairs = len(a_list)
    in_specs = [pl.BlockSpec((1, tm, a.shape[2]), lambda b, i: (b, i, 0)) for a in a_list]
    in_specs += [pl.BlockSpec(w.shape, lambda b, i: (0, 0)) for w in w_list]
    in_specs += [pl.BlockSpec((1, tm, d), lambda b, i: (b, i, 0)), _mod_spec(gate, tm, 1)]
    return pl.pallas_call(
        functools.partial(_mgr_kernel, n_pairs=n_pairs),
        grid=(nb, m // tm),
        in_specs=in_specs,
        out_specs=pl.BlockSpec((1, tm, d), lambda b, i: (b, i, 0)),
        out_shape=jax.ShapeDtypeStruct((nb, m, d), F32),
        compiler_params=_params(("parallel", "parallel")),
        name="matmul_gate_res",
    )(*a_list, *w_list, res, gate)


def _mlp_kernel(x_ref, g_ref, sh_ref, sc_ref, gt_ref, w1_ref, w2_ref, gf_ref, o_ref, h_scr, acc_scr, *, final):
    f = pl.program_id(2)

    @pl.when(f == 0)
    def _():
        h_scr[...] = (_rms(x_ref[0], g_ref[...]) * (1.0 + sc_ref[0]) + sh_ref[0]).astype(BF16)
        acc_scr[...] = jnp.zeros_like(acc_scr)

    a = _dot(h_scr[...], w1_ref[...])
    a = jnp.square(jnp.maximum(a, 0.0)).astype(BF16)
    acc_scr[...] += _dot(a, w2_ref[...])

    @pl.when(f == pl.num_programs(2) - 1)
    def _():
        y = x_ref[0] + gt_ref[0] * acc_scr[...]
        if final:
            y = _rms(y, gf_ref[...])
        o_ref[0] = y


def mlp_block(x, g, shift, scale, gate, w1, w2, g_final, final, tm, tf):
    nb, m, d = x.shape
    dff = w1.shape[1]
    return pl.pallas_call(
        functools.partial(_mlp_kernel, final=final),
        grid=(nb, m // tm, dff // tf),
        in_specs=[pl.BlockSpec((1, tm, d), lambda b, i, f: (b, i, 0)),
                  pl.BlockSpec((1, d), lambda b, i, f: (0, 0)),
                  _mod_spec(shift, tm, 2), _mod_spec(scale, tm, 2), _mod_spec(gate, tm, 2),
                  pl.BlockSpec((d, tf), lambda b, i, f: (0, f)),
                  pl.BlockSpec((tf, d), lambda b, i, f: (f, 0)),
                  pl.BlockSpec((1, d), lambda b, i, f: (0, 0))],
        out_specs=pl.BlockSpec((1, tm, d), lambda b, i, f: (b, i, 0)),
        out_shape=jax.ShapeDtypeStruct((nb, m, d), F32),
        scratch_shapes=[pltpu.VMEM((tm, d), BF16), pltpu.VMEM((tm, d), F32)],
        compiler_params=_params(("parallel", "parallel", "arbitrary")),
        name="mlp_block",
    )(x, g.reshape(1, d), shift, scale, gate, w1, w2, g_final.reshape(1, d))


def _rope32(x, cf, sf):
    half = MLA_ROPE // 2
    sw = jnp.concatenate([x[:, half:], x[:, :half]], axis=1)
    return x * cf + sw * sf


def _kvprep_kernel(zkv_ref, g_ref, cf_ref, sf_ref, lat_ref, kr_ref, kcat_ref):
    z = zkv_ref[0]
    lat = _rms(z[:, :KV_RANK], g_ref[...])
    kr = _rope32(z[:, KV_RANK:], cf_ref[0], sf_ref[0])
    lat_ref[0] = lat
    kr_ref[0] = kr
    kcat_ref[0, :, :KV_RANK] = lat.astype(BF16)
    kcat_ref[0, :, KV_RANK:] = kr.astype(BF16)


def kv_prep(zkv, g_kv, cf, sf, tm):
    nb, m, _ = zkv.shape
    return pl.pallas_call(
        _kvprep_kernel,
        grid=(nb, m // tm),
        in_specs=[pl.BlockSpec((1, tm, MLA_QK), lambda b, i: (b, i, 0)),
                  pl.BlockSpec((1, KV_RANK), lambda b, i: (0, 0)),
                  pl.BlockSpec((1, tm, MLA_ROPE), lambda b, i: (0, i, 0)),
                  pl.BlockSpec((1, tm, MLA_ROPE), lambda b, i: (0, i, 0))],
        out_specs=[pl.BlockSpec((1, tm, KV_RANK), lambda b, i: (b, i, 0)),
                   pl.BlockSpec((1, tm, MLA_ROPE), lambda b, i: (b, i, 0)),
                   pl.BlockSpec((1, tm, MLA_QK), lambda b, i: (b, i, 0))],
        out_shape=[jax.ShapeDtypeStruct((nb, m, KV_RANK), F32),
                   jax.ShapeDtypeStruct((nb, m, MLA_ROPE), F32),
                   jax.ShapeDtypeStruct((nb, m, MLA_QK), BF16)],
        compiler_params=_params(("parallel", "parallel")),
        name="kv_prep",
    )(zkv, g_kv.reshape(1, KV_RANK), cf, sf)


def _qprep_kernel(zq_ref, wuk_ref, cf_ref, sf_ref, o_ref):
    z = zq_ref[0]
    cf = cf_ref[0]
    sf = sf_ref[0]
    nope_w = MLA_HEADS * MLA_NOPE
    for h in range(MLA_HEADS):
        qn = z[:, h * MLA_NOPE:(h + 1) * MLA_NOPE].astype(BF16)
        ql = _dot(qn, wuk_ref[h]) * MLA_SCALE
        qr = _rope32(z[:, nope_w + h * MLA_ROPE:nope_w + (h + 1) * MLA_ROPE], cf, sf) * MLA_SCALE
        o_ref[0, h, :, :KV_RANK] = ql.astype(BF16)
        o_ref[0, h, :, KV_RANK:] = qr.astype(BF16)


def q_prep(zq, wuk_t, cf, sf, tm):
    nb, m, w = zq.shape
    return pl.pallas_call(
        _qprep_kernel,
        grid=(nb, m // tm),
        in_specs=[pl.BlockSpec((1, tm, w), lambda b, i: (b, i, 0)),
                  pl.BlockSpec((MLA_HEADS, MLA_NOPE, KV_RANK), lambda b, i: (0, 0, 0)),
                  pl.BlockSpec((1, tm, MLA_ROPE), lambda b, i: (0, i, 0)),
                  pl.BlockSpec((1, tm, MLA_ROPE), lambda b, i: (0, i, 0))],
        out_specs=pl.BlockSpec((1, MLA_HEADS, tm, MLA_QK), lambda b, i: (b, 0, i, 0)),
        out_shape=jax.ShapeDtypeStruct((nb, MLA_HEADS, m, MLA_QK), BF16),
        compiler_params=_params(("parallel", "parallel")),
        name="q_prep",
    )(zq, wuk_t, cf, sf)


def _mla_prompt_kernel(q_ref, k_ref, wuv_ref, o_ref, m_scr, l_scr, acc_scr, *, tq, tk):
    qi = pl.program_id(1)
    ki = pl.program_id(2)
    rows = MLA_HEADS * tq

    @pl.when(ki == 0)
    def _():
        m_scr[...] = jnp.full_like(m_scr, -jnp.inf)
        l_scr[...] = jnp.zeros_like(l_scr)
        acc_scr[...] = jnp.zeros_like(acc_scr)

    @pl.when(ki * tk <= qi * tq + (tq - 1))
    def _():
        q = q_ref[0].reshape(rows, MLA_QK)
        kc = k_ref[0]
        s = _nt_dot(q, kc)
        row = lax.broadcasted_iota(jnp.int32, s.shape, 0)
        col = lax.broadcasted_iota(jnp.int32, s.shape, 1)
        qpos = qi * tq + jnp.bitwise_and(row, tq - 1)
        s = jnp.where(ki * tk + col <= qpos, s, -jnp.inf)
        m_prev = m_scr[...]
        m_new = jnp.maximum(m_prev, jnp.max(s, axis=1, keepdims=True))
        alpha = jnp.exp(m_prev - m_new)
        p = jnp.exp(s - m_new)
        l_scr[...] = alpha * l_scr[...] + jnp.sum(p, axis=1, keepdims=True)
        acc_scr[...] = alpha * acc_scr[...] + _dot(p.astype(BF16), kc[:, :KV_RANK])
        m_scr[...] = m_new

    @pl.when(ki == pl.num_programs(2) - 1)
    def _():
        o = (acc_scr[...] / l_scr[...]).astype(BF16)
        outs = [_dot(o[h * tq:(h + 1) * tq], wuv_ref[h]) for h in range(MLA_HEADS)]
        o_ref[0] = jnp.concatenate(outs, axis=1).astype(BF16)


def mla_prompt(qcat, kcat, wuv_h, tq, tk):
    nb, _, t, _ = qcat.shape
    tk = min(tk, t)
    rows = MLA_HEADS * tq
    return pl.pallas_call(
        functools.partial(_mla_prompt_kernel, tq=tq, tk=tk),
        grid=(nb, t // tq, t // tk),
        in_specs=[pl.BlockSpec((1, MLA_HEADS, tq, MLA_QK), lambda b, i, j: (b, 0, i, 0)),
                  pl.BlockSpec((1, tk, MLA_QK), lambda b, i, j: (b, jnp.minimum(j, (i * tq + tq - 1) // tk), 0)),
                  pl.BlockSpec((MLA_HEADS, KV_RANK, MLA_V), lambda b, i, j: (0, 0, 0))],
        out_specs=pl.BlockSpec((1, tq, MLA_HEADS * MLA_V), lambda b, i, j: (b, i, 0)),
        out_shape=jax.ShapeDtypeStruct((nb, t, MLA_HEADS * MLA_V), BF16),
        scratch_shapes=[pltpu.VMEM((rows, 1), F32), pltpu.VMEM((rows, 1), F32), pltpu.VMEM((rows, KV_RANK), F32)],
        compiler_params=_params(("parallel", "parallel", "arbitrary")),
        name="mla_prompt",
    )(qcat, kcat, wuv_h)


def _mla_sample_kernel(pt_ref, q_ref, kn_ref, wuv_ref, *rest, n_pg, t_new):
    kl_refs = rest[:n_pg]
    kp_refs = rest[n_pg:2 * n_pg]
    o_ref = rest[2 * n_pg]
    m_scr, l_scr, acc_scr = rest[2 * n_pg + 1:]
    j = pl.program_id(1)
    rows = t_new * MLA_HEADS

    @pl.when(j == 0)
    def _():
        m_scr[...] = jnp.full_like(m_scr, -jnp.inf)
        l_scr[...] = jnp.zeros_like(l_scr)
        acc_scr[...] = jnp.zeros_like(acc_scr)

    q = q_ref[0]
    ql = q[:, :KV_RANK]
    qr = q[:, KV_RANK:]
    kls = [kl_refs[i][...].astype(BF16) for i in range(n_pg)]
    ss = [_nt_dot(ql, kls[i]) + _nt_dot(qr, kp_refs[i][...].astype(BF16)) for i in range(n_pg)]
    s = jnp.concatenate(ss, axis=1)
    m_prev = m_scr[...]
    m_new = jnp.maximum(m_prev, jnp.max(s, axis=1, keepdims=True))
    alpha = jnp.exp(m_prev - m_new)
    p = jnp.exp(s - m_new).astype(BF16)
    l_scr[...] = alpha * l_scr[...] + jnp.sum(p.astype(F32), axis=1, keepdims=True)
    pv = _dot(p[:, :PAGE_SIZE], kls[0])
    for i in range(1, n_pg):
        pv = pv + _dot(p[:, i * PAGE_SIZE:(i + 1) * PAGE_SIZE], kls[i])
    acc_scr[...] = alpha * acc_scr[...] + pv
    m_scr[...] = m_new

    @pl.when(j == pl.num_programs(1) - 1)
    def _():
        qf = q.astype(F32)
        kn = kn_ref[0]
        trow = lax.broadcasted_iota(jnp.int32, (rows, 1), 0) // MLA_HEADS
        cols = []
        for jj in range(t_new):
            sj = jnp.sum(qf * kn[jj:jj + 1, :], axis=1, keepdims=True)
            cols.append(jnp.where(trow >= jj, sj, -jnp.inf))
        m0 = m_scr[...]
        m1 = m0
        for sj in cols:
            m1 = jnp.maximum(m1, sj)
        a1 = jnp.exp(m0 - m1)
        l1 = a1 * l_scr[...]
        acc1 = a1 * acc_scr[...]
        for jj, sj in enumerate(cols):
            pj = jnp.exp(sj - m1)
            l1 = l1 + pj
            acc1 = acc1 + pj * kn[jj:jj + 1, :KV_RANK]
        o = (acc1 / l1).astype(BF16)
        proj = _dot(o, wuv_ref[...])
        rr = lax.broadcasted_iota(jnp.int32, proj.shape, 0)
        cc = lax.broadcasted_iota(jnp.int32, proj.shape, 1)
        proj = jnp.where(jnp.bitwise_and(rr, MLA_HEADS - 1) == cc // MLA_V, proj, 0.0)
        o_ref[0] = jnp.sum(proj.reshape(t_new, MLA_HEADS, MLA_HEADS * MLA_V), axis=1).astype(BF16)


def mla_sample(q_s, kn_s, wuv_all, cache_lat, cache_kr, layer, page_table):
    nb, rows, _ = q_s.shape
    t_new = rows // MLA_HEADS
    n_pages = page_table.shape[1]
    n_pg = math.gcd(n_pages, 8)
    pt_flat = page_table.reshape(-1)

    def page_map(i):
        return lambda b, j, pt: (layer, pt[b * n_pages + j * n_pg + i], 0, 0)

    in_specs = [pl.BlockSpec((1, rows, MLA_QK), lambda b, j, pt: (b, 0, 0)),
                pl.BlockSpec((1, t_new, MLA_QK), lambda b, j, pt: (b, 0, 0)),
                pl.BlockSpec((KV_RANK, MLA_HEADS * MLA_V), lambda b, j, pt: (0, 0))]
    in_specs += [pl.BlockSpec((None, None, PAGE_SIZE, KV_RANK), page_map(i)) for i in range(n_pg)]
    in_specs += [pl.BlockSpec((None, None, PAGE_SIZE, MLA_ROPE), page_map(i)) for i in range(n_pg)]
    grid_spec = pltpu.PrefetchScalarGridSpec(
        num_scalar_prefetch=1,
        grid=(nb, n_pages // n_pg),
        in_specs=in_specs,
        out_specs=pl.BlockSpec((1, t_new, MLA_HEADS * MLA_V), lambda b, j, pt: (b, 0, 0)),
        scratch_shapes=[pltpu.VMEM((rows, 1), F32), pltpu.VMEM((rows, 1), F32), pltpu.VMEM((rows, KV_RANK), F32)],
    )
    return pl.pallas_call(
        functools.partial(_mla_sample_kernel, n_pg=n_pg, t_new=t_new),
        grid_spec=grid_spec,
        out_shape=jax.ShapeDtypeStruct((nb, t_new, MLA_HEADS * MLA_V), BF16),
        compiler_params=_params(("parallel", "arbitrary")),
        name="mla_sample",
    )(pt_flat, q_s, kn_s, wuv_all, *([cache_lat] * n_pg), *([cache_kr] * n_pg))


def _rwprep_kernel(zr_ref, pv_ref, mu_ref, w0_ref, ww2_ref, a0_ref, wa2_ref, wg2_ref, kk_ref, ka_ref,
                   r_o, k_o, v_o, kk_o, kka_o, lw_o, g_o):
    zr = zr_ref[0]
    zs = zr + (pv_ref[0] - zr) * mu_ref[...]
    o3 = 3 * RW_W
    o4 = o3 + RW_DECAY_LORA
    o5 = o4 + RW_A_LORA
    xr, xk, xv = zs[:, :RW_W], zs[:, RW_W:2 * RW_W], zs[:, 2 * RW_W:o3]
    xw, xa, xg = zs[:, o3:o4], zs[:, o4:o5], zs[:, o5:]
    wl = w0_ref[...] + _dot(jnp.tanh(xw).astype(BF16), ww2_ref[...])
    w_log = -(jnp.maximum(-wl, 0.0) + jnp.log1p(jnp.exp(-jnp.abs(wl)))) - 0.5
    logw = -jnp.exp(w_log)
    a = jax.nn.sigmoid(a0_ref[...] + _dot(xa.astype(BF16), wa2_ref[...]))
    g = _dot(jax.nn.sigmoid(xg).astype(BF16), wg2_ref[...])
    kkf = xk * kk_ref[...]
    kf = xk * (1.0 + (a - 1.0) * ka_ref[...])
    for h in range(RW_HEADS):
        sl = slice(h * RW_N, (h + 1) * RW_N)
        kkh = kkf[:, sl]
        kkh = kkh / jnp.maximum(jnp.sqrt(jnp.sum(kkh * kkh, axis=1, keepdims=True)), 1e-12)
        r_o[0, h] = xr[:, sl]
        k_o[0, h] = kf[:, sl]
        v_o[0, h] = xv[:, sl]
        kk_o[0, h] = kkh
        kka_o[0, h] = kkh * a[:, sl]
        lw_o[0, h] = logw[:, sl]
        g_o[0, h] = g[:, sl]


def rwkv_prep(zr, prev, mu, w0, w_w2, a0, w_a2, w_g2, k_k, k_a, tm):
    nb, m, w = zr.shape
    vec = lambda n: pl.BlockSpec((1, n), lambda b, i: (0, 0))
    mat = lambda a: pl.BlockSpec(a.shape, lambda b, i: (0, 0))
    out_spec = pl.BlockSpec((1, RW_HEADS, tm, RW_N), lambda b, i: (b, 0, i, 0))
    out_sds = jax.ShapeDtypeStruct((nb, RW_HEADS, m, RW_N), F32)
    return pl.pallas_call(
        _rwprep_kernel,
        grid=(nb, m // tm),
        in_specs=[pl.BlockSpec((1, tm, w), lambda b, i: (b, i, 0)),
                  pl.BlockSpec((1, tm, w), lambda b, i: (b, i, 0)),
                  vec(w), vec(RW_W), mat(w_w2), vec(RW_W), mat(w_a2), mat(w_g2), vec(RW_W), vec(RW_W)],
        out_specs=[out_spec] * 7,
        out_shape=[out_sds] * 7,
        compiler_params=_params(("parallel", "parallel")),
        name="rwkv_prep",
    )(zr, prev, mu.reshape(1, w), w0.reshape(1, RW_W), w_w2, a0.reshape(1, RW_W), w_a2, w_g2,
      k_k.reshape(1, RW_W), k_a.reshape(1, RW_W))


def _rwkv_scan_kernel(r_ref, k_ref, v_ref, kk_ref, kka_ref, lw_ref, g_ref, rk_ref, lnw_ref, lnb_ref, s0_ref,
                      o_ref, st_ref, *, chunk):
    c = pl.program_id(1)

    @pl.when(c == 0)
    def _():
        st_ref[0] = s0_ref[0]

    row = lax.broadcasted_iota(jnp.int32, (chunk, chunk), 0)
    col = lax.broadcasted_iota(jnp.int32, (chunk, chunk), 1)
    strict = col < row
    incl = col <= row
    tri = jnp.where(incl, 1.0, 0.0).astype(F32)
    eye_c = jnp.where(row == col, 1.0, 0.0).astype(F32)
    eye_n = lax.broadcasted_iota(jnp.int32, (RW_N, RW_N), 0) == lax.broadcasted_iota(jnp.int32, (RW_N, RW_N), 1)
    n_double = int(math.log2(chunk)) - 1
    outs = []
    for h in range(RW_HEADS):
        lw = lw_ref[0, h]
        r = r_ref[0, h]
        k = k_ref[0, h]
        v = v_ref[0, h]
        kk = kk_ref[0, h]
        kka = kka_ref[0, h]
        cs = _dot(tri, lw, HIGHEST)
        g_incl = jnp.exp(cs)
        g_prev = jnp.exp(cs - lw)
        g_inv = jnp.exp(-cs)
        cs_last = cs[chunk - 1:chunk, :]
        g_end = jnp.exp(cs_last)
        g_end_col = jnp.exp(jnp.sum(jnp.where(eye_n, jnp.broadcast_to(cs_last, (RW_N, RW_N)), 0.0),
                                    axis=1, keepdims=True))
        at = -kk * g_prev
        bt = kka * g_inv
        kt = k * g_inv
        rt = r * g_incl
        l_ab = jnp.where(strict, _nt_dot(at, bt, HIGHEST), 0.0)
        l_ak = jnp.where(strict, _nt_dot(at, kt, HIGHEST), 0.0)
        m_rb = jnp.where(incl, _nt_dot(rt, bt, HIGHEST), 0.0)
        m_rk = jnp.where(incl, _nt_dot(rt, kt, HIGHEST), 0.0)
        tinv = eye_c + l_ab
        pw = l_ab
        for _ in range(n_double):
            pw = _dot(pw, pw, HIGHEST)
            tinv = tinv + _dot(tinv, pw, HIGHEST)
        st = st_ref[0, h]
        u = _dot(tinv, _dot(at, st, HIGHEST) + _dot(l_ak, v, HIGHEST), HIGHEST)
        y = _dot(rt, st, HIGHEST) + _dot(m_rb, u, HIGHEST) + _dot(m_rk, v, HIGHEST)
        st_ref[0, h] = st * g_end_col + _tn_dot(bt * g_end, u, HIGHEST) + _tn_dot(kt * g_end, v, HIGHEST)
        mean = jnp.mean(y, axis=1, keepdims=True)
        yc = y - mean
        var = jnp.mean(yc * yc, axis=1, keepdims=True)
        yn = yc * lax.rsqrt(var + RW_GN_EPS) * lnw_ref[h:h + 1, :] + lnb_ref[h:h + 1, :]
        bonus = jnp.sum(r * k * rk_ref[h:h + 1, :], axis=1, keepdims=True) * v
        outs.append((yn + bonus) * g_ref[0, h])
    o_ref[0] = jnp.concatenate(outs, axis=1).astype(BF16)


def rwkv_scan(r, k, v, kk, kka, lw, g, r_k, ln_w, ln_b, s0_t, chunk):
    nb, _, t, _ = r.shape
    tspec = pl.BlockSpec((1, RW_HEADS, chunk, RW_N), lambda b, c: (b, 0, c, 0))
    hspec = pl.BlockSpec((RW_HEADS, RW_N), lambda b, c: (0, 0))
    sspec = pl.BlockSpec((1, RW_HEADS, RW_N, RW_N), lambda b, c: (b, 0, 0, 0))
    return pl.pallas_call(
        functools.partial(_rwkv_scan_kernel, chunk=chunk),
        grid=(nb, t // chunk),
        in_specs=[tspec] * 7 + [hspec] * 3 + [sspec],
        out_specs=[pl.BlockSpec((1, chunk, RW_W), lambda b, c: (b, c, 0)), sspec],
        out_shape=[jax.ShapeDtypeStruct((nb, t, RW_W), BF16),
                   jax.ShapeDtypeStruct((nb, RW_HEADS, RW_N, RW_N), F32)],
        compiler_params=_params(("parallel", "arbitrary")),
        name="rwkv_scan",
    )(r, k, v, kk, kka, lw, g, r_k, ln_w.reshape(RW_HEADS, RW_N), ln_b.reshape(RW_HEADS, RW_N), s0_t)


def _retention_kernel(lg_ref, q_ref, k_ref, v_ref, g_ref, cos_ref, sin_ref, s0_ref, o_ref, s_ref, *, lb, l_true, mm_dtype):
    h = pl.program_id(1)
    c = pl.program_id(2)

    @pl.when(c == 0)
    def _():
        s_ref[0, 0] = s0_ref[0, 0]

    lg = lg_ref[h]
    cos = cos_ref[0]
    sin = sin_ref[0]
    half = RET_DK // 2

    def rope(x):
        x1, x2 = x[:, :half], x[:, half:]
        return jnp.concatenate([x1 * cos - x2 * sin, x1 * sin + x2 * cos], axis=1)

    q = rope(q_ref[0])
    k = rope(k_ref[0]) * (RET_DK ** -0.5)
    v = v_ref[0]
    row = lax.broadcasted_iota(jnp.int32, (lb, lb), 0)
    col = lax.broadcasted_iota(jnp.int32, (lb, lb), 1)
    diff = (row - col).astype(F32)
    dmask = jnp.where(diff >= 0, jnp.exp(jnp.maximum(diff, 0.0) * lg), 0.0)
    idx = lax.broadcasted_iota(jnp.int32, (lb, 1), 0).astype(F32)
    qm = q.astype(mm_dtype)
    vm = v.astype(mm_dtype)
    s_old = s_ref[0, 0]
    sc = _nt_dot(qm, k.astype(mm_dtype)) * dmask
    inner = _dot(sc.astype(mm_dtype), vm)
    cross = _dot(qm, s_old.astype(mm_dtype)) * jnp.exp((idx + 1.0) * lg)
    kdec = k * jnp.exp((l_true - 1.0 - idx) * lg)
    s_dec = jnp.exp(jnp.zeros((1, RET_DV), F32) + l_true * lg)
    s_ref[0, 0] = s_old * s_dec + _tn_dot(kdec.astype(mm_dtype), vm)
    o = inner + cross
    o = o * lax.rsqrt(jnp.mean(o * o, axis=1, keepdims=True) + NORM_EPS)
    gv = g_ref[0]
    o_ref[0] = (o * (gv * jax.nn.sigmoid(gv))).astype(BF16)


def retention(z, cos, sin, lg, s0, lb, l_true, mm_dtype):
    nb, m, _ = z.shape
    kq = RET_HEADS
    kv = (2 * RET_HEADS * RET_DK) // RET_DV
    sspec = pl.BlockSpec((1, 1, RET_DK, RET_DV), lambda b, h, c: (b, h, 0, 0))
    return pl.pallas_call(
        functools.partial(_retention_kernel, lb=lb, l_true=float(l_true), mm_dtype=mm_dtype),
        grid=(nb, RET_HEADS, m // lb),
        in_specs=[pl.BlockSpec(memory_space=pltpu.SMEM),
                  pl.BlockSpec((1, lb, RET_DK), lambda b, h, c: (b, c, h)),
                  pl.BlockSpec((1, lb, RET_DK), lambda b, h, c: (b, c, kq + h)),
                  pl.BlockSpec((1, lb, RET_DV), lambda b, h, c: (b, c, kv + h)),
                  pl.BlockSpec((1, lb, RET_DV), lambda b, h, c: (b, c, kv + RET_HEADS + h)),
                  pl.BlockSpec((1, lb, RET_DK // 2), lambda b, h, c: (0, c, 0)),
                  pl.BlockSpec((1, lb, RET_DK // 2), lambda b, h, c: (0, c, 0)),
                  sspec],
        out_specs=[pl.BlockSpec((1, lb, RET_DV), lambda b, h, c: (b, c, h)), sspec],
        out_shape=[jax.ShapeDtypeStruct((nb, m, RET_HEADS * RET_DV), BF16),
                   jax.ShapeDtypeStruct((nb, RET_HEADS, RET_DK, RET_DV), F32)],
        compiler_params=_params(("parallel", "parallel", "arbitrary")),
        name="retention",
    )(lg, z, z, z, z, cos, sin, s0)


def _rope_tables(pos, half):
    inv = ROPE_BASE ** (-jnp.arange(half, dtype=F32) / half)
    ang = pos.astype(F32)[:, None] * inv[None, :]
    return jnp.cos(ang), jnp.sin(ang)


def _mla_tables(pos):
    cos, sin = _rope_tables(pos, MLA_ROPE // 2)
    return jnp.concatenate([cos, cos], axis=1), jnp.concatenate([-sin, sin], axis=1)


def _even_layer(x, mods, pos_tabs, prm, past, tm):
    (w_in_p, g_mix, g_kv, wuk_t, wuv_h, wuv_all, mu, w0, w_w2, a0, w_a2, w_g2, k_k, k_a, r_k, ln_w, ln_b,
     w_out_mla, w_out_rw) = prm
    sh1, sc1, gt1 = mods
    cf, sf = pos_tabs
    nb, m, _ = x.shape
    zr, zq, zkv = norm_mod_matmul_split(x, g_mix, sh1, sc1, w_in_p, (RW_SHIFT_W, MLA_HEADS * (MLA_NOPE + MLA_ROPE), MLA_QK), tm)
    lat, kr, kcat = kv_prep(zkv, g_kv, cf, sf, tm)
    qcat = q_prep(zq, wuk_t, cf, sf, tm)
    if past is None:
        mla_out = mla_prompt(qcat, kcat, wuv_h, 128, 512)
        shift_prev = jnp.zeros((nb, 1, RW_SHIFT_W), F32)
        prev = jnp.concatenate([shift_prev, zr[:, :-1]], axis=1)
        s0_t = jnp.zeros((nb, RW_HEADS, RW_N, RW_N), F32)
        tens = rwkv_prep(zr, prev, mu, w0, w_w2, a0, w_a2, w_g2, k_k, k_a, min(tm, 256))
        rw_out, s_t = rwkv_scan(*tens, r_k, ln_w, ln_b, s0_t, RW_CHUNK)
        shift_new = zr[:, -1]
    else:
        cache_lat, cache_kr, layer, page_table, s0, shift_prev, t_new = past
        nbs = m // t_new
        q_s = qcat.reshape(MLA_HEADS, nbs, t_new, MLA_QK).transpose(1, 2, 0, 3).reshape(nbs, t_new * MLA_HEADS, MLA_QK)
        kn_s = jnp.concatenate([lat, kr], axis=-1).reshape(nbs, t_new, MLA_QK)
        mla_out = mla_sample(q_s, kn_s, wuv_all, cache_lat, cache_kr, layer, page_table).reshape(1, m, MLA_HEADS * MLA_V)
        zr_b = zr.reshape(nbs, t_new, RW_SHIFT_W)
        prev = jnp.concatenate([shift_prev[:, None, :], zr_b[:, :-1]], axis=1).reshape(1, m, RW_SHIFT_W)
        tens = rwkv_prep(zr, prev, mu, w0, w_w2, a0, w_a2, w_g2, k_k, k_a, min(tm, 256))
        cpad = 8
        tens = [jnp.pad(u.reshape(RW_HEADS, nbs, t_new, RW_N).transpose(1, 0, 2, 3),
                        ((0, 0), (0, 0), (0, cpad - t_new), (0, 0))) for u in tens]
        rw_pad, s_t = rwkv_scan(*tens, r_k, ln_w, ln_b, jnp.swapaxes(s0, -1, -2), cpad)
        rw_out = rw_pad[:, :t_new].reshape(1, m, RW_W)
        shift_new = zr_b[:, -1]
    x_new = matmul_gate_res([mla_out, rw_out], [w_out_mla, w_out_rw], x, gt1, tm)
    return x_new, (lat, kr, jnp.swapaxes(s_t, -1, -2), shift_new)


def _odd_layer(x, mods, ret_tabs, prm, s0, t_new, tm):
    w_in, g_mix, w_out, lg = prm
    sh1, sc1, gt1 = mods
    cos, sin = ret_tabs
    nb, m, _ = x.shape
    z = norm_mod_matmul(x, g_mix, sh1, sc1, w_in, tm, 2048)
    if s0 is None:
        s0 = jnp.zeros((nb, RET_HEADS, RET_DK, RET_DV), F32)
        o, s_new = retention(z, cos, sin, lg, s0, RET_CHUNK, RET_CHUNK, BF16)
    else:
        nbs = m // t_new
        lpad = 8
        z_b = jnp.pad(z.reshape(nbs, t_new, -1), ((0, 0), (0, lpad - t_new), (0, 0)))
        o, s_new = retention(z_b, cos, sin, lg, s0, lpad, t_new, F32)
        o = o[:, :t_new].reshape(1, m, RET_HEADS * RET_DV)
    x_new = matmul_gate_res([o], [w_out], x, gt1, tm)
    return x_new, s_new


def kernel(x_prompt, x_sample, c_prompt, c_sample, cache_kv_latent, cache_k_rope, page_table, state_rwkv, state_rwkv_shift, state_ret, w_ada, b_ada, g_norm_mix, g_norm_mlp, g_final, w_in_even, g_kv, w_uk, w_uv, rw_mu, rw_w0, rw_w2, rw_a0, rw_a2, rw_g2, rw_k_k, rw_k_a, rw_r_k, rw_ln_w, rw_ln_b, w_out_even, w_in_odd, w_out_odd, w_ff1, w_ff2):
    nbp, t_p, d = x_prompt.shape
    nbs, t_s, _ = x_sample.shape
    depth = w_ada.shape[0]
    past_len = page_table.shape[1] * PAGE_SIZE
    m_s = nbs * t_s
    tm_p = min(512, t_p)
    tm_s = m_s

    c_all = jnp.concatenate([c_prompt, c_sample], axis=0)
    c_all = jnp.pad(c_all, ((0, -c_all.shape[0] % 16), (0, 0)))
    mods_all = ada_proj(c_all, w_ada, b_ada)

    def group_mods(l):
        mp = mods_all[l, :nbp].reshape(nbp, 1, 6, d)
        ms = jnp.broadcast_to(mods_all[l, nbp:nbp + nbs].reshape(nbs, 1, 6, d), (nbs, t_s, 6, d)).reshape(1, m_s, 6, d)
        return [mp[:, :, i] for i in range(6)], [ms[:, :, i] for i in range(6)]

    pos_p = jnp.arange(t_p)
    pos_s = past_len + jnp.arange(t_s)
    cf_p, sf_p = _mla_tables(pos_p)
    cf_s, sf_s = _mla_tables(pos_s)
    mla_tabs_p = (cf_p[None], sf_p[None])
    mla_tabs_s = (jnp.tile(cf_s, (nbs, 1))[None], jnp.tile(sf_s, (nbs, 1))[None])
    cr_p, sr_p = _rope_tables(pos_p, RET_DK // 2)
    cr_s, sr_s = _rope_tables(pos_s, RET_DK // 2)
    ret_tabs_p = (cr_p[None], sr_p[None])
    ret_tabs_s = (jnp.pad(cr_s, ((0, 8 - t_s), (0, 0)))[None], jnp.pad(sr_s, ((0, 8 - t_s), (0, 0)))[None])
    lg = jnp.log(1 - 2.0 ** (-5.0 - jnp.arange(RET_HEADS, dtype=F32)))

    xp = x_prompt
    xs = x_sample.reshape(1, m_s, d)
    lat_p, kr_p, rw_p, sh_p, ret_p = [], [], [], [], []
    lat_s, kr_s, rw_s, sh_s, ret_s = [], [], [], [], []
    q_w = MLA_HEADS * (MLA_NOPE + MLA_ROPE)
    for l in range(depth):
        (sh1p, sc1p, gt1p, sh2p, sc2p, gt2p), (sh1s, sc1s, gt1s, sh2s, sc2s, gt2s) = group_mods(l)
        i = l // 2
        if l % 2 == 0:
            w_in = w_in_even[i]
            wq = w_in[:, :q_w].reshape(d, MLA_HEADS, MLA_NOPE + MLA_ROPE)
            w_in_p = jnp.concatenate([w_in[:, q_w + MLA_QK:],
                                      wq[:, :, :MLA_NOPE].reshape(d, -1), wq[:, :, MLA_NOPE:].reshape(d, -1),
                                      w_in[:, q_w:q_w + MLA_QK]], axis=1).astype(BF16)
            wuv = w_uv[i]
            mla_w = MLA_HEADS * MLA_V
            prm = (w_in_p, g_norm_mix[l], g_kv[i], jnp.transpose(w_uk[i], (1, 2, 0)).astype(BF16),
                   jnp.transpose(wuv, (1, 0, 2)).astype(BF16), wuv.reshape(KV_RANK, mla_w).astype(BF16),
                   rw_mu[i], rw_w0[i], rw_w2[i].astype(BF16), rw_a0[i], rw_a2[i].astype(BF16), rw_g2[i].astype(BF16),
                   rw_k_k[i], rw_k_a[i], rw_r_k[i], rw_ln_w[i], rw_ln_b[i],
                   w_out_even[i, :mla_w].astype(BF16), w_out_even[i, mla_w:].astype(BF16))
            xp, (la, kr, st, sh) = _even_layer(xp, (sh1p, sc1p, gt1p), mla_tabs_p, prm, None, tm_p)
            lat_p.append(la); kr_p.append(kr); rw_p.append(st); sh_p.append(sh)
            past = (cache_kv_latent, cache_k_rope, i, page_table, state_rwkv[i], state_rwkv_shift[i], t_s)
            xs, (la, kr, st, sh) = _even_layer(xs, (sh1s, sc1s, gt1s), mla_tabs_s, prm, past, tm_s)
            lat_s.append(la.reshape(nbs, t_s, KV_RANK)); kr_s.append(kr.reshape(nbs, t_s, MLA_ROPE))
            rw_s.append(st); sh_s.append(sh)
        else:
            prm = (w_in_odd[i].astype(BF16), g_norm_mix[l], w_out_odd[i].astype(BF16), lg)
            xp, st = _odd_layer(xp, (sh1p, sc1p, gt1p), ret_tabs_p, prm, None, t_s, tm_p)
            ret_p.append(st)
            xs, st = _odd_layer(xs, (sh1s, sc1s, gt1s), ret_tabs_s, prm, state_ret[i], t_s, tm_s)
            ret_s.append(st)
        final = l == depth - 1
        w1 = w_ff1[l].astype(BF16)
        w2 = w_ff2[l].astype(BF16)
        xp = mlp_block(xp, g_norm_mlp[l], sh2p, sc2p, gt2p, w1, w2, g_final, final, tm_p, 1024)
        xs = mlp_block(xs, g_norm_mlp[l], sh2s, sc2s, gt2s, w1, w2, g_final, final, tm_s, 1024)
    return (xp, xs.reshape(nbs, t_s, d),
            jnp.stack(lat_p), jnp.stack(kr_p), jnp.stack(rw_p), jnp.stack(sh_p), jnp.stack(ret_p),
            jnp.stack(lat_s), jnp.stack(kr_s), jnp.stack(rw_s), jnp.stack(sh_s), jnp.stack(ret_s))
```

```python
import functools
import math

import jax
import jax.numpy as jnp
from jax import lax
from jax.experimental import pallas as pl
from jax.experimental.pallas import tpu as pltpu

F32 = jnp.float32
BF16 = jnp.bfloat16
HIGHEST = lax.Precision.HIGHEST

D_MODEL = 1024
PAGE_SIZE = 128
MLA_HEADS = 8
MLA_NOPE = 64
MLA_ROPE = 32
MLA_V = 64
KV_RANK = 256
MLA_QK = KV_RANK + MLA_ROPE
MLA_SCALE = (MLA_NOPE + MLA_ROPE) ** -0.5
MLA_QSCALE = MLA_SCALE * math.log2(math.e)
RW_HEADS = 8
RW_N = 64
RW_W = RW_HEADS * RW_N
RW_DECAY_LORA = 64
RW_A_LORA = 64
RW_G_LORA = 128
RW_SHIFT_W = 3 * RW_W + RW_DECAY_LORA + RW_A_LORA + RW_G_LORA
RW_GN_EPS = 64e-5
RW_CHUNK = 64
RET_HEADS = 4
RET_DK = 256
RET_DV = 512
RET_CHUNK = 128
D_FF = 4 * D_MODEL
ROPE_BASE = 10000.0
NORM_EPS = 1e-6
MIB = 1024 * 1024


def _params(sem, vmem_mib=48):
    return pltpu.CompilerParams(dimension_semantics=sem, vmem_limit_bytes=vmem_mib * MIB)


def _rms(x, g):
    return x * lax.rsqrt(jnp.mean(x * x, axis=-1, keepdims=True) + NORM_EPS) * g


def _nt_dot(a, b, precision=None):
    return lax.dot_general(a, b, (((1,), (1,)), ((), ())), precision=precision, preferred_element_type=F32)


def _tn_dot(a, b, precision=None):
    return lax.dot_general(a, b, (((0,), (0,)), ((), ())), precision=precision, preferred_element_type=F32)


def _dot(a, b, precision=None):
    return jnp.dot(a, b, precision=precision, preferred_element_type=F32)


def _mod_spec(mod, tm, nmid):
    if mod.shape[1] == 1:
        if nmid == 2:
            return pl.BlockSpec((1, 1, mod.shape[2]), lambda b, m, j: (b, 0, 0))
        return pl.BlockSpec((1, 1, mod.shape[2]), lambda b, m: (b, 0, 0))
    if nmid == 2:
        return pl.BlockSpec((1, tm, mod.shape[2]), lambda b, m, j: (b, m, 0))
    return pl.BlockSpec((1, tm, mod.shape[2]), lambda b, m: (b, m, 0))


def _ada_kernel(c_ref, w_ref, b_ref, o_ref):
    o_ref[0] = _dot(c_ref[...].astype(BF16), w_ref[0].astype(BF16)) + b_ref[0]


def ada_proj(c, w_ada, b_ada):
    nl, d, n = w_ada.shape
    r = c.shape[0]
    tn = 1536
    return pl.pallas_call(
        _ada_kernel,
        grid=(nl, n // tn),
        in_specs=[pl.BlockSpec((r, d), lambda l, j: (0, 0)),
                  pl.BlockSpec((1, d, tn), lambda l, j: (l, 0, j)),
                  pl.BlockSpec((1, 1, tn), lambda l, j: (l, 0, j))],
        out_specs=pl.BlockSpec((1, r, tn), lambda l, j: (l, 0, j)),
        out_shape=jax.ShapeDtypeStruct((nl, r, n), F32),
        compiler_params=_params(("parallel", "parallel")),
        name="ada_proj",
    )(c, w_ada, b_ada.reshape(nl, 1, n))


def _nmm_split_kernel(x_ref, g_ref, sh_ref, sc_ref, w_ref, *o_refs, splits):
    h = (_rms(x_ref[0], g_ref[...]) * (1.0 + sc_ref[0]) + sh_ref[0]).astype(BF16)
    off = 0
    for o_ref, n in zip(o_refs, splits):
        o_ref[0] = _dot(h, w_ref[:, off:off + n])
        off += n


def norm_mod_matmul_split(x, g, shift, scale, w, splits, tm):
    nb, m, d = x.shape
    n = w.shape[1]
    return pl.pallas_call(
        functools.partial(_nmm_split_kernel, splits=splits),
        grid=(nb, m // tm),
        in_specs=[pl.BlockSpec((1, tm, d), lambda b, i: (b, i, 0)),
                  pl.BlockSpec((1, d), lambda b, i: (0, 0)),
                  _mod_spec(shift, tm, 1), _mod_spec(scale, tm, 1),
                  pl.BlockSpec((d, n), lambda b, i: (0, 0))],
        out_specs=[pl.BlockSpec((1, tm, s), lambda b, i: (b, i, 0)) for s in splits],
        out_shape=[jax.ShapeDtypeStruct((nb, m, s), F32) for s in splits],
        compiler_params=_params(("parallel", "parallel")),
        name="norm_mod_matmul_split",
    )(x, g.reshape(1, d), shift, scale, w)


def _nmm_kernel(x_ref, g_ref, sh_ref, sc_ref, w_ref, o_ref):
    h = (_rms(x_ref[0], g_ref[...]) * (1.0 + sc_ref[0]) + sh_ref[0]).astype(BF16)
    o_ref[0] = _dot(h, w_ref[...])


def norm_mod_matmul(x, g, shift, scale, w, tm, tn):
    nb, m, d = x.shape
    n = w.shape[1]
    return pl.pallas_call(
        _nmm_kernel,
        grid=(nb, m // tm, n // tn),
        in_specs=[pl.BlockSpec((1, tm, d), lambda b, i, j: (b, i, 0)),
                  pl.BlockSpec((1, d), lambda b, i, j: (0, 0)),
                  _mod_spec(shift, tm, 2), _mod_spec(scale, tm, 2),
                  pl.BlockSpec((d, tn), lambda b, i, j: (0, j))],
        out_specs=pl.BlockSpec((1, tm, tn), lambda b, i, j: (b, i, j)),
        out_shape=jax.ShapeDtypeStruct((nb, m, n), F32),
        compiler_params=_params(("parallel", "parallel", "arbitrary")),
        name="norm_mod_matmul",
    )(x, g.reshape(1, d), shift, scale, w)


def _mgr_kernel(*refs, n_pairs):
    a_refs = refs[:n_pairs]
    w_refs = refs[n_pairs:2 * n_pairs]
    res_ref, gt_ref, o_ref = refs[2 * n_pairs:]
    acc = _dot(a_refs[0][0], w_refs[0][...])
    for a_ref, w_ref in zip(a_refs[1:], w_refs[1:]):
        acc = acc + _dot(a_ref[0], w_ref[...])
    o_ref[0] = res_ref[0] + gt_ref[0] * acc


def matmul_gate_res(a_list, w_list, res, gate, tm):
    nb, m, d = res.shape
    n_pairs = len(a_list)
    in_specs = [pl.BlockSpec((1, tm, a.shape[2]), lambda b, i: (b, i, 0)) for a in a_list]
    in_specs += [pl.BlockSpec(w.shape, lambda b, i: (0, 0)) for w in w_list]
    in_specs += [pl.BlockSpec((1, tm, d), lambda b, i: (b, i, 0)), _mod_spec(gate, tm, 1)]
    return pl.pallas_call(
        functools.partial(_mgr_kernel, n_pairs=n_pairs),
        grid=(nb, m // tm),
        in_specs=in_specs,
        out_specs=pl.BlockSpec((1, tm, d), lambda b, i: (b, i, 0)),
        out_shape=jax.ShapeDtypeStruct((nb, m, d), F32),
        compiler_params=_params(("parallel", "parallel")),
        name="matmul_gate_res",
    )(*a_list, *w_list, res, gate)


def _mlp_kernel(x_ref, g_ref, sh_ref, sc_ref, gt_ref, w1_ref, w2_ref, gf_ref, o_ref, h_scr, acc_scr, *, final):
    f = pl.program_id(2)

    @pl.when(f == 0)
    def _():
        h_scr[...] = (_rms(x_ref[0], g_ref[...]) * (1.0 + sc_ref[0]) + sh_ref[0]).astype(BF16)
        acc_scr[...] = jnp.zeros_like(acc_scr)

    a = _dot(h_scr[...], w1_ref[...])
    a = jnp.square(jnp.maximum(a, 0.0)).astype(BF16)
    acc_scr[...] += _dot(a, w2_ref[...])

    @pl.when(f == pl.num_programs(2) - 1)
    def _():
        y = x_ref[0] + gt_ref[0] * acc_scr[...]
        if final:
            y = _rms(y, gf_ref[...])
        o_ref[0] = y


def mlp_block(x, g, shift, scale, gate, w1, w2, g_final, final, tm, tf):
    nb, m, d = x.shape
    dff = w1.shape[1]
    return pl.pallas_call(
        functools.partial(_mlp_kernel, final=final),
        grid=(nb, m // tm, dff // tf),
        in_specs=[pl.BlockSpec((1, tm, d), lambda b, i, f: (b, i, 0)),
                  pl.BlockSpec((1, d), lambda b, i, f: (0, 0)),
                  _mod_spec(shift, tm, 2), _mod_spec(scale, tm, 2), _mod_spec(gate, tm, 2),
                  pl.BlockSpec((d, tf), lambda b, i, f: (0, f)),
                  pl.BlockSpec((tf, d), lambda b, i, f: (f, 0)),
                  pl.BlockSpec((1, d), lambda b, i, f: (0, 0))],
        out_specs=pl.BlockSpec((1, tm, d), lambda b, i, f: (b, i, 0)),
        out_shape=jax.ShapeDtypeStruct((nb, m, d), F32),
        scratch_shapes=[pltpu.VMEM((tm, d), BF16), pltpu.VMEM((tm, d), F32)],
        compiler_params=_params(("parallel", "parallel", "arbitrary")),
        name="mlp_block",
    )(x, g.reshape(1, d), shift, scale, gate, w1, w2, g_final.reshape(1, d))


def _rope32(x, cf, sf):
    half = MLA_ROPE // 2
    sw = jnp.concatenate([x[:, half:], x[:, :half]], axis=1)
    return x * cf + sw * sf


def _kvprep_kernel(zkv_ref, g_ref, cf_ref, sf_ref, lat_ref, kr_ref, kcat_ref):
    z = zkv_ref[0]
    lat = _rms(z[:, :KV_RANK], g_ref[...])
    kr = _rope32(z[:, KV_RANK:], cf_ref[0], sf_ref[0])
    lat_ref[0] = lat
    kr_ref[0] = kr
    kcat_ref[0, :, :KV_RANK] = lat.astype(BF16)
    kcat_ref[0, :, KV_RANK:] = kr.astype(BF16)


def kv_prep(zkv, g_kv, cf, sf, tm):
    nb, m, _ = zkv.shape
    return pl.pallas_call(
        _kvprep_kernel,
        grid=(nb, m // tm),
        in_specs=[pl.BlockSpec((1, tm, MLA_QK), lambda b, i: (b, i, 0)),
                  pl.BlockSpec((1, KV_RANK), lambda b, i: (0, 0)),
                  pl.BlockSpec((1, tm, MLA_ROPE), lambda b, i: (0, i, 0)),
                  pl.BlockSpec((1, tm, MLA_ROPE), lambda b, i: (0, i, 0))],
        out_specs=[pl.BlockSpec((1, tm, KV_RANK), lambda b, i: (b, i, 0)),
                   pl.BlockSpec((1, tm, MLA_ROPE), lambda b, i: (b, i, 0)),
                   pl.BlockSpec((1, tm, MLA_QK), lambda b, i: (b, i, 0))],
        out_shape=[jax.ShapeDtypeStruct((nb, m, KV_RANK), F32),
                   jax.ShapeDtypeStruct((nb, m, MLA_ROPE), F32),
                   jax.ShapeDtypeStruct((nb, m, MLA_QK), BF16)],
        compiler_params=_params(("parallel", "parallel")),
        name="kv_prep",
    )(zkv, g_kv.reshape(1, KV_RANK), cf, sf)


def _qprep_kernel(zq_ref, wuk_ref, cf_ref, sf_ref, o_ref):
    z = zq_ref[0]
    cf = cf_ref[0]
    sf = sf_ref[0]
    nope_w = MLA_HEADS * MLA_NOPE
    for h in range(MLA_HEADS):
        qn = z[:, h * MLA_NOPE:(h + 1) * MLA_NOPE].astype(BF16)
        ql = _dot(qn, wuk_ref[h]) * MLA_QSCALE
        qr = _rope32(z[:, nope_w + h * MLA_ROPE:nope_w + (h + 1) * MLA_ROPE], cf, sf) * MLA_QSCALE
        o_ref[0, h, :, :KV_RANK] = ql.astype(BF16)
        o_ref[0, h, :, KV_RANK:] = qr.astype(BF16)


def q_prep(zq, wuk_t, cf, sf, tm):
    nb, m, w = zq.shape
    return pl.pallas_call(
        _qprep_kernel,
        grid=(nb, m // tm),
        in_specs=[pl.BlockSpec((1, tm, w), lambda b, i: (b, i, 0)),
                  pl.BlockSpec((MLA_HEADS, MLA_NOPE, KV_RANK), lambda b, i: (0, 0, 0)),
                  pl.BlockSpec((1, tm, MLA_ROPE), lambda b, i: (0, i, 0)),
                  pl.BlockSpec((1, tm, MLA_ROPE), lambda b, i: (0, i, 0))],
        out_specs=pl.BlockSpec((1, MLA_HEADS, tm, MLA_QK), lambda b, i: (b, 0, i, 0)),
        out_shape=jax.ShapeDtypeStruct((nb, MLA_HEADS, m, MLA_QK), BF16),
        compiler_params=_params(("parallel", "parallel")),
        name="q_prep",
    )(zq, wuk_t, cf, sf)


def _mla_prompt_kernel(qi_ref, ki_ref, qt_ref, k_ref, latt_ref, wuvt_ref, o_ref, m_scr, l_scr, acc_scr, *, tq, tk):
    step = pl.program_id(1)
    qi = qi_ref[step]
    ki = ki_ref[step]
    last_k = (qi * tq + (tq - 1)) // tk

    @pl.when(ki == 0)
    def _():
        m_scr[...] = jnp.full_like(m_scr, -jnp.inf)
        l_scr[...] = jnp.zeros_like(l_scr)
        acc_scr[...] = jnp.zeros_like(acc_scr)

    def update(masked):
        st = _dot(k_ref[0], qt_ref[0, 0])
        if masked:
            kpos = ki * tk + lax.broadcasted_iota(jnp.int32, st.shape, 0)
            qpos = qi * tq + jnp.bitwise_and(lax.broadcasted_iota(jnp.int32, st.shape, 1), tq - 1)
            st = jnp.where(kpos <= qpos, st, -jnp.inf)
        m_prev = m_scr[...]
        m_new = jnp.maximum(m_prev, jnp.max(st, axis=0, keepdims=True))
        alpha = jnp.exp2(m_prev - m_new)
        pt = jnp.exp2(st - m_new)
        l_scr[...] = alpha * l_scr[...] + jnp.sum(pt, axis=0, keepdims=True)
        acc_scr[...] = alpha * acc_scr[...] + _dot(latt_ref[0], pt.astype(BF16))
        m_scr[...] = m_new

    needs_mask = ki * tk + (tk - 1) > qi * tq

    @pl.when(needs_mask)
    def _():
        update(True)

    @pl.when(jnp.logical_not(needs_mask))
    def _():
        update(False)

    @pl.when(ki == last_k)
    def _():
        ot = (acc_scr[...] / l_scr[...]).astype(BF16)
        for h in range(MLA_HEADS):
            o_ref[0, h * MLA_V:(h + 1) * MLA_V, :] = _dot(wuvt_ref[h], ot[:, h * tq:(h + 1) * tq]).astype(BF16)


def mla_prompt(qcat, kcat, wuv_t, tq, tk):
    nb, _, t, _ = qcat.shape
    tq = min(tq, t)
    tk = min(tk, t)
    nq = t // tq
    rows = MLA_HEADS * tq
    qt = qcat.reshape(nb, MLA_HEADS, nq, tq, MLA_QK).transpose(0, 2, 4, 1, 3).reshape(nb, nq, MLA_QK, rows)
    latt = jnp.swapaxes(kcat[:, :, :KV_RANK], 1, 2)
    pairs = [(i, j) for i in range(nq) for j in range((i * tq + tq - 1) // tk + 1)]
    qi_tab = jnp.asarray([p[0] for p in pairs], jnp.int32)
    ki_tab = jnp.asarray([p[1] for p in pairs], jnp.int32)
    grid_spec = pltpu.PrefetchScalarGridSpec(
        num_scalar_prefetch=2,
        grid=(nb, len(pairs)),
        in_specs=[pl.BlockSpec((1, 1, MLA_QK, rows), lambda b, s, qi, ki: (b, qi[s], 0, 0)),
                  pl.BlockSpec((1, tk, MLA_QK), lambda b, s, qi, ki: (b, ki[s], 0)),
                  pl.BlockSpec((1, KV_RANK, tk), lambda b, s, qi, ki: (b, 0, ki[s])),
                  pl.BlockSpec((MLA_HEADS, MLA_V, KV_RANK), lambda b, s, qi, ki: (0, 0, 0))],
        out_specs=pl.BlockSpec((1, MLA_HEADS * MLA_V, tq), lambda b, s, qi, ki: (b, 0, qi[s])),
        scratch_shapes=[pltpu.VMEM((1, rows), F32), pltpu.VMEM((1, rows), F32), pltpu.VMEM((KV_RANK, rows), F32)],
    )
    out_t = pl.pallas_call(
        functools.partial(_mla_prompt_kernel, tq=tq, tk=tk),
        grid_spec=grid_spec,
        out_shape=jax.ShapeDtypeStruct((nb, MLA_HEADS * MLA_V, t), BF16),
        compiler_params=_params(("parallel", "arbitrary")),
        name="mla_prompt",
    )(qi_tab, ki_tab, qt, kcat, latt, wuv_t)
    return jnp.swapaxes(out_t, 1, 2)


def _mla_sample_kernel(pt_ref, q_ref, kn_ref, wuv_ref, *rest, n_pg, n_grp, t_new):
    kl_refs = rest[:n_pg]
    kp_refs = rest[n_pg:2 * n_pg]
    o_ref = rest[2 * n_pg]
    m_scr, l_scr, acc_scr = rest[2 * n_pg + 1:]
    j = pl.program_id(1)
    rows = t_new * MLA_HEADS

    @pl.when(j == 0)
    def _():
        m_scr[...] = jnp.full_like(m_scr, -jnp.inf)
        l_scr[...] = jnp.zeros_like(l_scr)
        acc_scr[...] = jnp.zeros_like(acc_scr)

    q = q_ref[0]
    ql = q[:, :KV_RANK]
    qr = q[:, KV_RANK:]
    per = n_pg // n_grp
    kls = [kl_refs[i][...].astype(BF16) for i in range(n_pg)]
    ss = [_nt_dot(ql, kls[i]) + _dot(qr, kp_refs[i][...].astype(BF16)) for i in range(n_pg)]
    alphas, ps = [], []
    for gi in range(n_grp):
        s = jnp.concatenate(ss[gi * per:(gi + 1) * per], axis=1)
        m_prev = m_scr[gi]
        m_new = jnp.maximum(m_prev, jnp.max(s, axis=1, keepdims=True))
        alpha = jnp.exp2(m_prev - m_new)
        p = jnp.exp2(s - m_new).astype(BF16)
        l_scr[gi] = alpha * l_scr[gi] + jnp.sum(p.astype(F32), axis=1, keepdims=True)
        m_scr[gi] = m_new
        alphas.append(alpha)
        ps.append(p)
    for gi in range(n_grp):
        pv = _dot(ps[gi][:, :PAGE_SIZE], kls[gi * per])
        for i in range(1, per):
            pv = pv + _dot(ps[gi][:, i * PAGE_SIZE:(i + 1) * PAGE_SIZE], kls[gi * per + i])
        acc_scr[gi] = alphas[gi] * acc_scr[gi] + pv

    @pl.when(j == pl.num_programs(1) - 1)
    def _():
        qf = q.astype(F32)
        kn = kn_ref[0]
        trow = lax.broadcasted_iota(jnp.int32, (rows, 1), 0) // MLA_HEADS
        cols = []
        for jj in range(t_new):
            sj = jnp.sum(qf * kn[jj:jj + 1, :], axis=1, keepdims=True)
            cols.append(jnp.where(trow >= jj, sj, -jnp.inf))
        m1 = m_scr[0]
        for gi in range(1, n_grp):
            m1 = jnp.maximum(m1, m_scr[gi])
        for sj in cols:
            m1 = jnp.maximum(m1, sj)
        l1 = jnp.zeros_like(m1)
        acc1 = jnp.zeros((rows, KV_RANK), F32)
        for gi in range(n_grp):
            ag = jnp.exp2(m_scr[gi] - m1)
            l1 = l1 + ag * l_scr[gi]
            acc1 = acc1 + ag * acc_scr[gi]
        for jj, sj in enumerate(cols):
            pj = jnp.exp2(sj - m1)
            l1 = l1 + pj
            acc1 = acc1 + pj * kn[jj:jj + 1, :KV_RANK]
        o = (acc1 / l1).astype(BF16)
        proj = _dot(o, wuv_ref[...])
        rr = lax.broadcasted_iota(jnp.int32, proj.shape, 0)
        cc = lax.broadcasted_iota(jnp.int32, proj.shape, 1)
        proj = jnp.where(jnp.bitwise_and(rr, MLA_HEADS - 1) == cc // MLA_V, proj, 0.0)
        o_ref[0] = jnp.sum(proj.reshape(t_new, MLA_HEADS, MLA_HEADS * MLA_V), axis=1).astype(BF16)


def mla_sample(q_s, kn_s, wuv_all, cache_lat, cache_kr, layer, page_table):
    nb, rows, _ = q_s.shape
    t_new = rows // MLA_HEADS
    n_pages = page_table.shape[1]
    n_pg = math.gcd(n_pages, 16)
    n_grp = 2 if n_pg % 2 == 0 else 1
    pt_flat = page_table.reshape(-1)

    def page_map(i):
        return lambda b, j, pt: (layer, pt[b * n_pages + j * n_pg + i], 0, 0)

    in_specs = [pl.BlockSpec((1, rows, MLA_QK), lambda b, j, pt: (b, 0, 0)),
                pl.BlockSpec((1, t_new, MLA_QK), lambda b, j, pt: (b, 0, 0)),
                pl.BlockSpec((KV_RANK, MLA_HEADS * MLA_V), lambda b, j, pt: (0, 0))]
    in_specs += [pl.BlockSpec((None, None, PAGE_SIZE, KV_RANK), page_map(i)) for i in range(n_pg)]
    in_specs += [pl.BlockSpec((None, None, MLA_ROPE, PAGE_SIZE), page_map(i)) for i in range(n_pg)]
    grid_spec = pltpu.PrefetchScalarGridSpec(
        num_scalar_prefetch=1,
        grid=(nb, n_pages // n_pg),
        in_specs=in_specs,
        out_specs=pl.BlockSpec((1, t_new, MLA_HEADS * MLA_V), lambda b, j, pt: (b, 0, 0)),
        scratch_shapes=[pltpu.VMEM((n_grp, rows, 1), F32), pltpu.VMEM((n_grp, rows, 1), F32),
                        pltpu.VMEM((n_grp, rows, KV_RANK), F32)],
    )
    return pl.pallas_call(
        functools.partial(_mla_sample_kernel, n_pg=n_pg, n_grp=n_grp, t_new=t_new),
        grid_spec=grid_spec,
        out_shape=jax.ShapeDtypeStruct((nb, t_new, MLA_HEADS * MLA_V), BF16),
        compiler_params=_params(("parallel", "arbitrary")),
        name="mla_sample",
    )(pt_flat, q_s, kn_s, wuv_all, *([cache_lat] * n_pg), *([cache_kr] * n_pg))


def _rwprep_kernel(zr_ref, pv_ref, mu_ref, w0_ref, ww2_ref, a0_ref, wa2_ref, wg2_ref, kk_ref, ka_ref,
                   r_o, k_o, v_o, kk_o, kka_o, lw_o, g_o):
    zr = zr_ref[0]
    zs = zr + (pv_ref[0] - zr) * mu_ref[...]
    o3 = 3 * RW_W
    o4 = o3 + RW_DECAY_LORA
    o5 = o4 + RW_A_LORA
    xr, xk, xv = zs[:, :RW_W], zs[:, RW_W:2 * RW_W], zs[:, 2 * RW_W:o3]
    xw, xa, xg = zs[:, o3:o4], zs[:, o4:o5], zs[:, o5:]
    wl = w0_ref[...] + _dot(jnp.tanh(xw).astype(BF16), ww2_ref[...])
    w_log = -(jnp.maximum(-wl, 0.0) + jnp.log1p(jnp.exp(-jnp.abs(wl)))) - 0.5
    logw = -jnp.exp(w_log)
    a = jax.nn.sigmoid(a0_ref[...] + _dot(xa.astype(BF16), wa2_ref[...]))
    g = _dot(jax.nn.sigmoid(xg).astype(BF16), wg2_ref[...])
    kkf = xk * kk_ref[...]
    kf = xk * (1.0 + (a - 1.0) * ka_ref[...])
    for h in range(RW_HEADS):
        sl = slice(h * RW_N, (h + 1) * RW_N)
        kkh = kkf[:, sl]
        kkh = kkh / jnp.maximum(jnp.sqrt(jnp.sum(kkh * kkh, axis=1, keepdims=True)), 1e-12)
        r_o[0, h] = xr[:, sl]
        k_o[0, h] = kf[:, sl]
        v_o[0, h] = xv[:, sl]
        kk_o[0, h] = kkh
        kka_o[0, h] = kkh * a[:, sl]
        lw_o[0, h] = logw[:, sl]
        g_o[0, h] = g[:, sl]


def rwkv_prep(zr, prev, mu, w0, w_w2, a0, w_a2, w_g2, k_k, k_a, tm):
    nb, m, w = zr.shape
    vec = lambda n: pl.BlockSpec((1, n), lambda b, i: (0, 0))
    mat = lambda a: pl.BlockSpec(a.shape, lambda b, i: (0, 0))
    out_spec = pl.BlockSpec((1, RW_HEADS, tm, RW_N), lambda b, i: (b, 0, i, 0))
    out_sds = jax.ShapeDtypeStruct((nb, RW_HEADS, m, RW_N), F32)
    return pl.pallas_call(
        _rwprep_kernel,
        grid=(nb, m // tm),
        in_specs=[pl.BlockSpec((1, tm, w), lambda b, i: (b, i, 0)),
                  pl.BlockSpec((1, tm, w), lambda b, i: (b, i, 0)),
                  vec(w), vec(RW_W), mat(w_w2), vec(RW_W), mat(w_a2), mat(w_g2), vec(RW_W), vec(RW_W)],
        out_specs=[out_spec] * 7,
        out_shape=[out_sds] * 7,
        compiler_params=_params(("parallel", "parallel")),
        name="rwkv_prep",
    )(zr, prev, mu.reshape(1, w), w0.reshape(1, RW_W), w_w2, a0.reshape(1, RW_W), w_a2, w_g2,
      k_k.reshape(1, RW_W), k_a.reshape(1, RW_W))


def _split_bf16(x, terms):
    parts = []
    rem = x
    for i in range(terms):
        p = rem.astype(BF16)
        parts.append(p)
        if i + 1 < terms:
            rem = rem - p.astype(F32)
    return parts


def _mm(a, b, ta, tb, dot=_dot):
    ap = _split_bf16(a, ta)
    bp = _split_bf16(b, tb)
    n = max(ta, tb)
    acc = None
    for i, x in enumerate(ap):
        for j, y in enumerate(bp):
            if i + j < n:
                d = dot(x, y)
                acc = d if acc is None else acc + d
    return acc


RW_P_CUMSUM = 2
RW_P_INTRA = 1
RW_P_INV = 1
RW_P_STATE = 1


def _rwkv_scan_kernel(r_ref, k_ref, v_ref, kk_ref, kka_ref, lw_ref, g_ref, rk_ref, lnw_ref, lnb_ref, s0_ref,
                      o_ref, st_ref, *, chunk):
    c = pl.program_id(1)

    @pl.when(c == 0)
    def _():
        st_ref[0] = s0_ref[0]

    c2 = 2 * chunk
    row = lax.broadcasted_iota(jnp.int32, (chunk, chunk), 0)
    col = lax.broadcasted_iota(jnp.int32, (chunk, chunk), 1)
    tri = jnp.where(col <= row, 1.0, 0.0).astype(BF16)
    eye_c = jnp.where(row == col, 1.0, 0.0).astype(F32)
    row2 = lax.broadcasted_iota(jnp.int32, (c2, c2), 0)
    col2 = jnp.bitwise_and(lax.broadcasted_iota(jnp.int32, (c2, c2), 1), chunk - 1)
    mask2 = col2 < jnp.where(row2 < chunk, row2, row2 - (chunk - 1))
    eye_n = lax.broadcasted_iota(jnp.int32, (RW_N, RW_N), 0) == lax.broadcasted_iota(jnp.int32, (RW_N, RW_N), 1)
    zeros_cn = jnp.zeros((chunk, RW_N), F32)
    n_double = int(math.log2(chunk)) - 1
    heads = range(RW_HEADS)
    cs = [_mm(tri, lw_ref[0, h], 1, RW_P_CUMSUM) for h in heads]
    lhs, rhs, g_end, g_end_col = [], [], [], []
    for h in heads:
        g_incl = jnp.exp(cs[h])
        g_prev = jnp.exp(cs[h] - lw_ref[0, h])
        g_inv = jnp.exp(-cs[h])
        cs_last = cs[h][chunk - 1:chunk, :]
        g_end.append(jnp.exp(cs_last))
        g_end_col.append(jnp.exp(jnp.sum(jnp.where(eye_n, jnp.broadcast_to(cs_last, (RW_N, RW_N)), 0.0),
                                         axis=1, keepdims=True)))
        lhs.append(jnp.concatenate([-kk_ref[0, h] * g_prev, r_ref[0, h] * g_incl], axis=0))
        rhs.append(jnp.concatenate([kka_ref[0, h] * g_inv, k_ref[0, h] * g_inv], axis=0))
    mx = [jnp.where(mask2, _mm(lhs[h], rhs[h], RW_P_INTRA, RW_P_INTRA, _nt_dot), 0.0) for h in heads]
    from_state = [_mm(lhs[h], st_ref[0, h], RW_P_STATE, RW_P_STATE) for h in heads]
    from_v = [_mm(mx[h], jnp.concatenate([zeros_cn, v_ref[0, h]], axis=0), RW_P_INTRA, RW_P_INTRA)
              for h in heads]
    l_ab = [mx[h][:chunk, :chunk] for h in heads]
    tinv = [eye_c + l_ab[h] for h in heads]
    pw = [_mm(l_ab[h], l_ab[h], RW_P_INV, RW_P_INV) for h in heads]
    for _ in range(n_double - 1):
        both = [_mm(jnp.concatenate([pw[h], tinv[h]], axis=0), pw[h], RW_P_INV, RW_P_INV) for h in heads]
        tinv = [tinv[h] + both[h][chunk:] for h in heads]
        pw = [both[h][:chunk] for h in heads]
    tinv = [tinv[h] + _mm(tinv[h], pw[h], RW_P_INV, RW_P_INV) for h in heads]
    u = [_mm(tinv[h], from_state[h][:chunk] + from_v[h][:chunk], RW_P_INV, RW_P_INV) for h in heads]
    y_u = [_mm(mx[h][chunk:, :chunk], u[h], RW_P_INTRA, RW_P_INTRA) for h in heads]
    st_add = [_mm(rhs[h] * g_end[h], jnp.concatenate([u[h], v_ref[0, h]], axis=0), RW_P_STATE, RW_P_STATE, _tn_dot)
              for h in heads]
    outs = []
    for h in heads:
        st_ref[0, h] = st_ref[0, h] * g_end_col[h] + st_add[h]
        y = from_state[h][chunk:] + from_v[h][chunk:] + y_u[h]
        mean = jnp.mean(y, axis=1, keepdims=True)
        yc = y - mean
        var = jnp.mean(yc * yc, axis=1, keepdims=True)
        yn = yc * lax.rsqrt(var + RW_GN_EPS) * lnw_ref[h:h + 1, :] + lnb_ref[h:h + 1, :]
        bonus = jnp.sum(r_ref[0, h] * k_ref[0, h] * rk_ref[h:h + 1, :], axis=1, keepdims=True) * v_ref[0, h]
        outs.append((yn + bonus) * g_ref[0, h])
    o_ref[0] = jnp.concatenate(outs, axis=1).astype(BF16)


def rwkv_scan(r, k, v, kk, kka, lw, g, r_k, ln_w, ln_b, s0_t, chunk):
    nb, _, t, _ = r.shape
    tspec = pl.BlockSpec((1, RW_HEADS, chunk, RW_N), lambda b, c: (b, 0, c, 0))
    hspec = pl.BlockSpec((RW_HEADS, RW_N), lambda b, c: (0, 0))
    sspec = pl.BlockSpec((1, RW_HEADS, RW_N, RW_N), lambda b, c: (b, 0, 0, 0))
    return pl.pallas_call(
        functools.partial(_rwkv_scan_kernel, chunk=chunk),
        grid=(nb, t // chunk),
        in_specs=[tspec] * 7 + [hspec] * 3 + [sspec],
        out_specs=[pl.BlockSpec((1, chunk, RW_W), lambda b, c: (b, c, 0)), sspec],
        out_shape=[jax.ShapeDtypeStruct((nb, t, RW_W), BF16),
                   jax.ShapeDtypeStruct((nb, RW_HEADS, RW_N, RW_N), F32)],
        compiler_params=_params(("parallel", "arbitrary")),
        name="rwkv_scan",
    )(r, k, v, kk, kka, lw, g, r_k, ln_w.reshape(RW_HEADS, RW_N), ln_b.reshape(RW_HEADS, RW_N), s0_t)


def _rwkv_step_kernel(r_ref, k_ref, v_ref, kk_ref, kka_ref, lw_ref, g_ref, rk_ref, lnw_ref, lnb_ref, s0_ref,
                      o_ref, s_ref, w_scr, y_scr, *, t_new):
    for t in range(t_new):
        w_scr[t] = jnp.exp(lw_ref[t, 0])

    def value_row(vi, carry):
        s = s0_ref[0, vi]
        for t in range(t_new):
            sa = -jnp.sum(s * kk_ref[t, 0], axis=0, keepdims=True)
            s = s * w_scr[t] + sa * kka_ref[t, 0] + v_ref[t, 0, pl.ds(vi, 1), :] * k_ref[t, 0]
            y_scr[t, pl.ds(vi, 1), :] = jnp.sum(s * r_ref[t, 0], axis=0, keepdims=True)
        s_ref[0, vi] = s
        return carry

    lax.fori_loop(0, RW_N, value_row, 0)
    for t in range(t_new):
        y = y_scr[t]
        mean = jnp.mean(y, axis=0, keepdims=True)
        yc = y - mean
        var = jnp.mean(yc * yc, axis=0, keepdims=True)
        yn = yc * lax.rsqrt(var + RW_GN_EPS) * lnw_ref[0] + lnb_ref[0]
        bonus = jnp.sum(r_ref[t, 0] * k_ref[t, 0] * rk_ref[0], axis=0, keepdims=True) * v_ref[t, 0]
        o_ref[t, 0] = (yn + bonus) * g_ref[t, 0]


def rwkv_step(r, k, v, kk, kka, lw, g, r_k, ln_w, ln_b, s0):
    t_new, nh, n, nb = r.shape
    tspec = pl.BlockSpec((t_new, 1, n, nb), lambda h: (0, h, 0, 0))
    hspec = pl.BlockSpec((1, n, nb), lambda h: (h, 0, 0))
    sspec = pl.BlockSpec((1, n, n, nb), lambda h: (h, 0, 0, 0))
    return pl.pallas_call(
        functools.partial(_rwkv_step_kernel, t_new=t_new),
        grid=(nh,),
        in_specs=[tspec] * 7 + [hspec] * 3 + [sspec],
        out_specs=[tspec, sspec],
        out_shape=[jax.ShapeDtypeStruct((t_new, nh, n, nb), F32), jax.ShapeDtypeStruct((nh, n, n, nb), F32)],
        scratch_shapes=[pltpu.VMEM((t_new, n, nb), F32), pltpu.VMEM((t_new, n, nb), F32)],
        compiler_params=_params(("parallel",)),
        name="rwkv_step",
    )(r, k, v, kk, kka, lw, g, r_k, ln_w, ln_b, s0)


def _retention_kernel(lg_ref, qk_ref, v_ref, g_ref, cos_ref, sin_ref, s0_ref, o_ref, s_ref, *, lb, l_true):
    c = pl.program_id(1)

    @pl.when(c == 0)
    def _():
        s_ref[0] = s0_ref[0]

    lp = max(lb, 16)
    cos = cos_ref[0]
    sin = sin_ref[0]
    half = RET_DK // 2
    qk_w = RET_HEADS * RET_DK

    def rope(x):
        x1, x2 = x[:, :half], x[:, half:]
        return jnp.concatenate([x1 * cos - x2 * sin, x1 * sin + x2 * cos], axis=1)

    def rows(x):
        if lp == lb:
            return x
        return jnp.concatenate([x, jnp.zeros((lp - lb, x.shape[1]), x.dtype)], axis=0)

    row = lax.broadcasted_iota(jnp.int32, (lp, lp), 0)
    col = lax.broadcasted_iota(jnp.int32, (lp, lp), 1)
    diff = (row - col).astype(F32)
    idx = lax.broadcasted_iota(jnp.int32, (lp, 1), 0).astype(F32)
    heads = range(RET_HEADS)
    qm, km, kdm, vm, dmask, row_dec = [], [], [], [], [], []
    for h in heads:
        lg = lg_ref[h]
        q = rows(rope(qk_ref[0, :, h * RET_DK:(h + 1) * RET_DK]))
        k = rows(rope(qk_ref[0, :, qk_w + h * RET_DK:qk_w + (h + 1) * RET_DK]) * (RET_DK ** -0.5))
        qm.append(q.astype(BF16))
        km.append(k.astype(BF16))
        kdm.append((k * jnp.exp((l_true - 1.0 - idx) * lg)).astype(BF16))
        vm.append(rows(v_ref[0, :, h * RET_DV:(h + 1) * RET_DV]).astype(BF16))
        dmask.append(jnp.where(diff >= 0, jnp.exp(jnp.maximum(diff, 0.0) * lg), 0.0))
        row_dec.append(jnp.exp((idx + 1.0) * lg))
    sc = [(_nt_dot(qm[h], km[h]) * dmask[h]).astype(BF16) for h in heads]
    cross = [_dot(qm[h], s_ref[0, h].astype(BF16)) * row_dec[h] for h in heads]
    s_add = [_tn_dot(kdm[h], vm[h]) for h in heads]
    inner = [_dot(sc[h], vm[h]) for h in heads]
    outs = []
    for h in heads:
        s_dec = jnp.exp(jnp.zeros((1, RET_DV), F32) + l_true * lg_ref[h])
        s_ref[0, h] = s_ref[0, h] * s_dec + s_add[h]
        o = (inner[h] + cross[h])[:lb]
        o = o * lax.rsqrt(jnp.mean(o * o, axis=1, keepdims=True) + NORM_EPS)
        gv = g_ref[0, :, h * RET_DV:(h + 1) * RET_DV]
        outs.append(o * (gv * jax.nn.sigmoid(gv)))
    o_ref[0] = jnp.concatenate(outs, axis=1).astype(BF16)


def retention(z, cos, sin, lg, s0, lb, l_true):
    nb, m, _ = z.shape
    vw = RET_HEADS * RET_DV
    assert 2 * RET_HEADS * RET_DK == vw
    sspec = pl.BlockSpec((1, RET_HEADS, RET_DK, RET_DV), lambda b, c: (b, 0, 0, 0))
    return pl.pallas_call(
        functools.partial(_retention_kernel, lb=lb, l_true=float(l_true)),
        grid=(nb, m // lb),
        in_specs=[pl.BlockSpec(memory_space=pltpu.SMEM),
                  pl.BlockSpec((1, lb, vw), lambda b, c: (b, c, 0)),
                  pl.BlockSpec((1, lb, vw), lambda b, c: (b, c, 1)),
                  pl.BlockSpec((1, lb, vw), lambda b, c: (b, c, 2)),
                  pl.BlockSpec((1, lb, RET_DK // 2), lambda b, c: (0, c, 0)),
                  pl.BlockSpec((1, lb, RET_DK // 2), lambda b, c: (0, c, 0)),
                  sspec],
        out_specs=[pl.BlockSpec((1, lb, vw), lambda b, c: (b, c, 0)), sspec],
        out_shape=[jax.ShapeDtypeStruct((nb, m, vw), BF16),
                   jax.ShapeDtypeStruct((nb, RET_HEADS, RET_DK, RET_DV), F32)],
        compiler_params=_params(("parallel", "arbitrary")),
        name="retention",
    )(lg, z, z, z, cos, sin, s0)


def _rope_tables(pos, half):
    inv = ROPE_BASE ** (-jnp.arange(half, dtype=F32) / half)
    ang = pos.astype(F32)[:, None] * inv[None, :]
    return jnp.cos(ang), jnp.sin(ang)


def _mla_tables(pos):
    cos, sin = _rope_tables(pos, MLA_ROPE // 2)
    return jnp.concatenate([cos, cos], axis=1), jnp.concatenate([-sin, sin], axis=1)


def _even_layer(x, mods, pos_tabs, prm, past, tm):
    (w_in_p, g_mix, g_kv, wuk_t, wuv_t, wuv_all, mu, w0, w_w2, a0, w_a2, w_g2, k_k, k_a, r_k, ln_w, ln_b,
     w_out_mla, w_out_rw) = prm
    sh1, sc1, gt1 = mods
    cf, sf = pos_tabs
    nb, m, _ = x.shape
    zr, zq, zkv = norm_mod_matmul_split(x, g_mix, sh1, sc1, w_in_p, (RW_SHIFT_W, MLA_HEADS * (MLA_NOPE + MLA_ROPE), MLA_QK), tm)
    lat, kr, kcat = kv_prep(zkv, g_kv, cf, sf, tm)
    qcat = q_prep(zq, wuk_t, cf, sf, tm)
    if past is None:
        mla_out = mla_prompt(qcat, kcat, wuv_t, 256, 512)
        shift_prev = jnp.zeros((nb, 1, RW_SHIFT_W), F32)
        prev = jnp.concatenate([shift_prev, zr[:, :-1]], axis=1)
        s0_t = jnp.zeros((nb, RW_HEADS, RW_N, RW_N), F32)
        tens = rwkv_prep(zr, prev, mu, w0, w_w2, a0, w_a2, w_g2, k_k, k_a, min(tm, 256))
        rw_out, s_t = rwkv_scan(*tens, r_k, ln_w, ln_b, s0_t, RW_CHUNK)
        s_new = jnp.swapaxes(s_t, -1, -2)
        shift_new = zr[:, -1]
    else:
        cache_lat, cache_kr, layer, page_table, s0, shift_prev, t_new = past
        nbs = m // t_new
        q_s = qcat.reshape(MLA_HEADS, nbs, t_new, MLA_QK).transpose(1, 2, 0, 3).reshape(nbs, t_new * MLA_HEADS, MLA_QK)
        kn_s = jnp.concatenate([lat, kr], axis=-1).reshape(nbs, t_new, MLA_QK)
        mla_out = mla_sample(q_s, kn_s, wuv_all, cache_lat, cache_kr, layer, page_table).reshape(1, m, MLA_HEADS * MLA_V)
        zr_b = zr.reshape(nbs, t_new, RW_SHIFT_W)
        prev = jnp.concatenate([shift_prev[:, None, :], zr_b[:, :-1]], axis=1).reshape(1, m, RW_SHIFT_W)
        tens = rwkv_prep(zr, prev, mu, w0, w_w2, a0, w_a2, w_g2, k_k, k_a, min(tm, 256))
        tens = [u.reshape(RW_HEADS, nbs, t_new, RW_N).transpose(2, 0, 3, 1) for u in tens]
        lanes = lambda p: jnp.broadcast_to(p.reshape(RW_HEADS, RW_N, 1), (RW_HEADS, RW_N, nbs))
        rw_l, s_l = rwkv_step(*tens, lanes(r_k), lanes(ln_w), lanes(ln_b), jnp.transpose(s0, (1, 2, 3, 0)))
        rw_out = rw_l.transpose(3, 0, 1, 2).reshape(1, m, RW_W).astype(BF16)
        s_new = jnp.transpose(s_l, (3, 0, 1, 2))
        shift_new = zr_b[:, -1]
    x_new = matmul_gate_res([mla_out, rw_out], [w_out_mla, w_out_rw], x, gt1, tm)
    return x_new, (lat, kr, s_new, shift_new)


def _odd_layer(x, mods, ret_tabs, prm, s0, t_new, tm):
    w_in, g_mix, w_out, lg = prm
    sh1, sc1, gt1 = mods
    cos, sin = ret_tabs
    nb, m, _ = x.shape
    z = norm_mod_matmul(x, g_mix, sh1, sc1, w_in, tm, 2048)
    if s0 is None:
        s0 = jnp.zeros((nb, RET_HEADS, RET_DK, RET_DV), F32)
        o, s_new = retention(z, cos, sin, lg, s0, RET_CHUNK, RET_CHUNK)
    else:
        nbs = m // t_new
        lpad = 8
        z_b = jnp.pad(z.reshape(nbs, t_new, -1), ((0, 0), (0, lpad - t_new), (0, 0)))
        o, s_new = retention(z_b, cos, sin, lg, s0, lpad, t_new)
        o = o[:, :t_new].reshape(1, m, RET_HEADS * RET_DV)
    x_new = matmul_gate_res([o], [w_out], x, gt1, tm)
    return x_new, s_new


def kernel(x_prompt, x_sample, c_prompt, c_sample, cache_kv_latent, cache_k_rope, page_table, state_rwkv, state_rwkv_shift, state_ret, w_ada, b_ada, g_norm_mix, g_norm_mlp, g_final, w_in_even, g_kv, w_uk, w_uv, rw_mu, rw_w0, rw_w2, rw_a0, rw_a2, rw_g2, rw_k_k, rw_k_a, rw_r_k, rw_ln_w, rw_ln_b, w_out_even, w_in_odd, w_out_odd, w_ff1, w_ff2):
    nbp, t_p, d = x_prompt.shape
    nbs, t_s, _ = x_sample.shape
    depth = w_ada.shape[0]
    past_len = page_table.shape[1] * PAGE_SIZE
    m_s = nbs * t_s
    tm_p = min(512, t_p)
    tm_s = m_s

    c_all = jnp.concatenate([c_prompt, c_sample], axis=0)
    c_all = jnp.pad(c_all, ((0, -c_all.shape[0] % 16), (0, 0)))
    mods_all = ada_proj(c_all, w_ada, b_ada)

    def group_mods(l):
        mp = mods_all[l, :nbp].reshape(nbp, 1, 6, d)
        ms = jnp.broadcast_to(mods_all[l, nbp:nbp + nbs].reshape(nbs, 1, 6, d), (nbs, t_s, 6, d)).reshape(1, m_s, 6, d)
        return [mp[:, :, i] for i in range(6)], [ms[:, :, i] for i in range(6)]

    pos_p = jnp.arange(t_p)
    pos_s = past_len + jnp.arange(t_s)
    cf_p, sf_p = _mla_tables(pos_p)
    cf_s, sf_s = _mla_tables(pos_s)
    mla_tabs_p = (cf_p[None], sf_p[None])
    mla_tabs_s = (jnp.tile(cf_s, (nbs, 1))[None], jnp.tile(sf_s, (nbs, 1))[None])
    cr_p, sr_p = _rope_tables(pos_p, RET_DK // 2)
    cr_s, sr_s = _rope_tables(pos_s, RET_DK // 2)
    ret_tabs_p = (cr_p[None], sr_p[None])
    ret_tabs_s = (jnp.pad(cr_s, ((0, 8 - t_s), (0, 0)))[None], jnp.pad(sr_s, ((0, 8 - t_s), (0, 0)))[None])
    lg = jnp.log(1 - 2.0 ** (-5.0 - jnp.arange(RET_HEADS, dtype=F32)))

    xp = x_prompt
    xs = x_sample.reshape(1, m_s, d)
    lat_p, kr_p, rw_p, sh_p, ret_p = [], [], [], [], []
    lat_s, kr_s, rw_s, sh_s, ret_s = [], [], [], [], []
    q_w = MLA_HEADS * (MLA_NOPE + MLA_ROPE)
    for l in range(depth):
        (sh1p, sc1p, gt1p, sh2p, sc2p, gt2p), (sh1s, sc1s, gt1s, sh2s, sc2s, gt2s) = group_mods(l)
        i = l // 2
        if l % 2 == 0:
            w_in = w_in_even[i]
            wq = w_in[:, :q_w].reshape(d, MLA_HEADS, MLA_NOPE + MLA_ROPE)
            w_in_p = jnp.concatenate([w_in[:, q_w + MLA_QK:],
                                      wq[:, :, :MLA_NOPE].reshape(d, -1), wq[:, :, MLA_NOPE:].reshape(d, -1),
                                      w_in[:, q_w:q_w + MLA_QK]], axis=1).astype(BF16)
            wuv = w_uv[i]
            mla_w = MLA_HEADS * MLA_V
            prm = (w_in_p, g_norm_mix[l], g_kv[i], jnp.transpose(w_uk[i], (1, 2, 0)).astype(BF16),
                   jnp.transpose(wuv, (1, 2, 0)).astype(BF16), wuv.reshape(KV_RANK, mla_w).astype(BF16),
                   rw_mu[i], rw_w0[i], rw_w2[i].astype(BF16), rw_a0[i], rw_a2[i].astype(BF16), rw_g2[i].astype(BF16),
                   rw_k_k[i], rw_k_a[i], rw_r_k[i], rw_ln_w[i], rw_ln_b[i],
                   w_out_even[i, :mla_w].astype(BF16), w_out_even[i, mla_w:].astype(BF16))
            xp, (la, kr, st, sh) = _even_layer(xp, (sh1p, sc1p, gt1p), mla_tabs_p, prm, None, tm_p)
            lat_p.append(la); kr_p.append(kr); rw_p.append(st); sh_p.append(sh)
            cache_kr_t = jnp.swapaxes(cache_k_rope, 2, 3)
            past = (cache_kv_latent, cache_kr_t, i, page_table, state_rwkv[i], state_rwkv_shift[i], t_s)
            xs, (la, kr, st, sh) = _even_layer(xs, (sh1s, sc1s, gt1s), mla_tabs_s, prm, past, tm_s)
            lat_s.append(la.reshape(nbs, t_s, KV_RANK)); kr_s.append(kr.reshape(nbs, t_s, MLA_ROPE))
            rw_s.append(st); sh_s.append(sh)
        else:
            prm = (w_in_odd[i].astype(BF16), g_norm_mix[l], w_out_odd[i].astype(BF16), lg)
            xp, st = _odd_layer(xp, (sh1p, sc1p, gt1p), ret_tabs_p, prm, None, t_s, tm_p)
            ret_p.append(st)
            xs, st = _odd_layer(xs, (sh1s, sc1s, gt1s), ret_tabs_s, prm, state_ret[i], t_s, tm_s)
            ret_s.append(st)
        final = l == depth - 1
        w1 = w_ff1[l].astype(BF16)
        w2 = w_ff2[l].astype(BF16)
        xp = mlp_block(xp, g_norm_mlp[l], sh2p, sc2p, gt2p, w1, w2, g_final, final, tm_p, 1024)
        xs = mlp_block(xs, g_norm_mlp[l], sh2s, sc2s, gt2s, w1, w2, g_final, final, tm_s, 1024)
    return (xp, xs.reshape(nbs, t_s, d),
            jnp.stack(lat_p), jnp.stack(kr_p), jnp.stack(rw_p), jnp.stack(sh_p), jnp.stack(ret_p),
            jnp.stack(lat_s), jnp.stack(kr_s), jnp.stack(rw_s), jnp.stack(sh_s), jnp.stack(ret_s))
```

```python
import functools
import math

import jax
import jax.numpy as jnp
from jax import lax
from jax.experimental import pallas as pl
from jax.experimental.pallas import tpu as pltpu

F32 = jnp.float32
BF16 = jnp.bfloat16
HIGHEST = lax.Precision.HIGHEST

D_MODEL = 1024
PAGE_SIZE = 128
MLA_HEADS = 8
MLA_NOPE = 64
MLA_ROPE = 32
MLA_V = 64
KV_RANK = 256
MLA_QK = KV_RANK + MLA_ROPE
MLA_SCALE = (MLA_NOPE + MLA_ROPE) ** -0.5
MLA_QSCALE = MLA_SCALE * math.log2(math.e)
RW_HEADS = 8
RW_N = 64
RW_W = RW_HEADS * RW_N
RW_DECAY_LORA = 64
RW_A_LORA = 64
RW_G_LORA = 128
RW_SHIFT_W = 3 * RW_W + RW_DECAY_LORA + RW_A_LORA + RW_G_LORA
RW_GN_EPS = 64e-5
RW_CHUNK = 64
RET_HEADS = 4
RET_DK = 256
RET_DV = 512
RET_CHUNK = 128
D_FF = 4 * D_MODEL
ROPE_BASE = 10000.0
NORM_EPS = 1e-6
MIB = 1024 * 1024


def _params(sem, vmem_mib=48):
    return pltpu.CompilerParams(dimension_semantics=sem, vmem_limit_bytes=vmem_mib * MIB)


def _rms(x, g):
    return x * lax.rsqrt(jnp.mean(x * x, axis=-1, keepdims=True) + NORM_EPS) * g


def _nt_dot(a, b, precision=None):
    return lax.dot_general(a, b, (((1,), (1,)), ((), ())), precision=precision, preferred_element_type=F32)


def _tn_dot(a, b, precision=None):
    return lax.dot_general(a, b, (((0,), (0,)), ((), ())), precision=precision, preferred_element_type=F32)


def _dot(a, b, precision=None):
    return jnp.dot(a, b, precision=precision, preferred_element_type=F32)


def _mod_spec(mod, tm, nmid):
    if mod.shape[1] == 1:
        if nmid == 2:
            return pl.BlockSpec((1, 1, mod.shape[2]), lambda b, m, j: (b, 0, 0))
        return pl.BlockSpec((1, 1, mod.shape[2]), lambda b, m: (b, 0, 0))
    if nmid == 2:
        return pl.BlockSpec((1, tm, mod.shape[2]), lambda b, m, j: (b, m, 0))
    return pl.BlockSpec((1, tm, mod.shape[2]), lambda b, m: (b, m, 0))


def _ada_kernel(c_ref, w_ref, b_ref, o_ref):
    o_ref[0] = _dot(c_ref[...].astype(BF16), w_ref[0].astype(BF16)) + b_ref[0]


def ada_proj(c, w_ada, b_ada):
    nl, d, n = w_ada.shape
    r = c.shape[0]
    tn = 1536
    return pl.pallas_call(
        _ada_kernel,
        grid=(nl, n // tn),
        in_specs=[pl.BlockSpec((r, d), lambda l, j: (0, 0)),
                  pl.BlockSpec((1, d, tn), lambda l, j: (l, 0, j)),
                  pl.BlockSpec((1, 1, tn), lambda l, j: (l, 0, j))],
        out_specs=pl.BlockSpec((1, r, tn), lambda l, j: (l, 0, j)),
        out_shape=jax.ShapeDtypeStruct((nl, r, n), F32),
        compiler_params=_params(("parallel", "parallel")),
        name="ada_proj",
    )(c, w_ada, b_ada.reshape(nl, 1, n))


def _nmm_split_kernel(x_ref, g_ref, sh_ref, sc_ref, w_ref, *o_refs, splits):
    h = (_rms(x_ref[0], g_ref[...]) * (1.0 + sc_ref[0]) + sh_ref[0]).astype(BF16)
    off = 0
    for o_ref, n in zip(o_refs, splits):
        o_ref[0] = _dot(h, w_ref[:, off:off + n])
        off += n


def norm_mod_matmul_split(x, g, shift, scale, w, splits, tm):
    nb, m, d = x.shape
    n = w.shape[1]
    return pl.pallas_call(
        functools.partial(_nmm_split_kernel, splits=splits),
        grid=(nb, m // tm),
        in_specs=[pl.BlockSpec((1, tm, d), lambda b, i: (b, i, 0)),
                  pl.BlockSpec((1, d), lambda b, i: (0, 0)),
                  _mod_spec(shift, tm, 1), _mod_spec(scale, tm, 1),
                  pl.BlockSpec((d, n), lambda b, i: (0, 0))],
        out_specs=[pl.BlockSpec((1, tm, s), lambda b, i: (b, i, 0)) for s in splits],
        out_shape=[jax.ShapeDtypeStruct((nb, m, s), F32) for s in splits],
        compiler_params=_params(("parallel", "parallel")),
        name="norm_mod_matmul_split",
    )(x, g.reshape(1, d), shift, scale, w)


def _nmm_kernel(x_ref, g_ref, sh_ref, sc_ref, w_ref, o_ref):
    h = (_rms(x_ref[0], g_ref[...]) * (1.0 + sc_ref[0]) + sh_ref[0]).astype(BF16)
    o_ref[0] = _dot(h, w_ref[...])


def norm_mod_matmul(x, g, shift, scale, w, tm, tn):
    nb, m, d = x.shape
    n = w.shape[1]
    return pl.pallas_call(
        _nmm_kernel,
        grid=(nb, m // tm, n // tn),
        in_specs=[pl.BlockSpec((1, tm, d), lambda b, i, j: (b, i, 0)),
                  pl.BlockSpec((1, d), lambda b, i, j: (0, 0)),
                  _mod_spec(shift, tm, 2), _mod_spec(scale, tm, 2),
                  pl.BlockSpec((d, tn), lambda b, i, j: (0, j))],
        out_specs=pl.BlockSpec((1, tm, tn), lambda b, i, j: (b, i, j)),
        out_shape=jax.ShapeDtypeStruct((nb, m, n), F32),
        compiler_params=_params(("parallel", "parallel", "arbitrary")),
        name="norm_mod_matmul",
    )(x, g.reshape(1, d), shift, scale, w)


def _mgr_kernel(*refs, n_pairs, transposed):
    a_refs = refs[:n_pairs]
    w_refs = refs[n_pairs:2 * n_pairs]
    res_ref, gt_ref, o_ref = refs[2 * n_pairs:]
    acc = None
    for a_ref, w_ref, tr in zip(a_refs, w_refs, transposed):
        d = (_tn_dot if tr else _dot)(a_ref[0], w_ref[...])
        acc = d if acc is None else acc + d
    o_ref[0] = res_ref[0] + gt_ref[0] * acc


def matmul_gate_res(a_list, w_list, res, gate, tm, transposed=None):
    nb, m, d = res.shape
    n_pairs = len(a_list)
    transposed = tuple(transposed or (False,) * n_pairs)
    in_specs = [pl.BlockSpec((1, a.shape[1], tm), lambda b, i: (b, 0, i)) if tr else
                pl.BlockSpec((1, tm, a.shape[2]), lambda b, i: (b, i, 0)) for a, tr in zip(a_list, transposed)]
    in_specs += [pl.BlockSpec(w.shape, lambda b, i: (0, 0)) for w in w_list]
    in_specs += [pl.BlockSpec((1, tm, d), lambda b, i: (b, i, 0)), _mod_spec(gate, tm, 1)]
    return pl.pallas_call(
        functools.partial(_mgr_kernel, n_pairs=n_pairs, transposed=transposed),
        grid=(nb, m // tm),
        in_specs=in_specs,
        out_specs=pl.BlockSpec((1, tm, d), lambda b, i: (b, i, 0)),
        out_shape=jax.ShapeDtypeStruct((nb, m, d), F32),
        compiler_params=_params(("parallel", "parallel")),
        name="matmul_gate_res",
    )(*a_list, *w_list, res, gate)


def _mlp_kernel(x_ref, g_ref, sh_ref, sc_ref, gt_ref, w1_ref, w2_ref, gf_ref, o_ref, h_scr, acc_scr, *, final):
    f = pl.program_id(2)

    @pl.when(f == 0)
    def _():
        h_scr[...] = (_rms(x_ref[0], g_ref[...]) * (1.0 + sc_ref[0]) + sh_ref[0]).astype(BF16)
        acc_scr[...] = jnp.zeros_like(acc_scr)

    a = _dot(h_scr[...], w1_ref[...])
    a = jnp.square(jnp.maximum(a, 0.0)).astype(BF16)
    acc_scr[...] += _dot(a, w2_ref[...])

    @pl.when(f == pl.num_programs(2) - 1)
    def _():
        y = x_ref[0] + gt_ref[0] * acc_scr[...]
        if final:
            y = _rms(y, gf_ref[...])
        o_ref[0] = y


def mlp_block(x, g, shift, scale, gate, w1, w2, g_final, final, tm, tf):
    nb, m, d = x.shape
    dff = w1.shape[1]
    return pl.pallas_call(
        functools.partial(_mlp_kernel, final=final),
        grid=(nb, m // tm, dff // tf),
        in_specs=[pl.BlockSpec((1, tm, d), lambda b, i, f: (b, i, 0)),
                  pl.BlockSpec((1, d), lambda b, i, f: (0, 0)),
                  _mod_spec(shift, tm, 2), _mod_spec(scale, tm, 2), _mod_spec(gate, tm, 2),
                  pl.BlockSpec((d, tf), lambda b, i, f: (0, f)),
                  pl.BlockSpec((tf, d), lambda b, i, f: (f, 0)),
                  pl.BlockSpec((1, d), lambda b, i, f: (0, 0))],
        out_specs=pl.BlockSpec((1, tm, d), lambda b, i, f: (b, i, 0)),
        out_shape=jax.ShapeDtypeStruct((nb, m, d), F32),
        scratch_shapes=[pltpu.VMEM((tm, d), BF16), pltpu.VMEM((tm, d), F32)],
        compiler_params=_params(("parallel", "parallel", "arbitrary")),
        name="mlp_block",
    )(x, g.reshape(1, d), shift, scale, gate, w1, w2, g_final.reshape(1, d))


def _rope32(x, cf, sf):
    half = MLA_ROPE // 2
    sw = jnp.concatenate([x[:, half:], x[:, :half]], axis=1)
    return x * cf + sw * sf


def _eye_bf16(n):
    return jnp.where(lax.broadcasted_iota(jnp.int32, (n, n), 0) == lax.broadcasted_iota(jnp.int32, (n, n), 1),
                     1.0, 0.0).astype(BF16)


def _kvprep_kernel(zkv_ref, g_ref, cf_ref, sf_ref, lat_ref, kr_ref, kcat_ref, latt_ref):
    z = zkv_ref[0]
    lat = _rms(z[:, :KV_RANK], g_ref[...])
    kr = _rope32(z[:, KV_RANK:], cf_ref[0], sf_ref[0])
    lat_ref[0] = lat
    kr_ref[0] = kr
    lat_b = lat.astype(BF16)
    kcat_ref[0, :, :KV_RANK] = lat_b
    kcat_ref[0, :, KV_RANK:] = kr.astype(BF16)
    latt_ref[0] = _nt_dot(_eye_bf16(KV_RANK), lat_b).astype(BF16)


def kv_prep(zkv, g_kv, cf, sf, tm):
    nb, m, _ = zkv.shape
    return pl.pallas_call(
        _kvprep_kernel,
        grid=(nb, m // tm),
        in_specs=[pl.BlockSpec((1, tm, MLA_QK), lambda b, i: (b, i, 0)),
                  pl.BlockSpec((1, KV_RANK), lambda b, i: (0, 0)),
                  pl.BlockSpec((1, tm, MLA_ROPE), lambda b, i: (0, i, 0)),
                  pl.BlockSpec((1, tm, MLA_ROPE), lambda b, i: (0, i, 0))],
        out_specs=[pl.BlockSpec((1, tm, KV_RANK), lambda b, i: (b, i, 0)),
                   pl.BlockSpec((1, tm, MLA_ROPE), lambda b, i: (b, i, 0)),
                   pl.BlockSpec((1, tm, MLA_QK), lambda b, i: (b, i, 0)),
                   pl.BlockSpec((1, KV_RANK, tm), lambda b, i: (b, 0, i))],
        out_shape=[jax.ShapeDtypeStruct((nb, m, KV_RANK), F32),
                   jax.ShapeDtypeStruct((nb, m, MLA_ROPE), F32),
                   jax.ShapeDtypeStruct((nb, m, MLA_QK), BF16),
                   jax.ShapeDtypeStruct((nb, KV_RANK, m), BF16)],
        compiler_params=_params(("parallel", "parallel")),
        name="kv_prep",
    )(zkv, g_kv.reshape(1, KV_RANK), cf, sf)


def _qprep_kernel(zq_ref, wuk_ref, cf_ref, sf_ref, o_ref):
    z = zq_ref[0]
    cf = cf_ref[0]
    sf = sf_ref[0]
    nope_w = MLA_HEADS * MLA_NOPE
    for h in range(MLA_HEADS):
        qn = z[:, h * MLA_NOPE:(h + 1) * MLA_NOPE].astype(BF16)
        ql = _dot(qn, wuk_ref[h]) * MLA_QSCALE
        qr = _rope32(z[:, nope_w + h * MLA_ROPE:nope_w + (h + 1) * MLA_ROPE], cf, sf) * MLA_QSCALE
        o_ref[0, h, :, :KV_RANK] = ql.astype(BF16)
        o_ref[0, h, :, KV_RANK:] = qr.astype(BF16)


def _qprep_t_kernel(zq_ref, wuk_ref, cf_ref, sf_ref, o_ref, *, tq):
    z = zq_ref[0]
    cf = cf_ref[0]
    sf = sf_ref[0]
    nope_w = MLA_HEADS * MLA_NOPE
    eye = _eye_bf16(MLA_ROPE)
    for h in range(MLA_HEADS):
        qn = z[:, h * MLA_NOPE:(h + 1) * MLA_NOPE].astype(BF16)
        ql_t = _nt_dot(wuk_ref[h], qn) * MLA_QSCALE
        qr = _rope32(z[:, nope_w + h * MLA_ROPE:nope_w + (h + 1) * MLA_ROPE], cf, sf) * MLA_QSCALE
        qr_t = _nt_dot(eye, qr.astype(BF16))
        o_ref[0, 0, :KV_RANK, h * tq:(h + 1) * tq] = ql_t.astype(BF16)
        o_ref[0, 0, KV_RANK:, h * tq:(h + 1) * tq] = qr_t.astype(BF16)


def q_prep_t(zq, wuk_r, cf, sf, tq):
    nb, m, w = zq.shape
    return pl.pallas_call(
        functools.partial(_qprep_t_kernel, tq=tq),
        grid=(nb, m // tq),
        in_specs=[pl.BlockSpec((1, tq, w), lambda b, i: (b, i, 0)),
                  pl.BlockSpec((MLA_HEADS, KV_RANK, MLA_NOPE), lambda b, i: (0, 0, 0)),
                  pl.BlockSpec((1, tq, MLA_ROPE), lambda b, i: (0, i, 0)),
                  pl.BlockSpec((1, tq, MLA_ROPE), lambda b, i: (0, i, 0))],
        out_specs=pl.BlockSpec((1, 1, MLA_QK, MLA_HEADS * tq), lambda b, i: (b, i, 0, 0)),
        out_shape=jax.ShapeDtypeStruct((nb, m // tq, MLA_QK, MLA_HEADS * tq), BF16),
        compiler_params=_params(("parallel", "parallel")),
        name="q_prep_t",
    )(zq, wuk_r, cf, sf)


def q_prep(zq, wuk_t, cf, sf, tm):
    nb, m, w = zq.shape
    return pl.pallas_call(
        _qprep_kernel,
        grid=(nb, m // tm),
        in_specs=[pl.BlockSpec((1, tm, w), lambda b, i: (b, i, 0)),
                  pl.BlockSpec((MLA_HEADS, MLA_NOPE, KV_RANK), lambda b, i: (0, 0, 0)),
                  pl.BlockSpec((1, tm, MLA_ROPE), lambda b, i: (0, i, 0)),
                  pl.BlockSpec((1, tm, MLA_ROPE), lambda b, i: (0, i, 0))],
        out_specs=pl.BlockSpec((1, MLA_HEADS, tm, MLA_QK), lambda b, i: (b, 0, i, 0)),
        out_shape=jax.ShapeDtypeStruct((nb, MLA_HEADS, m, MLA_QK), BF16),
        compiler_params=_params(("parallel", "parallel")),
        name="q_prep",
    )(zq, wuk_t, cf, sf)


def _mla_prompt_kernel(qi_ref, ki_ref, qt_ref, k_ref, latt_ref, wuvt_ref, o_ref, m_scr, l_scr, acc_scr, *, tq, tk):
    step = pl.program_id(1)
    qi = qi_ref[step]
    ki = ki_ref[step]
    last_k = (qi * tq + (tq - 1)) // tk

    @pl.when(ki == 0)
    def _():
        m_scr[...] = jnp.full_like(m_scr, -jnp.inf)
        l_scr[...] = jnp.zeros_like(l_scr)
        acc_scr[...] = jnp.zeros_like(acc_scr)

    def update(masked):
        st = _dot(k_ref[0], qt_ref[0, 0])
        if masked:
            kpos = ki * tk + lax.broadcasted_iota(jnp.int32, st.shape, 0)
            qpos = qi * tq + jnp.bitwise_and(lax.broadcasted_iota(jnp.int32, st.shape, 1), tq - 1)
            st = jnp.where(kpos <= qpos, st, -jnp.inf)
        m_prev = m_scr[...]
        m_new = jnp.maximum(m_prev, jnp.max(st, axis=0, keepdims=True))
        alpha = jnp.exp2(m_prev - m_new)
        pt = jnp.exp2(st - m_new)
        l_scr[...] = alpha * l_scr[...] + jnp.sum(pt, axis=0, keepdims=True)
        acc_scr[...] = alpha * acc_scr[...] + _dot(latt_ref[0], pt.astype(BF16))
        m_scr[...] = m_new

    needs_mask = ki * tk + (tk - 1) > qi * tq

    @pl.when(needs_mask)
    def _():
        update(True)

    @pl.when(jnp.logical_not(needs_mask))
    def _():
        update(False)

    @pl.when(ki == last_k)
    def _():
        ot = (acc_scr[...] / l_scr[...]).astype(BF16)
        for h in range(MLA_HEADS):
            o_ref[0, h * MLA_V:(h + 1) * MLA_V, :] = _dot(wuvt_ref[h], ot[:, h * tq:(h + 1) * tq]).astype(BF16)


def mla_prompt(qt, kcat, latt, wuv_t, tk):
    nb, nq, _, rows = qt.shape
    tq = rows // MLA_HEADS
    t = nq * tq
    tk = min(tk, t)
    pairs =[(i, j) for i in range(nq) for j in range((i * tq + tq - 1) // tk + 1)]
    qi_tab = jnp.asarray([p[0] for p in pairs], jnp.int32)
    ki_tab = jnp.asarray([p[1] for p in pairs], jnp.int32)
    grid_spec = pltpu.PrefetchScalarGridSpec(
        num_scalar_prefetch=2,
        grid=(nb, len(pairs)),
        in_specs=[pl.BlockSpec((1, 1, MLA_QK, rows), lambda b, s, qi, ki: (b, qi[s], 0, 0)),
                  pl.BlockSpec((1, tk, MLA_QK), lambda b, s, qi, ki: (b, ki[s], 0)),
                  pl.BlockSpec((1, KV_RANK, tk), lambda b, s, qi, ki: (b, 0, ki[s])),
                  pl.BlockSpec((MLA_HEADS, MLA_V, KV_RANK), lambda b, s, qi, ki: (0, 0, 0))],
        out_specs=pl.BlockSpec((1, MLA_HEADS * MLA_V, tq), lambda b, s, qi, ki: (b, 0, qi[s])),
        scratch_shapes=[pltpu.VMEM((1, rows), F32), pltpu.VMEM((1, rows), F32), pltpu.VMEM((KV_RANK, rows), F32)],
    )
    return pl.pallas_call(
        functools.partial(_mla_prompt_kernel, tq=tq, tk=tk),
        grid_spec=grid_spec,
        out_shape=jax.ShapeDtypeStruct((nb, MLA_HEADS * MLA_V, t), BF16),
        compiler_params=_params(("parallel", "arbitrary")),
        name="mla_prompt",
    )(qi_tab, ki_tab, qt, kcat, latt, wuv_t)


def _mla_sample_kernel(pt_ref, q_ref, kn_ref, wuv_ref, lat_hbm, kr_hbm, o_ref, kl_buf, kp_buf, sem,
                       *, layer, n_pages, n_pg, n_grp, t_new):
    b = pl.program_id(0)
    rows = t_new * MLA_HEADS
    n_chunks = n_pages // n_pg
    per = n_pg // n_grp

    def page_copies(bb, c, slot):
        cps = []
        for i in range(n_pg):
            page = pt_ref[bb * n_pages + c * n_pg + i]
            cps.append(pltpu.make_async_copy(lat_hbm.at[layer, page], kl_buf.at[slot, i], sem.at[slot, 0]))
            cps.append(pltpu.make_async_copy(kr_hbm.at[layer, page], kp_buf.at[slot, i], sem.at[slot, 1]))
        return cps

    @pl.when(b == 0)
    def _():
        for cp in page_copies(0, 0, 0):
            cp.start()

    q = q_ref[0]
    ql = q[:, :KV_RANK]
    qr = q[:, KV_RANK:]
    m_g = [jnp.full((rows, 1), -jnp.inf, F32) for _ in range(n_grp)]
    l_g = [jnp.zeros((rows, 1), F32) for _ in range(n_grp)]
    acc_g = [jnp.zeros((rows, KV_RANK), F32) for _ in range(n_grp)]
    for c in range(n_chunks):
        slot = c % 2
        if c + 1 < n_chunks:
            for cp in page_copies(b, c + 1, 1 - slot):
                cp.start()
        else:
            @pl.when(b + 1 < pl.num_programs(0))
            def _():
                for cp in page_copies(b + 1, 0, 1 - slot):
                    cp.start()
        for cp in page_copies(b, c, slot):
            cp.wait()
        kls = [kl_buf[slot, i].astype(BF16) for i in range(n_pg)]
        ss = [_nt_dot(ql, kls[i]) + _dot(qr, kp_buf[slot, i].astype(BF16)) for i in range(n_pg)]
        alphas, ps = [], []
        for gi in range(n_grp):
            s = jnp.concatenate(ss[gi * per:(gi + 1) * per], axis=1)
            m_new = jnp.maximum(m_g[gi], jnp.max(s, axis=1, keepdims=True))
            alpha = jnp.exp2(m_g[gi] - m_new)
            p = jnp.exp2(s - m_new).astype(BF16)
            l_g[gi] = alpha * l_g[gi] + jnp.sum(p.astype(F32), axis=1, keepdims=True)
            m_g[gi] = m_new
            alphas.append(alpha)
            ps.append(p)
        for gi in range(n_grp):
            pv = _dot(ps[gi][:, :PAGE_SIZE], kls[gi * per])
            for i in range(1, per):
                pv = pv + _dot(ps[gi][:, i * PAGE_SIZE:(i + 1) * PAGE_SIZE], kls[gi * per + i])
            acc_g[gi] = alphas[gi] * acc_g[gi] + pv

    qf = q.astype(F32)
    kn = kn_ref[0]
    trow = lax.broadcasted_iota(jnp.int32, (rows, 1), 0) // MLA_HEADS
    cols = []
    for jj in range(t_new):
        sj = jnp.sum(qf * kn[jj:jj + 1, :], axis=1, keepdims=True)
        cols.append(jnp.where(trow >= jj, sj, -jnp.inf))
    m1 = m_g[0]
    for gi in range(1, n_grp):
        m1 = jnp.maximum(m1, m_g[gi])
    for sj in cols:
        m1 = jnp.maximum(m1, sj)
    l1 = jnp.zeros_like(m1)
    acc1 = jnp.zeros((rows, KV_RANK), F32)
    for gi in range(n_grp):
        ag = jnp.exp2(m_g[gi] - m1)
        l1 = l1 + ag * l_g[gi]
        acc1 = acc1 + ag * acc_g[gi]
    for jj, sj in enumerate(cols):
        pj = jnp.exp2(sj - m1)
        l1 = l1 + pj
        acc1 = acc1 + pj * kn[jj:jj + 1, :KV_RANK]
    o = (acc1 / l1).astype(BF16)
    proj = _dot(o, wuv_ref[...])
    rr = lax.broadcasted_iota(jnp.int32, proj.shape, 0)
    cc = lax.broadcasted_iota(jnp.int32, proj.shape, 1)
    proj = jnp.where(jnp.bitwise_and(rr, MLA_HEADS - 1) == cc // MLA_V, proj, 0.0)
    o_ref[0] = jnp.sum(proj.reshape(t_new, MLA_HEADS, MLA_HEADS * MLA_V), axis=1).astype(BF16)


MLA_SAMPLE_PAGES_PER_CHUNK = 32


def mla_sample(q_s, kn_s, wuv_all, cache_lat, cache_kr, layer, page_table):
    nb, rows, _ = q_s.shape
    t_new = rows // MLA_HEADS
    n_pages = page_table.shape[1]
    n_pg = min(MLA_SAMPLE_PAGES_PER_CHUNK, n_pages // 2)
    assert n_pages % (2 * n_pg) == 0
    n_grp = 2 if n_pg % 2 == 0 else 1
    grid_spec = pltpu.PrefetchScalarGridSpec(
        num_scalar_prefetch=1,
        grid=(nb,),
        in_specs=[pl.BlockSpec((1, rows, MLA_QK), lambda b, pt: (b, 0, 0)),
                  pl.BlockSpec((1, t_new, MLA_QK), lambda b, pt: (b, 0, 0)),
                  pl.BlockSpec((KV_RANK, MLA_HEADS * MLA_V), lambda b, pt: (0, 0)),
                  pl.BlockSpec(memory_space=pl.ANY),
                  pl.BlockSpec(memory_space=pl.ANY)],
        out_specs=pl.BlockSpec((1, t_new, MLA_HEADS * MLA_V), lambda b, pt: (b, 0, 0)),
        scratch_shapes=[pltpu.VMEM((2, n_pg, PAGE_SIZE, KV_RANK), F32),
                        pltpu.VMEM((2, n_pg, MLA_ROPE, PAGE_SIZE), F32),
                        pltpu.SemaphoreType.DMA((2, 2))],
    )
    return pl.pallas_call(
        functools.partial(_mla_sample_kernel, layer=layer, n_pages=n_pages, n_pg=n_pg, n_grp=n_grp, t_new=t_new),
        grid_spec=grid_spec,
        out_shape=jax.ShapeDtypeStruct((nb, t_new, MLA_HEADS * MLA_V), BF16),
        compiler_params=_params(("arbitrary",)),
        name="mla_sample",
    )(page_table.reshape(-1), q_s, kn_s, wuv_all, cache_lat, cache_kr)


def _rwprep_kernel(zr_ref, pv_ref, mu_ref, w0_ref, ww2_ref, a0_ref, wa2_ref, wg2_ref, kk_ref, ka_ref,
                   r_o, k_o, v_o, kk_o, kka_o, lw_o, g_o, *scratch, shift_in_kernel):
    zr = zr_ref[0]
    if shift_in_kernel:
        (carry,) = scratch

        @pl.when(pl.program_id(1) == 0)
        def _():
            carry[...] = pv_ref[0]

        first = lax.broadcasted_iota(jnp.int32, zr.shape, 0) == 0
        prev = jnp.where(first, carry[...], pltpu.roll(zr, 1, 0))
        carry[...] = zr[zr.shape[0] - 1:, :]
    else:
        prev = pv_ref[0]
    zs = zr + (prev - zr) * mu_ref[...]
    o3 = 3 * RW_W
    o4 = o3 + RW_DECAY_LORA
    o5 = o4 + RW_A_LORA
    xr, xk, xv = zs[:, :RW_W], zs[:, RW_W:2 * RW_W], zs[:, 2 * RW_W:o3]
    xw, xa, xg = zs[:, o3:o4], zs[:, o4:o5], zs[:, o5:]
    wl = w0_ref[...] + _dot(jnp.tanh(xw).astype(BF16), ww2_ref[...])
    w_log = -(jnp.maximum(-wl, 0.0) + jnp.log1p(jnp.exp(-jnp.abs(wl)))) - 0.5
    logw = -jnp.exp(w_log)
    a = jax.nn.sigmoid(a0_ref[...] + _dot(xa.astype(BF16), wa2_ref[...]))
    g = _dot(jax.nn.sigmoid(xg).astype(BF16), wg2_ref[...])
    kkf = xk * kk_ref[...]
    kf = xk * (1.0 + (a - 1.0) * ka_ref[...])
    for h in range(RW_HEADS):
        sl = slice(h * RW_N, (h + 1) * RW_N)
        kkh = kkf[:, sl]
        kkh = kkh / jnp.maximum(jnp.sqrt(jnp.sum(kkh * kkh, axis=1, keepdims=True)), 1e-12)
        r_o[0, h] = xr[:, sl]
        k_o[0, h] = kf[:, sl]
        v_o[0, h] = xv[:, sl]
        kk_o[0, h] = kkh
        kka_o[0, h] = kkh * a[:, sl]
        lw_o[0, h] = logw[:, sl]
        g_o[0, h] = g[:, sl]


def rwkv_prep(zr, prev, mu, w0, w_w2, a0, w_a2, w_g2, k_k, k_a, tm):
    nb, m, w = zr.shape
    shift_in_kernel = prev.shape[1] == 1 and m > 1
    vec = lambda n: pl.BlockSpec((1, n), lambda b, i: (0, 0))
    mat = lambda a: pl.BlockSpec(a.shape, lambda b, i: (0, 0))
    out_spec = pl.BlockSpec((1, RW_HEADS, tm, RW_N), lambda b, i: (b, 0, i, 0))
    out_sds = jax.ShapeDtypeStruct((nb, RW_HEADS, m, RW_N), F32)
    prev_spec = (pl.BlockSpec((1, 1, w), lambda b, i: (b, 0, 0)) if shift_in_kernel else
                 pl.BlockSpec((1, tm, w), lambda b, i: (b, i, 0)))
    return pl.pallas_call(
        functools.partial(_rwprep_kernel, shift_in_kernel=shift_in_kernel),
        grid=(nb, m // tm),
        in_specs=[pl.BlockSpec((1, tm, w), lambda b, i: (b, i, 0)),
                  prev_spec,
                  vec(w), vec(RW_W), mat(w_w2), vec(RW_W), mat(w_a2), mat(w_g2), vec(RW_W), vec(RW_W)],
        out_specs=[out_spec] * 7,
        out_shape=[out_sds] * 7,
        scratch_shapes=[pltpu.VMEM((1, w), F32)] if shift_in_kernel else [],
        compiler_params=_params(("parallel", "arbitrary")),
        name="rwkv_prep",
    )(zr, prev, mu.reshape(1, w), w0.reshape(1, RW_W), w_w2, a0.reshape(1, RW_W), w_a2, w_g2,
      k_k.reshape(1, RW_W), k_a.reshape(1, RW_W))


def _split_bf16(x, terms):
    parts = []
    rem = x
    for i in range(terms):
        p = rem.astype(BF16)
        parts.append(p)
        if i + 1 < terms:
            rem = rem - p.astype(F32)
    return parts


def _mm(a, b, ta, tb, dot=_dot):
    ap = _split_bf16(a, ta)
    bp = _split_bf16(b, tb)
    n = max(ta, tb)
    acc = None
    for i, x in enumerate(ap):
        for j, y in enumerate(bp):
            if i + j < n:
                d = dot(x, y)
                acc = d if acc is None else acc + d
    return acc


RW_P_CUMSUM = 2
RW_P_INTRA = 1
RW_P_INV = 1
RW_P_STATE = 1


def _rwkv_scan_kernel(r_ref, k_ref, v_ref, kk_ref, kka_ref, lw_ref, g_ref, rk_ref, lnw_ref, lnb_ref, s0_ref,
                      o_ref, st_ref, *, chunk):
    c = pl.program_id(1)

    @pl.when(c == 0)
    def _():
        st_ref[0] = s0_ref[0]

    c2 = 2 * chunk
    row = lax.broadcasted_iota(jnp.int32, (chunk, chunk), 0)
    col = lax.broadcasted_iota(jnp.int32, (chunk, chunk), 1)
    tri = jnp.where(col <= row, 1.0, 0.0).astype(BF16)
    eye_c = jnp.where(row == col, 1.0, 0.0).astype(F32)
    row2 = lax.broadcasted_iota(jnp.int32, (c2, c2), 0)
    col2 = jnp.bitwise_and(lax.broadcasted_iota(jnp.int32, (c2, c2), 1), chunk - 1)
    mask2 = col2 < jnp.where(row2 < chunk, row2, row2 - (chunk - 1))
    eye_n = lax.broadcasted_iota(jnp.int32, (RW_N, RW_N), 0) == lax.broadcasted_iota(jnp.int32, (RW_N, RW_N), 1)
    zeros_cn = jnp.zeros((chunk, RW_N), F32)
    n_double = int(math.log2(chunk)) - 1
    heads = range(RW_HEADS)
    cs = [_mm(tri, lw_ref[0, h], 1, RW_P_CUMSUM) for h in heads]
    lhs, rhs, g_end, g_end_col = [], [], [], []
    for h in heads:
        g_incl = jnp.exp(cs[h])
        g_prev = jnp.exp(cs[h] - lw_ref[0, h])
        g_inv = jnp.exp(-cs[h])
        cs_last = cs[h][chunk - 1:chunk, :]
        g_end.append(jnp.exp(cs_last))
        g_end_col.append(jnp.exp(jnp.sum(jnp.where(eye_n, jnp.broadcast_to(cs_last, (RW_N, RW_N)), 0.0),
                                         axis=1, keepdims=True)))
        lhs.append(jnp.concatenate([-kk_ref[0, h] * g_prev, r_ref[0, h] * g_incl], axis=0))
        rhs.append(jnp.concatenate([kka_ref[0, h] * g_inv, k_ref[0, h] * g_inv], axis=0))
    mx = [jnp.where(mask2, _mm(lhs[h], rhs[h], RW_P_INTRA, RW_P_INTRA, _nt_dot), 0.0) for h in heads]
    from_state = [_mm(lhs[h], st_ref[0, h], RW_P_STATE, RW_P_STATE) for h in heads]
    from_v = [_mm(mx[h], jnp.concatenate([zeros_cn, v_ref[0, h]], axis=0), RW_P_INTRA, RW_P_INTRA)
              for h in heads]
    l_ab = [mx[h][:chunk, :chunk] for h in heads]
    tinv = [eye_c + l_ab[h] for h in heads]
    pw = [_mm(l_ab[h], l_ab[h], RW_P_INV, RW_P_INV) for h in heads]
    for _ in range(n_double - 1):
        both = [_mm(jnp.concatenate([pw[h], tinv[h]], axis=0), pw[h], RW_P_INV, RW_P_INV) for h in heads]
        tinv = [tinv[h] + both[h][chunk:] for h in heads]
        pw = [both[h][:chunk] for h in heads]
    tinv = [tinv[h] + _mm(tinv[h], pw[h], RW_P_INV, RW_P_INV) for h in heads]
    u = [_mm(tinv[h], from_state[h][:chunk] + from_v[h][:chunk], RW_P_INV, RW_P_INV) for h in heads]
    y_u = [_mm(mx[h][chunk:, :chunk], u[h], RW_P_INTRA, RW_P_INTRA) for h in heads]
    st_add = [_mm(rhs[h] * g_end[h], jnp.concatenate([u[h], v_ref[0, h]], axis=0), RW_P_STATE, RW_P_STATE, _tn_dot)
              for h in heads]
    outs = []
    for h in heads:
        st_ref[0, h] = st_ref[0, h] * g_end_col[h] + st_add[h]
        y = from_state[h][chunk:] + from_v[h][chunk:] + y_u[h]
        mean = jnp.mean(y, axis=1, keepdims=True)
        yc = y - mean
        var = jnp.mean(yc * yc, axis=1, keepdims=True)
        yn = yc * lax.rsqrt(var + RW_GN_EPS) * lnw_ref[h:h + 1, :] + lnb_ref[h:h + 1, :]
        bonus = jnp.sum(r_ref[0, h] * k_ref[0, h] * rk_ref[h:h + 1, :], axis=1, keepdims=True) * v_ref[0, h]
        outs.append((yn + bonus) * g_ref[0, h])
    o_ref[0] = jnp.concatenate(outs, axis=1).astype(BF16)


def rwkv_scan(r, k, v, kk, kka, lw, g, r_k, ln_w, ln_b, s0_t, chunk):
    nb, _, t, _ = r.shape
    tspec = pl.BlockSpec((1, RW_HEADS, chunk, RW_N), lambda b, c: (b, 0, c, 0))
    hspec = pl.BlockSpec((RW_HEADS, RW_N), lambda b, c: (0, 0))
    sspec = pl.BlockSpec((1, RW_HEADS, RW_N, RW_N), lambda b, c: (b, 0, 0, 0))
    return pl.pallas_call(
        functools.partial(_rwkv_scan_kernel, chunk=chunk),
        grid=(nb, t // chunk),
        in_specs=[tspec] * 7 + [hspec] * 3 + [sspec],
        out_specs=[pl.BlockSpec((1, chunk, RW_W), lambda b, c: (b, c, 0)), sspec],
        out_shape=[jax.ShapeDtypeStruct((nb, t, RW_W), BF16),
                   jax.ShapeDtypeStruct((nb, RW_HEADS, RW_N, RW_N), F32)],
        compiler_params=_params(("parallel", "arbitrary")),
        name="rwkv_scan",
    )(r, k, v, kk, kka, lw, g, r_k, ln_w.reshape(RW_HEADS, RW_N), ln_b.reshape(RW_HEADS, RW_N), s0_t)


def _rwkv_step_kernel(r_ref, k_ref, v_ref, kk_ref, kka_ref, lw_ref, g_ref, rk_ref, lnw_ref, lnb_ref, s0_ref,
                      o_ref, s_ref, w_scr, y_scr, *, t_new):
    for t in range(t_new):
        w_scr[t] = jnp.exp(lw_ref[t, 0])

    def value_row(vi, carry):
        s = s0_ref[0, vi]
        for t in range(t_new):
            sa = -jnp.sum(s * kk_ref[t, 0], axis=0, keepdims=True)
            s = s * w_scr[t] + sa * kka_ref[t, 0] + v_ref[t, 0, pl.ds(vi, 1), :] * k_ref[t, 0]
            y_scr[t, pl.ds(vi, 1), :] = jnp.sum(s * r_ref[t, 0], axis=0, keepdims=True)
        s_ref[0, vi] = s
        return carry

    lax.fori_loop(0, RW_N, value_row, 0)
    for t in range(t_new):
        y = y_scr[t]
        mean = jnp.mean(y, axis=0, keepdims=True)
        yc = y - mean
        var = jnp.mean(yc * yc, axis=0, keepdims=True)
        yn = yc * lax.rsqrt(var + RW_GN_EPS) * lnw_ref[0] + lnb_ref[0]
        bonus = jnp.sum(r_ref[t, 0] * k_ref[t, 0] * rk_ref[0], axis=0, keepdims=True) * v_ref[t, 0]
        o_ref[t, 0] = (yn + bonus) * g_ref[t, 0]


def rwkv_step(r, k, v, kk, kka, lw, g, r_k, ln_w, ln_b, s0):
    t_new, nh, n, nb = r.shape
    tspec = pl.BlockSpec((t_new, 1, n, nb), lambda h: (0, h, 0, 0))
    hspec = pl.BlockSpec((1, n, nb), lambda h: (h, 0, 0))
    sspec = pl.BlockSpec((1, n, n, nb), lambda h: (h, 0, 0, 0))
    return pl.pallas_call(
        functools.partial(_rwkv_step_kernel, t_new=t_new),
        grid=(nh,),
        in_specs=[tspec] * 7 + [hspec] * 3 + [sspec],
        out_specs=[tspec, sspec],
        out_shape=[jax.ShapeDtypeStruct((t_new, nh, n, nb), F32), jax.ShapeDtypeStruct((nh, n, n, nb), F32)],
        scratch_shapes=[pltpu.VMEM((t_new, n, nb), F32), pltpu.VMEM((t_new, n, nb), F32)],
        compiler_params=_params(("parallel",)),
        name="rwkv_step",
    )(r, k, v, kk, kka, lw, g, r_k, ln_w, ln_b, s0)


def _retention_kernel(lg_ref, qk_ref, v_ref, g_ref, cos_ref, sin_ref, s0_ref, o_ref, s_ref, *, lb, l_true):
    c = pl.program_id(1)

    @pl.when(c == 0)
    def _():
        s_ref[0] = s0_ref[0]

    lp = max(lb, 16)
    cos = cos_ref[0]
    sin = sin_ref[0]
    half = RET_DK // 2
    qk_w = RET_HEADS * RET_DK

    def rope(x):
        x1, x2 = x[:, :half], x[:, half:]
        return jnp.concatenate([x1 * cos - x2 * sin, x1 * sin + x2 * cos], axis=1)

    def rows(x):
        if lp == lb:
            return x
        return jnp.concatenate([x, jnp.zeros((lp - lb, x.shape[1]), x.dtype)], axis=0)

    row = lax.broadcasted_iota(jnp.int32, (lp, lp), 0)
    col = lax.broadcasted_iota(jnp.int32, (lp, lp), 1)
    diff = (row - col).astype(F32)
    idx = lax.broadcasted_iota(jnp.int32, (lp, 1), 0).astype(F32)
    heads = range(RET_HEADS)
    qm, km, kdm, vm, dmask, row_dec = [], [], [], [], [], []
    for h in heads:
        lg = lg_ref[h]
        q = rows(rope(qk_ref[0, :, h * RET_DK:(h + 1) * RET_DK]))
        k = rows(rope(qk_ref[0, :, qk_w + h * RET_DK:qk_w + (h + 1) * RET_DK]) * (RET_DK ** -0.5))
        qm.append(q.astype(BF16))
        km.append(k.astype(BF16))
        kdm.append((k * jnp.exp((l_true - 1.0 - idx) * lg)).astype(BF16))
        vm.append(rows(v_ref[0, :, h * RET_DV:(h + 1) * RET_DV]).astype(BF16))
        dmask.append(jnp.where(diff >= 0, jnp.exp(jnp.maximum(diff, 0.0) * lg), 0.0))
        row_dec.append(jnp.exp((idx + 1.0) * lg))
    sc = [(_nt_dot(qm[h], km[h]) * dmask[h]).astype(BF16) for h in heads]
    cross = [_dot(qm[h], s_ref[0, h].astype(BF16)) * row_dec[h] for h in heads]
    s_add = [_tn_dot(kdm[h], vm[h]) for h in heads]
    inner = [_dot(sc[h], vm[h]) for h in heads]
    outs = []
    for h in heads:
        s_dec = jnp.exp(jnp.zeros((1, RET_DV), F32) + l_true * lg_ref[h])
        s_ref[0, h] = s_ref[0, h] * s_dec + s_add[h]
        o = (inner[h] + cross[h])[:lb]
        o = o * lax.rsqrt(jnp.mean(o * o, axis=1, keepdims=True) + NORM_EPS)
        gv = g_ref[0, :, h * RET_DV:(h + 1) * RET_DV]
        outs.append(o * (gv * jax.nn.sigmoid(gv)))
    o_ref[0] = jnp.concatenate(outs, axis=1).astype(BF16)


def retention(z, cos, sin, lg, s0, lb, l_true):
    nb, m, _ = z.shape
    vw = RET_HEADS * RET_DV
    assert 2 * RET_HEADS * RET_DK == vw
    sspec = pl.BlockSpec((1, RET_HEADS, RET_DK, RET_DV), lambda b, c: (b, 0, 0, 0))
    return pl.pallas_call(
        functools.partial(_retention_kernel, lb=lb, l_true=float(l_true)),
        grid=(nb, m // lb),
        in_specs=[pl.BlockSpec(memory_space=pltpu.SMEM),
                  pl.BlockSpec((1, lb, vw), lambda b, c: (b, c, 0)),
                  pl.BlockSpec((1, lb, vw), lambda b, c: (b, c, 1)),
                  pl.BlockSpec((1, lb, vw), lambda b, c: (b, c, 2)),
                  pl.BlockSpec((1, lb, RET_DK // 2), lambda b, c: (0, c, 0)),
                  pl.BlockSpec((1, lb, RET_DK // 2), lambda b, c: (0, c, 0)),
                  sspec],
        out_specs=[pl.BlockSpec((1, lb, vw), lambda b, c: (b, c, 0)), sspec],
        out_shape=[jax.ShapeDtypeStruct((nb, m, vw), BF16),
                   jax.ShapeDtypeStruct((nb, RET_HEADS, RET_DK, RET_DV), F32)],
        compiler_params=_params(("parallel", "arbitrary")),
        name="retention",
    )(lg, z, z, z, cos, sin, s0)


def _rope_tables(pos, half):
    inv = ROPE_BASE ** (-jnp.arange(half, dtype=F32) / half)
    ang = pos.astype(F32)[:, None] * inv[None, :]
    return jnp.cos(ang), jnp.sin(ang)


def _mla_tables(pos):
    cos, sin = _rope_tables(pos, MLA_ROPE // 2)
    return jnp.concatenate([cos, cos], axis=1), jnp.concatenate([-sin, sin], axis=1)


def _even_layer(x, mods, pos_tabs, prm, past, tm):
    (w_in_p, g_mix, g_kv, wuk_t, wuk_r, wuv_t, wuv_all, mu, w0, w_w2, a0, w_a2, w_g2, k_k, k_a, r_k, ln_w, ln_b,
     w_out_mla, w_out_rw) = prm
    sh1, sc1, gt1 = mods
    cf, sf = pos_tabs
    nb, m, _ = x.shape
    zr, zq, zkv = norm_mod_matmul_split(x, g_mix, sh1, sc1, w_in_p, (RW_SHIFT_W, MLA_HEADS * (MLA_NOPE + MLA_ROPE), MLA_QK), tm)
    lat, kr, kcat, latt = kv_prep(zkv, g_kv, cf, sf, tm)
    if past is None:
        qt = q_prep_t(zq, wuk_r, cf, sf, min(256, m))
        mla_out = mla_prompt(qt, kcat, latt, wuv_t, 512)
        mla_transposed = True
        s0_t = jnp.zeros((nb, RW_HEADS, RW_N, RW_N), F32)
        tens = rwkv_prep(zr, jnp.zeros((nb, 1, RW_SHIFT_W), F32), mu, w0, w_w2, a0, w_a2, w_g2, k_k, k_a, min(tm, 256))
        rw_out, s_t = rwkv_scan(*tens, r_k, ln_w, ln_b, s0_t, RW_CHUNK)
        s_new = jnp.swapaxes(s_t, -1, -2)
        shift_new = zr[:, -1]
    else:
        cache_lat, cache_kr, layer, page_table, s0, shift_prev, t_new = past
        nbs = m // t_new
        mla_transposed = False
        qcat = q_prep(zq, wuk_t, cf, sf, tm)
        q_s = qcat.reshape(MLA_HEADS, nbs, t_new, MLA_QK).transpose(1, 2, 0, 3).reshape(nbs, t_new * MLA_HEADS, MLA_QK)
        kn_s = jnp.concatenate([lat, kr], axis=-1).reshape(nbs, t_new, MLA_QK)
        mla_out = mla_sample(q_s, kn_s, wuv_all, cache_lat, cache_kr, layer, page_table).reshape(1, m, MLA_HEADS * MLA_V)
        zr_b = zr.reshape(nbs, t_new, RW_SHIFT_W)
        prev = jnp.concatenate([shift_prev[:, None, :], zr_b[:, :-1]], axis=1).reshape(1, m, RW_SHIFT_W)
        tens = rwkv_prep(zr, prev, mu, w0, w_w2, a0, w_a2, w_g2, k_k, k_a, min(tm, 256))
        tens = [u.reshape(RW_HEADS, nbs, t_new, RW_N).transpose(2, 0, 3, 1) for u in tens]
        lanes = lambda p: jnp.broadcast_to(p.reshape(RW_HEADS, RW_N, 1), (RW_HEADS, RW_N, nbs))
        rw_l, s_l = rwkv_step(*tens, lanes(r_k), lanes(ln_w), lanes(ln_b), jnp.transpose(s0, (1, 2, 3, 0)))
        rw_out = rw_l.transpose(3, 0, 1, 2).reshape(1, m, RW_W).astype(BF16)
        s_new = jnp.transpose(s_l, (3, 0, 1, 2))
        shift_new = zr_b[:, -1]
    x_new = matmul_gate_res([mla_out, rw_out], [w_out_mla, w_out_rw], x, gt1, tm, (mla_transposed, False))
    return x_new, (lat, kr, s_new, shift_new)


def _odd_layer(x, mods, ret_tabs, prm, s0, t_new, tm):
    w_in, g_mix, w_out, lg = prm
    sh1, sc1, gt1 = mods
    cos, sin = ret_tabs
    nb, m, _ = x.shape
    z = norm_mod_matmul(x, g_mix, sh1, sc1, w_in, tm, 2048)
    if s0 is None:
        s0 = jnp.zeros((nb, RET_HEADS, RET_DK, RET_DV), F32)
        o, s_new = retention(z, cos, sin, lg, s0, RET_CHUNK, RET_CHUNK)
    else:
        nbs = m // t_new
        lpad = 8
        z_b = jnp.pad(z.reshape(nbs, t_new, -1), ((0, 0), (0, lpad - t_new), (0, 0)))
        o, s_new = retention(z_b, cos, sin, lg, s0, lpad, t_new)
        o = o[:, :t_new].reshape(1, m, RET_HEADS * RET_DV)
    x_new = matmul_gate_res([o], [w_out], x, gt1, tm)
    return x_new, s_new


def kernel(x_prompt, x_sample, c_prompt, c_sample, cache_kv_latent, cache_k_rope, page_table, state_rwkv, state_rwkv_shift, state_ret, w_ada, b_ada, g_norm_mix, g_norm_mlp, g_final, w_in_even, g_kv, w_uk, w_uv, rw_mu, rw_w0, rw_w2, rw_a0, rw_a2, rw_g2, rw_k_k, rw_k_a, rw_r_k, rw_ln_w, rw_ln_b, w_out_even, w_in_odd, w_out_odd, w_ff1, w_ff2):
    nbp, t_p, d = x_prompt.shape
    nbs, t_s, _ = x_sample.shape
    depth = w_ada.shape[0]
    past_len = page_table.shape[1] * PAGE_SIZE
    m_s = nbs * t_s
    tm_p = min(512, t_p)
    tm_s = m_s

    c_all = jnp.concatenate([c_prompt, c_sample], axis=0)
    c_all = jnp.pad(c_all, ((0, -c_all.shape[0] % 16), (0, 0)))
    mods_all = ada_proj(c_all, w_ada, b_ada)

    def group_mods(l):
        mp = mods_all[l, :nbp].reshape(nbp, 1, 6, d)
        ms = jnp.broadcast_to(mods_all[l, nbp:nbp + nbs].reshape(nbs, 1, 6, d), (nbs, t_s, 6, d)).reshape(1, m_s, 6, d)
        return [mp[:, :, i] for i in range(6)], [ms[:, :, i] for i in range(6)]

    pos_p = jnp.arange(t_p)
    pos_s = past_len + jnp.arange(t_s)
    cf_p, sf_p = _mla_tables(pos_p)
    cf_s, sf_s = _mla_tables(pos_s)
    mla_tabs_p = (cf_p[None], sf_p[None])
    mla_tabs_s = (jnp.tile(cf_s, (nbs, 1))[None], jnp.tile(sf_s, (nbs, 1))[None])
    cr_p, sr_p = _rope_tables(pos_p, RET_DK // 2)
    cr_s, sr_s = _rope_tables(pos_s, RET_DK // 2)
    ret_tabs_p = (cr_p[None], sr_p[None])
    ret_tabs_s = (jnp.pad(cr_s, ((0, 8 - t_s), (0, 0)))[None], jnp.pad(sr_s, ((0, 8 - t_s), (0, 0)))[None])
    lg = jnp.log(1 - 2.0 ** (-5.0 - jnp.arange(RET_HEADS, dtype=F32)))

    xp = x_prompt
    xs = x_sample.reshape(1, m_s, d)
    lat_p, kr_p, rw_p, sh_p, ret_p = [], [], [], [], []
    lat_s, kr_s, rw_s, sh_s, ret_s = [], [], [], [], []
    q_w = MLA_HEADS * (MLA_NOPE + MLA_ROPE)
    for l in range(depth):
        (sh1p, sc1p, gt1p, sh2p, sc2p, gt2p), (sh1s, sc1s, gt1s, sh2s, sc2s, gt2s) = group_mods(l)
        i = l // 2
        if l % 2 == 0:
            w_in = w_in_even[i]
            wq = w_in[:, :q_w].reshape(d, MLA_HEADS, MLA_NOPE + MLA_ROPE)
            w_in_p = jnp.concatenate([w_in[:, q_w + MLA_QK:],
                                      wq[:, :, :MLA_NOPE].reshape(d, -1), wq[:, :, MLA_NOPE:].reshape(d, -1),
                                      w_in[:, q_w:q_w + MLA_QK]], axis=1).astype(BF16)
            wuv = w_uv[i]
            mla_w = MLA_HEADS * MLA_V
            prm = (w_in_p, g_norm_mix[l], g_kv[i], jnp.transpose(w_uk[i], (1, 2, 0)).astype(BF16),
                   jnp.transpose(w_uk[i], (1, 0, 2)).astype(BF16), jnp.transpose(wuv, (1, 2, 0)).astype(BF16), wuv.reshape(KV_RANK, mla_w).astype(BF16),
                   rw_mu[i], rw_w0[i], rw_w2[i].astype(BF16), rw_a0[i], rw_a2[i].astype(BF16), rw_g2[i].astype(BF16),
                   rw_k_k[i], rw_k_a[i], rw_r_k[i], rw_ln_w[i], rw_ln_b[i],
                   w_out_even[i, :mla_w].astype(BF16), w_out_even[i, mla_w:].astype(BF16))
            xp, (la, kr, st, sh) = _even_layer(xp, (sh1p, sc1p, gt1p), mla_tabs_p, prm, None, tm_p)
            lat_p.append(la); kr_p.append(kr); rw_p.append(st); sh_p.append(sh)
            cache_kr_t = jnp.swapaxes(cache_k_rope, 2, 3)
            past = (cache_kv_latent, cache_kr_t, i, page_table, state_rwkv[i], state_rwkv_shift[i], t_s)
            xs, (la, kr, st, sh) = _even_layer(xs, (sh1s, sc1s, gt1s), mla_tabs_s, prm, past, tm_s)
            lat_s.append(la.reshape(nbs, t_s, KV_RANK)); kr_s.append(kr.reshape(nbs, t_s, MLA_ROPE))
            rw_s.append(st); sh_s.append(sh)
        else:
            prm = (w_in_odd[i].astype(BF16), g_norm_mix[l], w_out_odd[i].astype(BF16), lg)
            xp, st = _odd_layer(xp, (sh1p, sc1p, gt1p), ret_tabs_p, prm, None, t_s, tm_p)
            ret_p.append(st)
            xs, st = _odd_layer(xs, (sh1s, sc1s, gt1s), ret_tabs_s, prm, state_ret[i], t_s, tm_s)
            ret_s.append(st)
        final = l == depth - 1
        w1 = w_ff1[l].astype(BF16)
        w2 = w_ff2[l].astype(BF16)
        xp = mlp_block(xp, g_norm_mlp[l], sh2p, sc2p, gt2p, w1, w2, g_final, final, tm_p, 1024)
        xs = mlp_block(xs, g_norm_mlp[l], sh2s, sc2s, gt2s, w1, w2, g_final, final, tm_s, 1024)
    return (xp, xs.reshape(nbs, t_s, d),
            jnp.stack(lat_p), jnp.stack(kr_p), jnp.stack(rw_p), jnp.stack(sh_p), jnp.stack(ret_p),
            jnp.stack(lat_s), jnp.stack(kr_s), jnp.stack(rw_s), jnp.stack(sh_s), jnp.stack(ret_s))
```

```python
import functools
import math

import jax
import jax.numpy as jnp
from jax import lax
from jax.experimental import pallas as pl
from jax.experimental.pallas import tpu as pltpu

F32 = jnp.float32
BF16 = jnp.bfloat16
HIGHEST = lax.Precision.HIGHEST

D_MODEL = 1024
PAGE_SIZE = 128
MLA_HEADS = 8
MLA_NOPE = 64
MLA_ROPE = 32
MLA_V = 64
KV_RANK = 256
MLA_QK = KV_RANK + MLA_ROPE
MLA_SCALE = (MLA_NOPE + MLA_ROPE) ** -0.5
MLA_QSCALE = MLA_SCALE * math.log2(math.e)
RW_HEADS = 8
RW_N = 64
RW_W = RW_HEADS * RW_N
RW_DECAY_LORA = 64
RW_A_LORA = 64
RW_G_LORA = 128
RW_SHIFT_W = 3 * RW_W + RW_DECAY_LORA + RW_A_LORA + RW_G_LORA
RW_GN_EPS = 64e-5
RW_CHUNK = 64
RET_HEADS = 4
RET_DK = 256
RET_DV = 512
RET_BLOCK = 256
D_FF = 4 * D_MODEL
ROPE_BASE = 10000.0
NORM_EPS = 1e-6
MIB = 1024 * 1024


def _params(sem, vmem_mib=48):
    return pltpu.CompilerParams(dimension_semantics=sem, vmem_limit_bytes=vmem_mib * MIB)


def _rms(x, g):
    return x * lax.rsqrt(jnp.mean(x * x, axis=-1, keepdims=True) + NORM_EPS) * g


def _nt_dot(a, b, precision=None):
    return lax.dot_general(a, b, (((1,), (1,)), ((), ())), precision=precision, preferred_element_type=F32)


def _tn_dot(a, b, precision=None):
    return lax.dot_general(a, b, (((0,), (0,)), ((), ())), precision=precision, preferred_element_type=F32)


def _dot(a, b, precision=None):
    return jnp.dot(a, b, precision=precision, preferred_element_type=F32)


def _mod_arg(mod):
    return mod[0] if isinstance(mod, tuple) else mod


def _mod_spec(mod, tm, nmid):
    if isinstance(mod, tuple):
        _, layer, which = mod
        if nmid == 2:
            return pl.BlockSpec((1, tm, D_MODEL), lambda b, m, j: (layer, m, which))
        return pl.BlockSpec((1, tm, D_MODEL), lambda b, m: (layer, m, which))
    if mod.shape[1] == 1:
        if nmid == 2:
            return pl.BlockSpec((1, 1, mod.shape[2]), lambda b, m, j: (b, 0, 0))
        return pl.BlockSpec((1, 1, mod.shape[2]), lambda b, m: (b, 0, 0))
    if nmid == 2:
        return pl.BlockSpec((1, tm, mod.shape[2]), lambda b, m, j: (b, m, 0))
    return pl.BlockSpec((1, tm, mod.shape[2]), lambda b, m: (b, m, 0))


def _ada_kernel(c_ref, w_ref, b_ref, o_ref):
    o_ref[0] = _dot(c_ref[...].astype(BF16), w_ref[0].astype(BF16)) + b_ref[0]


def ada_proj(c, w_ada, b_ada):
    nl, d, n = w_ada.shape
    r = c.shape[0]
    tn = 1536
    return pl.pallas_call(
        _ada_kernel,
        grid=(nl, n // tn),
        in_specs=[pl.BlockSpec((r, d), lambda l, j: (0, 0)),
                  pl.BlockSpec((1, d, tn), lambda l, j: (l, 0, j)),
                  pl.BlockSpec((1, 1, tn), lambda l, j: (l, 0, j))],
        out_specs=pl.BlockSpec((1, r, tn), lambda l, j: (l, 0, j)),
        out_shape=jax.ShapeDtypeStruct((nl, r, n), F32),
        compiler_params=_params(("parallel", "parallel")),
        name="ada_proj",
    )(c, w_ada, b_ada.reshape(nl, 1, n))


def _nmm_split_kernel(x_ref, g_ref, sh_ref, sc_ref, w_ref, *o_refs, splits):
    h = (_rms(x_ref[0], g_ref[...]) * (1.0 + sc_ref[0]) + sh_ref[0]).astype(BF16)
    off = 0
    for o_ref, n in zip(o_refs, splits):
        o_ref[0] = _dot(h, w_ref[:, off:off + n])
        off += n


def norm_mod_matmul_split(x, g, shift, scale, w, splits, tm):
    nb, m, d = x.shape
    n = w.shape[1]
    return pl.pallas_call(
        functools.partial(_nmm_split_kernel, splits=splits),
        grid=(nb, m // tm),
        in_specs=[pl.BlockSpec((1, tm, d), lambda b, i: (b, i, 0)),
                  pl.BlockSpec((1, d), lambda b, i: (0, 0)),
                  _mod_spec(shift, tm, 1), _mod_spec(scale, tm, 1),
                  pl.BlockSpec((d, n), lambda b, i: (0, 0))],
        out_specs=[pl.BlockSpec((1, tm, s), lambda b, i: (b, i, 0)) for s in splits],
        out_shape=[jax.ShapeDtypeStruct((nb, m, s), F32) for s in splits],
        compiler_params=_params(("parallel", "parallel")),
        name="norm_mod_matmul_split",
    )(x, g.reshape(1, d), _mod_arg(shift), _mod_arg(scale), w)


def _nmm_kernel(x_ref, g_ref, sh_ref, sc_ref, w_ref, o_ref):
    h = (_rms(x_ref[0], g_ref[...]) * (1.0 + sc_ref[0]) + sh_ref[0]).astype(BF16)
    o_ref[0] = _dot(h, w_ref[...])


def norm_mod_matmul(x, g, shift, scale, w, tm, tn):
    nb, m, d = x.shape
    n = w.shape[1]
    return pl.pallas_call(
        _nmm_kernel,
        grid=(nb, m // tm, n // tn),
        in_specs=[pl.BlockSpec((1, tm, d), lambda b, i, j: (b, i, 0)),
                  pl.BlockSpec((1, d), lambda b, i, j: (0, 0)),
                  _mod_spec(shift, tm, 2), _mod_spec(scale, tm, 2),
                  pl.BlockSpec((d, tn), lambda b, i, j: (0, j))],
        out_specs=pl.BlockSpec((1, tm, tn), lambda b, i, j: (b, i, j)),
        out_shape=jax.ShapeDtypeStruct((nb, m, n), F32),
        compiler_params=_params(("parallel", "parallel", "arbitrary")),
        name="norm_mod_matmul",
    )(x, g.reshape(1, d), _mod_arg(shift), _mod_arg(scale), w)


def _mgr_kernel(*refs, n_pairs, transposed):
    a_refs = refs[:n_pairs]
    w_refs = refs[n_pairs:2 * n_pairs]
    res_ref, gt_ref, o_ref = refs[2 * n_pairs:]
    acc = None
    for a_ref, w_ref, tr in zip(a_refs, w_refs, transposed):
        d = (_tn_dot if tr else _dot)(a_ref[0], w_ref[...])
        acc = d if acc is None else acc + d
    o_ref[0] = res_ref[0] + gt_ref[0] * acc


def matmul_gate_res(a_list, w_list, res, gate, tm, transposed=None):
    nb, m, d = res.shape
    n_pairs = len(a_list)
    transposed = tuple(transposed or (False,) * n_pairs)
    in_specs = [pl.BlockSpec((1, a.shape[1], tm), lambda b, i: (b, 0, i)) if tr else
                pl.BlockSpec((1, tm, a.shape[2]), lambda b, i: (b, i, 0)) for a, tr in zip(a_list, transposed)]
    in_specs += [pl.BlockSpec(w.shape, lambda b, i: (0, 0)) for w in w_list]
    in_specs += [pl.BlockSpec((1, tm, d), lambda b, i: (b, i, 0)), _mod_spec(gate, tm, 1)]
    return pl.pallas_call(
        functools.partial(_mgr_kernel, n_pairs=n_pairs, transposed=transposed),
        grid=(nb, m // tm),
        in_specs=in_specs,
        out_specs=pl.BlockSpec((1, tm, d), lambda b, i: (b, i, 0)),
        out_shape=jax.ShapeDtypeStruct((nb, m, d), F32),
        compiler_params=_params(("parallel", "parallel")),
        name="matmul_gate_res",
    )(*a_list, *w_list, res, _mod_arg(gate))


def _mlp_kernel(x_ref, g_ref, sh_ref, sc_ref, gt_ref, w1_ref, w2_ref, gf_ref, o_ref, h_scr, acc_scr, *, final):
    f = pl.program_id(2)

    @pl.when(f == 0)
    def _():
        h_scr[...] = (_rms(x_ref[0], g_ref[...]) * (1.0 + sc_ref[0]) + sh_ref[0]).astype(BF16)
        acc_scr[...] = jnp.zeros_like(acc_scr)

    a = _dot(h_scr[...], w1_ref[...])
    a = jnp.square(jnp.maximum(a, 0.0)).astype(BF16)
    acc_scr[...] += _dot(a, w2_ref[...])

    @pl.when(f == pl.num_programs(2) - 1)
    def _():
        y = x_ref[0] + gt_ref[0] * acc_scr[...]
        if final:
            y = _rms(y, gf_ref[...])
        o_ref[0] = y


def mlp_block(x, g, shift, scale, gate, w1, w2, g_final, final, tm, tf):
    nb, m, d = x.shape
    dff = w1.shape[1]
    return pl.pallas_call(
        functools.partial(_mlp_kernel, final=final),
        grid=(nb, m // tm, dff // tf),
        in_specs=[pl.BlockSpec((1, tm, d), lambda b, i, f: (b, i, 0)),
                  pl.BlockSpec((1, d), lambda b, i, f: (0, 0)),
                  _mod_spec(shift, tm, 2), _mod_spec(scale, tm, 2), _mod_spec(gate, tm, 2),
                  pl.BlockSpec((d, tf), lambda b, i, f: (0, f)),
                  pl.BlockSpec((tf, d), lambda b, i, f: (f, 0)),
                  pl.BlockSpec((1, d), lambda b, i, f: (0, 0))],
        out_specs=pl.BlockSpec((1, tm, d), lambda b, i, f: (b, i, 0)),
        out_shape=jax.ShapeDtypeStruct((nb, m, d), F32),
        scratch_shapes=[pltpu.VMEM((tm, d), BF16), pltpu.VMEM((tm, d), F32)],
        compiler_params=_params(("parallel", "parallel", "arbitrary"), 56),
        name="mlp_block",
    )(x, g.reshape(1, d), _mod_arg(shift), _mod_arg(scale), _mod_arg(gate), w1, w2, g_final.reshape(1, d))


def _rope32(x, cf, sf):
    half = MLA_ROPE // 2
    sw = jnp.concatenate([x[:, half:], x[:, :half]], axis=1)
    return x * cf + sw * sf


def _eye_bf16(n):
    return jnp.where(lax.broadcasted_iota(jnp.int32, (n, n), 0) == lax.broadcasted_iota(jnp.int32, (n, n), 1),
                     1.0, 0.0).astype(BF16)


def _kvprep_kernel(zkv_ref, g_ref, cf_ref, sf_ref, lat_ref, kr_ref, kcat_ref, latt_ref):
    z = zkv_ref[0]
    lat = _rms(z[:, :KV_RANK], g_ref[...])
    kr = _rope32(z[:, KV_RANK:], cf_ref[0], sf_ref[0])
    lat_ref[0] = lat
    kr_ref[0] = kr
    lat_b = lat.astype(BF16)
    kcat_ref[0, :, :KV_RANK] = lat_b
    kcat_ref[0, :, KV_RANK:] = kr.astype(BF16)
    latt_ref[0] = _nt_dot(_eye_bf16(KV_RANK), lat_b).astype(BF16)


def kv_prep(zkv, g_kv, cf, sf, tm):
    nb, m, _ = zkv.shape
    return pl.pallas_call(
        _kvprep_kernel,
        grid=(nb, m // tm),
        in_specs=[pl.BlockSpec((1, tm, MLA_QK), lambda b, i: (b, i, 0)),
                  pl.BlockSpec((1, KV_RANK), lambda b, i: (0, 0)),
                  pl.BlockSpec((1, tm, MLA_ROPE), lambda b, i: (0, i, 0)),
                  pl.BlockSpec((1, tm, MLA_ROPE), lambda b, i: (0, i, 0))],
        out_specs=[pl.BlockSpec((1, tm, KV_RANK), lambda b, i: (b, i, 0)),
                   pl.BlockSpec((1, tm, MLA_ROPE), lambda b, i: (b, i, 0)),
                   pl.BlockSpec((1, tm, MLA_QK), lambda b, i: (b, i, 0)),
                   pl.BlockSpec((1, KV_RANK, tm), lambda b, i: (b, 0, i))],
        out_shape=[jax.ShapeDtypeStruct((nb, m, KV_RANK), F32),
                   jax.ShapeDtypeStruct((nb, m, MLA_ROPE), F32),
                   jax.ShapeDtypeStruct((nb, m, MLA_QK), BF16),
                   jax.ShapeDtypeStruct((nb, KV_RANK, m), BF16)],
        compiler_params=_params(("parallel", "parallel")),
        name="kv_prep",
    )(zkv, g_kv.reshape(1, KV_RANK), cf, sf)


def _qprep_kernel(zq_ref, wuk_ref, cf_ref, sf_ref, o_ref):
    z = zq_ref[0]
    cf = cf_ref[0]
    sf = sf_ref[0]
    nope_w = MLA_HEADS * MLA_NOPE
    for h in range(MLA_HEADS):
        qn = z[:, h * MLA_NOPE:(h + 1) * MLA_NOPE].astype(BF16)
        ql = _dot(qn, wuk_ref[h]) * MLA_QSCALE
        qr = _rope32(z[:, nope_w + h * MLA_ROPE:nope_w + (h + 1) * MLA_ROPE], cf, sf) * MLA_QSCALE
        o_ref[0, h, :, :KV_RANK] = ql.astype(BF16)
        o_ref[0, h, :, KV_RANK:] = qr.astype(BF16)


def _qprep_t_kernel(zq_ref, wuk_ref, cf_ref, sf_ref, o_ref, *, tq):
    z = zq_ref[0]
    cf = cf_ref[0]
    sf = sf_ref[0]
    nope_w = MLA_HEADS * MLA_NOPE
    eye = _eye_bf16(MLA_ROPE)
    for h in range(MLA_HEADS):
        qn = z[:, h * MLA_NOPE:(h + 1) * MLA_NOPE].astype(BF16)
        ql_t = _nt_dot(wuk_ref[h], qn) * MLA_QSCALE
        qr = _rope32(z[:, nope_w + h * MLA_ROPE:nope_w + (h + 1) * MLA_ROPE], cf, sf) * MLA_QSCALE
        qr_t = _nt_dot(eye, qr.astype(BF16))
        o_ref[0, 0, :KV_RANK, h * tq:(h + 1) * tq] = ql_t.astype(BF16)
        o_ref[0, 0, KV_RANK:, h * tq:(h + 1) * tq] = qr_t.astype(BF16)


def q_prep_t(zq, wuk_r, cf, sf, tq):
    nb, m, w = zq.shape
    return pl.pallas_call(
        functools.partial(_qprep_t_kernel, tq=tq),
        grid=(nb, m // tq),
        in_specs=[pl.BlockSpec((1, tq, w), lambda b, i: (b, i, 0)),
                  pl.BlockSpec((MLA_HEADS, KV_RANK, MLA_NOPE), lambda b, i: (0, 0, 0)),
                  pl.BlockSpec((1, tq, MLA_ROPE), lambda b, i: (0, i, 0)),
                  pl.BlockSpec((1, tq, MLA_ROPE), lambda b, i: (0, i, 0))],
        out_specs=pl.BlockSpec((1, 1, MLA_QK, MLA_HEADS * tq), lambda b, i: (b, i, 0, 0)),
        out_shape=jax.ShapeDtypeStruct((nb, m // tq, MLA_QK, MLA_HEADS * tq), BF16),
        compiler_params=_params(("parallel", "parallel")),
        name="q_prep_t",
    )(zq, wuk_r, cf, sf)


def q_prep(zq, wuk_t, cf, sf, tm):
    nb, m, w = zq.shape
    return pl.pallas_call(
        _qprep_kernel,
        grid=(nb, m // tm),
        in_specs=[pl.BlockSpec((1, tm, w), lambda b, i: (b, i, 0)),
                  pl.BlockSpec((MLA_HEADS, MLA_NOPE, KV_RANK), lambda b, i: (0, 0, 0)),
                  pl.BlockSpec((1, tm, MLA_ROPE), lambda b, i: (0, i, 0)),
                  pl.BlockSpec((1, tm, MLA_ROPE), lambda b, i: (0, i, 0))],
        out_specs=pl.BlockSpec((1, MLA_HEADS, tm, MLA_QK), lambda b, i: (b, 0, i, 0)),
        out_shape=jax.ShapeDtypeStruct((nb, MLA_HEADS, m, MLA_QK), BF16),
        compiler_params=_params(("parallel", "parallel")),
        name="q_prep",
    )(zq, wuk_t, cf, sf)


MLA_PROMPT_COL_GROUPS = 4


def _mla_prompt_kernel(qi_ref, ki_ref, qt_ref, k_ref, latt_ref, wuvt_ref, o_ref, m_scr, l_scr, acc_scr, *, tq, tk):
    step = pl.program_id(1)
    qi = qi_ref[step]
    ki = ki_ref[step]
    last_k = (qi * tq + (tq - 1)) // tk

    @pl.when(ki == 0)
    def _():
        m_scr[...] = jnp.full_like(m_scr, -jnp.inf)
        l_scr[...] = jnp.zeros_like(l_scr)
        acc_scr[...] = jnp.zeros_like(acc_scr)

    def update(masked):
        rows = MLA_HEADS * tq
        cw = rows // MLA_PROMPT_COL_GROUPS
        groups = [slice(g * cw, (g + 1) * cw) for g in range(MLA_PROMPT_COL_GROUPS)]
        st_next = _dot(k_ref[0], qt_ref[0, 0, :, groups[0]])
        for g, cs in enumerate(groups):
            st = st_next
            if masked:
                kpos = ki * tk + lax.broadcasted_iota(jnp.int32, st.shape, 0)
                col = cs.start + lax.broadcasted_iota(jnp.int32, st.shape, 1)
                qpos = qi * tq + jnp.bitwise_and(col, tq - 1)
                st = jnp.where(kpos <= qpos, st, -jnp.inf)
            m_prev = m_scr[:, cs]
            m_new = jnp.maximum(m_prev, jnp.max(st, axis=0, keepdims=True))
            alpha = jnp.exp2(m_prev - m_new)
            pt = jnp.exp2(st - m_new)
            l_scr[:, cs] = alpha * l_scr[:, cs] + jnp.sum(pt, axis=0, keepdims=True)
            m_scr[:, cs] = m_new
            if g + 1 < len(groups):
                st_next = _dot(k_ref[0], qt_ref[0, 0, :, groups[g + 1]])
            acc_scr[:, cs] = alpha * acc_scr[:, cs] + _dot(latt_ref[0], pt.astype(BF16))

    needs_mask = ki * tk + (tk - 1) > qi * tq

    @pl.when(needs_mask)
    def _():
        update(True)

    @pl.when(jnp.logical_not(needs_mask))
    def _():
        update(False)

    @pl.when(ki == last_k)
    def _():
        ot = (acc_scr[...] / l_scr[...]).astype(BF16)
        for h in range(MLA_HEADS):
            o_ref[0, h * MLA_V:(h + 1) * MLA_V, :] = _dot(wuvt_ref[h], ot[:, h * tq:(h + 1) * tq]).astype(BF16)


def mla_prompt(qt, kcat, latt, wuv_t, tk):
    nb, nq, _, rows = qt.shape
    tq = rows // MLA_HEADS
    t = nq * tq
    tk = min(tk, t)
    pairs =[(i, j) for i in range(nq) for j in range((i * tq + tq - 1) // tk + 1)]
    qi_tab = jnp.asarray([p[0] for p in pairs], jnp.int32)
    ki_tab = jnp.asarray([p[1] for p in pairs], jnp.int32)
    grid_spec = pltpu.PrefetchScalarGridSpec(
        num_scalar_prefetch=2,
        grid=(nb, len(pairs)),
        in_specs=[pl.BlockSpec((1, 1, MLA_QK, rows), lambda b, s, qi, ki: (b, qi[s], 0, 0)),
                  pl.BlockSpec((1, tk, MLA_QK), lambda b, s, qi, ki: (b, ki[s], 0)),
                  pl.BlockSpec((1, KV_RANK, tk), lambda b, s, qi, ki: (b, 0, ki[s])),
                  pl.BlockSpec((MLA_HEADS, MLA_V, KV_RANK), lambda b, s, qi, ki: (0, 0, 0))],
        out_specs=pl.BlockSpec((1, MLA_HEADS * MLA_V, tq), lambda b, s, qi, ki: (b, 0, qi[s])),
        scratch_shapes=[pltpu.VMEM((1, rows), F32), pltpu.VMEM((1, rows), F32), pltpu.VMEM((KV_RANK, rows), F32)],
    )
    return pl.pallas_call(
        functools.partial(_mla_prompt_kernel, tq=tq, tk=tk),
        grid_spec=grid_spec,
        out_shape=jax.ShapeDtypeStruct((nb, MLA_HEADS * MLA_V, t), BF16),
        compiler_params=_params(("parallel", "arbitrary")),
        name="mla_prompt",
    )(qi_tab, ki_tab, qt, kcat, latt, wuv_t)


def _mla_sample_kernel(pt_ref, q_ref, kn_ref, wuv_ref, lat_hbm, kr_hbm, o_ref, kl_buf, kp_buf, sem,
                       *, layer, n_pages, n_pg, n_grp, t_new):
    b = pl.program_id(0)
    rows = t_new * MLA_HEADS
    n_chunks = n_pages // n_pg
    per = n_pg // n_grp

    def page_copies(bb, c, slot):
        cps = []
        for i in range(n_pg):
            page = pt_ref[bb * n_pages + c * n_pg + i]
            cps.append(pltpu.make_async_copy(lat_hbm.at[layer, page], kl_buf.at[slot, i], sem.at[slot, 0]))
            cps.append(pltpu.make_async_copy(kr_hbm.at[layer, page], kp_buf.at[slot, i], sem.at[slot, 1]))
        return cps

    @pl.when(b == 0)
    def _():
        for cp in page_copies(0, 0, 0):
            cp.start()

    q = q_ref[0]
    ql = q[:, :KV_RANK]
    qr = q[:, KV_RANK:]
    m_g = [jnp.full((rows, 1), -jnp.inf, F32) for _ in range(n_grp)]
    l_g = [jnp.zeros((rows, 1), F32) for _ in range(n_grp)]
    acc_g = [jnp.zeros((rows, KV_RANK), F32) for _ in range(n_grp)]
    for c in range(n_chunks):
        slot = c % 2
        if c + 1 < n_chunks:
            for cp in page_copies(b, c + 1, 1 - slot):
                cp.start()
        else:
            @pl.when(b + 1 < pl.num_programs(0))
            def _():
                for cp in page_copies(b + 1, 0, 1 - slot):
                    cp.start()
        for cp in page_copies(b, c, slot):
            cp.wait()
        kls = [kl_buf[slot, i].astype(BF16) for i in range(n_pg)]
        ss = [_nt_dot(ql, kls[i]) + _dot(qr, kp_buf[slot, i].astype(BF16)) for i in range(n_pg)]
        alphas, ps = [], []
        for gi in range(n_grp):
            s = jnp.concatenate(ss[gi * per:(gi + 1) * per], axis=1)
            m_new = jnp.maximum(m_g[gi], jnp.max(s, axis=1, keepdims=True))
            alpha = jnp.exp2(m_g[gi] - m_new)
            p = jnp.exp2(s - m_new).astype(BF16)
            l_g[gi] = alpha * l_g[gi] + jnp.sum(p.astype(F32), axis=1, keepdims=True)
            m_g[gi] = m_new
            alphas.append(alpha)
            ps.append(p)
        for gi in range(n_grp):
            pv = _dot(ps[gi][:, :PAGE_SIZE], kls[gi * per])
            for i in range(1, per):
                pv = pv + _dot(ps[gi][:, i * PAGE_SIZE:(i + 1) * PAGE_SIZE], kls[gi * per + i])
            acc_g[gi] = alphas[gi] * acc_g[gi] + pv

    qf = q.astype(F32)
    kn = kn_ref[0]
    trow = lax.broadcasted_iota(jnp.int32, (rows, 1), 0) // MLA_HEADS
    cols = []
    for jj in range(t_new):
        sj = jnp.sum(qf * kn[jj:jj + 1, :], axis=1, keepdims=True)
        cols.append(jnp.where(trow >= jj, sj, -jnp.inf))
    m1 = m_g[0]
    for gi in range(1, n_grp):
        m1 = jnp.maximum(m1, m_g[gi])
    for sj in cols:
        m1 = jnp.maximum(m1, sj)
    l1 = jnp.zeros_like(m1)
    acc1 = jnp.zeros((rows, KV_RANK), F32)
    for gi in range(n_grp):
        ag = jnp.exp2(m_g[gi] - m1)
        l1 = l1 + ag * l_g[gi]
        acc1 = acc1 + ag * acc_g[gi]
    for jj, sj in enumerate(cols):
        pj = jnp.exp2(sj - m1)
        l1 = l1 + pj
        acc1 = acc1 + pj * kn[jj:jj + 1, :KV_RANK]
    o = (acc1 / l1).astype(BF16)
    proj = _dot(o, wuv_ref[...])
    rr = lax.broadcasted_iota(jnp.int32, proj.shape, 0)
    cc = lax.broadcasted_iota(jnp.int32, proj.shape, 1)
    proj = jnp.where(jnp.bitwise_and(rr, MLA_HEADS - 1) == cc // MLA_V, proj, 0.0)
    o_ref[0] = jnp.sum(proj.reshape(t_new, MLA_HEADS, MLA_HEADS * MLA_V), axis=1).astype(BF16)


MLA_SAMPLE_PAGES_PER_CHUNK = 32


def mla_sample(q_s, kn_s, wuv_all, cache_lat, cache_kr, layer, page_table):
    nb, rows, _ = q_s.shape
    t_new = rows // MLA_HEADS
    n_pages = page_table.shape[1]
    n_pg = min(MLA_SAMPLE_PAGES_PER_CHUNK, n_pages // 2)
    assert n_pages % (2 * n_pg) == 0
    n_grp = 2 if n_pg % 2 == 0 else 1
    grid_spec = pltpu.PrefetchScalarGridSpec(
        num_scalar_prefetch=1,
        grid=(nb,),
        in_specs=[pl.BlockSpec((1, rows, MLA_QK), lambda b, pt: (b, 0, 0)),
                  pl.BlockSpec((1, t_new, MLA_QK), lambda b, pt: (b, 0, 0)),
                  pl.BlockSpec((KV_RANK, MLA_HEADS * MLA_V), lambda b, pt: (0, 0)),
                  pl.BlockSpec(memory_space=pl.ANY),
                  pl.BlockSpec(memory_space=pl.ANY)],
        out_specs=pl.BlockSpec((1, t_new, MLA_HEADS * MLA_V), lambda b, pt: (b, 0, 0)),
        scratch_shapes=[pltpu.VMEM((2, n_pg, PAGE_SIZE, KV_RANK), F32),
                        pltpu.VMEM((2, n_pg, MLA_ROPE, PAGE_SIZE), F32),
                        pltpu.SemaphoreType.DMA((2, 2))],
    )
    return pl.pallas_call(
        functools.partial(_mla_sample_kernel, layer=layer, n_pages=n_pages, n_pg=n_pg, n_grp=n_grp, t_new=t_new),
        grid_spec=grid_spec,
        out_shape=jax.ShapeDtypeStruct((nb, t_new, MLA_HEADS * MLA_V), BF16),
        compiler_params=_params(("arbitrary",)),
        name="mla_sample",
    )(page_table.reshape(-1), q_s, kn_s, wuv_all, cache_lat, cache_kr)


def _rwprep_kernel(zr_ref, pv_ref, mu_ref, w0_ref, ww2_ref, a0_ref, wa2_ref, wg2_ref, kk_ref, ka_ref,
                   r_o, k_o, v_o, kk_o, kka_o, lw_o, g_o, *scratch, shift_in_kernel):
    zr = zr_ref[0]
    if shift_in_kernel:
        (carry,) = scratch

        @pl.when(pl.program_id(1) == 0)
        def _():
            carry[...] = pv_ref[0]

        first = lax.broadcasted_iota(jnp.int32, zr.shape, 0) == 0
        prev = jnp.where(first, carry[...], pltpu.roll(zr, 1, 0))
        carry[...] = zr[zr.shape[0] - 1:, :]
    else:
        prev = pv_ref[0]
    zs = zr + (prev - zr) * mu_ref[...]
    o3 = 3 * RW_W
    o4 = o3 + RW_DECAY_LORA
    o5 = o4 + RW_A_LORA
    xr, xk, xv = zs[:, :RW_W], zs[:, RW_W:2 * RW_W], zs[:, 2 * RW_W:o3]
    xw, xa, xg = zs[:, o3:o4], zs[:, o4:o5], zs[:, o5:]
    wl = w0_ref[...] + _dot(jnp.tanh(xw).astype(BF16), ww2_ref[...])
    w_log = -(jnp.maximum(-wl, 0.0) + jnp.log1p(jnp.exp(-jnp.abs(wl)))) - 0.5
    logw = -jnp.exp(w_log)
    a = jax.nn.sigmoid(a0_ref[...] + _dot(xa.astype(BF16), wa2_ref[...]))
    g = _dot(jax.nn.sigmoid(xg).astype(BF16), wg2_ref[...])
    kkf = xk * kk_ref[...]
    kf = xk * (1.0 + (a - 1.0) * ka_ref[...])
    for h in range(RW_HEADS):
        sl = slice(h * RW_N, (h + 1) * RW_N)
        kkh = kkf[:, sl]
        kkh = kkh / jnp.maximum(jnp.sqrt(jnp.sum(kkh * kkh, axis=1, keepdims=True)), 1e-12)
        r_o[0, h] = xr[:, sl]
        k_o[0, h] = kf[:, sl]
        v_o[0, h] = xv[:, sl]
        kk_o[0, h] = kkh
        kka_o[0, h] = kkh * a[:, sl]
        lw_o[0, h] = logw[:, sl]
        g_o[0, h] = g[:, sl]


def rwkv_prep(zr, prev, mu, w0, w_w2, a0, w_a2, w_g2, k_k, k_a, tm):
    nb, m, w = zr.shape
    shift_in_kernel = prev.shape[1] == 1 and m > 1
    vec = lambda n: pl.BlockSpec((1, n), lambda b, i: (0, 0))
    mat = lambda a: pl.BlockSpec(a.shape, lambda b, i: (0, 0))
    out_spec = pl.BlockSpec((1, RW_HEADS, tm, RW_N), lambda b, i: (b, 0, i, 0))
    out_sds = jax.ShapeDtypeStruct((nb, RW_HEADS, m, RW_N), F32)
    prev_spec = (pl.BlockSpec((1, 1, w), lambda b, i: (b, 0, 0)) if shift_in_kernel else
                 pl.BlockSpec((1, tm, w), lambda b, i: (b, i, 0)))
    return pl.pallas_call(
        functools.partial(_rwprep_kernel, shift_in_kernel=shift_in_kernel),
        grid=(nb, m // tm),
        in_specs=[pl.BlockSpec((1, tm, w), lambda b, i: (b, i, 0)),
                  prev_spec,
                  vec(w), vec(RW_W), mat(w_w2), vec(RW_W), mat(w_a2), mat(w_g2), vec(RW_W), vec(RW_W)],
        out_specs=[out_spec] * 7,
        out_shape=[out_sds] * 7,
        scratch_shapes=[pltpu.VMEM((1, w), F32)] if shift_in_kernel else [],
        compiler_params=_params(("parallel", "arbitrary")),
        name="rwkv_prep",
    )(zr, prev, mu.reshape(1, w), w0.reshape(1, RW_W), w_w2, a0.reshape(1, RW_W), w_a2, w_g2,
      k_k.reshape(1, RW_W), k_a.reshape(1, RW_W))


def _split_bf16(x, terms):
    parts = []
    rem = x
    for i in range(terms):
        p = rem.astype(BF16)
        parts.append(p)
        if i + 1 < terms:
            rem = rem - p.astype(F32)
    return parts


def _mm(a, b, ta, tb, dot=_dot):
    ap = _split_bf16(a, ta)
    bp = _split_bf16(b, tb)
    n = max(ta, tb)
    acc = None
    for i, x in enumerate(ap):
        for j, y in enumerate(bp):
            if i + j < n:
                d = dot(x, y)
                acc = d if acc is None else acc + d
    return acc


RW_P_CUMSUM = 2
RW_P_INTRA = 1
RW_P_INV = 1
RW_P_STATE = 1


def _rwkv_scan_kernel(r_ref, k_ref, v_ref, kk_ref, kka_ref, lw_ref, g_ref, rk_ref, lnw_ref, lnb_ref, s0_ref,
                      o_ref, st_ref, *, chunk):
    c = pl.program_id(1)

    @pl.when(c == 0)
    def _():
        st_ref[0] = s0_ref[0]

    c2 = 2 * chunk
    row = lax.broadcasted_iota(jnp.int32, (chunk, chunk), 0)
    col = lax.broadcasted_iota(jnp.int32, (chunk, chunk), 1)
    tri = jnp.where(col <= row, 1.0, 0.0).astype(BF16)
    eye_c = jnp.where(row == col, 1.0, 0.0).astype(F32)
    row2 = lax.broadcasted_iota(jnp.int32, (c2, c2), 0)
    col2 = jnp.bitwise_and(lax.broadcasted_iota(jnp.int32, (c2, c2), 1), chunk - 1)
    mask2 = col2 < jnp.where(row2 < chunk, row2, row2 - (chunk - 1))
    eye_n = lax.broadcasted_iota(jnp.int32, (RW_N, RW_N), 0) == lax.broadcasted_iota(jnp.int32, (RW_N, RW_N), 1)
    zeros_cn = jnp.zeros((chunk, RW_N), F32)
    n_double = int(math.log2(chunk)) - 1
    heads = range(RW_HEADS)
    cs = [_mm(tri, lw_ref[0, h], 1, RW_P_CUMSUM) for h in heads]
    lhs, rhs, g_end, g_end_col = [], [], [], []
    for h in heads:
        g_incl = jnp.exp(cs[h])
        g_prev = jnp.exp(cs[h] - lw_ref[0, h])
        g_inv = jnp.exp(-cs[h])
        cs_last = cs[h][chunk - 1:chunk, :]
        g_end.append(jnp.exp(cs_last))
        g_end_col.append(jnp.exp(jnp.sum(jnp.where(eye_n, jnp.broadcast_to(cs_last, (RW_N, RW_N)), 0.0),
                                         axis=1, keepdims=True)))
        lhs.append(jnp.concatenate([-kk_ref[0, h] * g_prev, r_ref[0, h] * g_incl], axis=0))
        rhs.append(jnp.concatenate([kka_ref[0, h] * g_inv, k_ref[0, h] * g_inv], axis=0))
    mx = [jnp.where(mask2, _mm(lhs[h], rhs[h], RW_P_INTRA, RW_P_INTRA, _nt_dot), 0.0) for h in heads]
    from_state = [_mm(lhs[h], st_ref[0, h], RW_P_STATE, RW_P_STATE) for h in heads]
    from_v = [_mm(mx[h], jnp.concatenate([zeros_cn, v_ref[0, h]], axis=0), RW_P_INTRA, RW_P_INTRA)
              for h in heads]
    l_ab = [mx[h][:chunk, :chunk] for h in heads]
    tinv = [eye_c + l_ab[h] for h in heads]
    pw = [_mm(l_ab[h], l_ab[h], RW_P_INV, RW_P_INV) for h in heads]
    for _ in range(n_double - 1):
        both = [_mm(jnp.concatenate([pw[h], tinv[h]], axis=0), pw[h], RW_P_INV, RW_P_INV) for h in heads]
        tinv = [tinv[h] + both[h][chunk:] for h in heads]
        pw = [both[h][:chunk] for h in heads]
    tinv = [tinv[h] + _mm(tinv[h], pw[h], RW_P_INV, RW_P_INV) for h in heads]
    u = [_mm(tinv[h], from_state[h][:chunk] + from_v[h][:chunk], RW_P_INV, RW_P_INV) for h in heads]
    y_u = [_mm(mx[h][chunk:, :chunk], u[h], RW_P_INTRA, RW_P_INTRA) for h in heads]
    st_add = [_mm(rhs[h] * g_end[h], jnp.concatenate([u[h], v_ref[0, h]], axis=0), RW_P_STATE, RW_P_STATE, _tn_dot)
              for h in heads]
    outs = []
    for h in heads:
        st_ref[0, h] = st_ref[0, h] * g_end_col[h] + st_add[h]
        y = from_state[h][chunk:] + from_v[h][chunk:] + y_u[h]
        mean = jnp.mean(y, axis=1, keepdims=True)
        yc = y - mean
        var = jnp.mean(yc * yc, axis=1, keepdims=True)
        yn = yc * lax.rsqrt(var + RW_GN_EPS) * lnw_ref[h:h + 1, :] + lnb_ref[h:h + 1, :]
        bonus = jnp.sum(r_ref[0, h] * k_ref[0, h] * rk_ref[h:h + 1, :], axis=1, keepdims=True) * v_ref[0, h]
        outs.append((yn + bonus) * g_ref[0, h])
    o_ref[0] = jnp.concatenate(outs, axis=1).astype(BF16)


def rwkv_scan(r, k, v, kk, kka, lw, g, r_k, ln_w, ln_b, s0_t, chunk):
    nb, _, t, _ = r.shape
    tspec = pl.BlockSpec((1, RW_HEADS, chunk, RW_N), lambda b, c: (b, 0, c, 0))
    hspec = pl.BlockSpec((RW_HEADS, RW_N), lambda b, c: (0, 0))
    sspec = pl.BlockSpec((1, RW_HEADS, RW_N, RW_N), lambda b, c: (b, 0, 0, 0))
    return pl.pallas_call(
        functools.partial(_rwkv_scan_kernel, chunk=chunk),
        grid=(nb, t // chunk),
        in_specs=[tspec] * 7 + [hspec] * 3 + [sspec],
        out_specs=[pl.BlockSpec((1, chunk, RW_W), lambda b, c: (b, c, 0)), sspec],
        out_shape=[jax.ShapeDtypeStruct((nb, t, RW_W), BF16),
                   jax.ShapeDtypeStruct((nb, RW_HEADS, RW_N, RW_N), F32)],
        compiler_params=_params(("parallel", "arbitrary")),
        name="rwkv_scan",
    )(r, k, v, kk, kka, lw, g, r_k, ln_w.reshape(RW_HEADS, RW_N), ln_b.reshape(RW_HEADS, RW_N), s0_t)


def _rwkv_step_kernel(r_ref, k_ref, v_ref, kk_ref, kka_ref, lw_ref, g_ref, rk_ref, lnw_ref, lnb_ref, s0_ref,
                      o_ref, s_ref, w_scr, y_scr, *, t_new):
    for t in range(t_new):
        w_scr[t] = jnp.exp(lw_ref[t, 0])

    def value_row(vi, carry):
        s = s0_ref[0, vi]
        for t in range(t_new):
            sa = -jnp.sum(s * kk_ref[t, 0], axis=0, keepdims=True)
            s = s * w_scr[t] + sa * kka_ref[t, 0] + v_ref[t, 0, pl.ds(vi, 1), :] * k_ref[t, 0]
            y_scr[t, pl.ds(vi, 1), :] = jnp.sum(s * r_ref[t, 0], axis=0, keepdims=True)
        s_ref[0, vi] = s
        return carry

    lax.fori_loop(0, RW_N, value_row, 0)
    for t in range(t_new):
        y = y_scr[t]
        mean = jnp.mean(y, axis=0, keepdims=True)
        yc = y - mean
        var = jnp.mean(yc * yc, axis=0, keepdims=True)
        yn = yc * lax.rsqrt(var + RW_GN_EPS) * lnw_ref[0] + lnb_ref[0]
        bonus = jnp.sum(r_ref[t, 0] * k_ref[t, 0] * rk_ref[0], axis=0, keepdims=True) * v_ref[t, 0]
        o_ref[t, 0] = (yn + bonus) * g_ref[t, 0]


def rwkv_step(r, k, v, kk, kka, lw, g, r_k, ln_w, ln_b, s0):
    t_new, nh, n, nb = r.shape
    tspec = pl.BlockSpec((t_new, 1, n, nb), lambda h: (0, h, 0, 0))
    hspec = pl.BlockSpec((1, n, nb), lambda h: (h, 0, 0))
    sspec = pl.BlockSpec((1, n, n, nb), lambda h: (h, 0, 0, 0))
    return pl.pallas_call(
        functools.partial(_rwkv_step_kernel, t_new=t_new),
        grid=(nh,),
        in_specs=[tspec] * 7 + [hspec] * 3 + [sspec],
        out_specs=[tspec, sspec],
        out_shape=[jax.ShapeDtypeStruct((t_new, nh, n, nb), F32), jax.ShapeDtypeStruct((nh, n, n, nb), F32)],
        scratch_shapes=[pltpu.VMEM((t_new, n, nb), F32), pltpu.VMEM((t_new, n, nb), F32)],
        compiler_params=_params(("parallel",)),
        name="rwkv_step",
    )(r, k, v, kk, kka, lw, g, r_k, ln_w, ln_b, s0)


def _retention_kernel(lg_ref, qk_ref, v_ref, g_ref, cos_ref, sin_ref, s0_ref, o_ref, s_ref, *, lb, l_true):
    c = pl.program_id(1)

    @pl.when(c == 0)
    def _():
        s_ref[0] = s0_ref[0]

    lp = max(lb, 16)
    cos = cos_ref[0]
    sin = sin_ref[0]
    half = RET_DK // 2
    qk_w = RET_HEADS * RET_DK

    def rope(x):
        x1, x2 = x[:, :half], x[:, half:]
        return jnp.concatenate([x1 * cos - x2 * sin, x1 * sin + x2 * cos], axis=1)

    def rows(x):
        if lp == lb:
            return x
        return jnp.concatenate([x, jnp.zeros((lp - lb, x.shape[1]), x.dtype)], axis=0)

    row = lax.broadcasted_iota(jnp.int32, (lp, lp), 0)
    col = lax.broadcasted_iota(jnp.int32, (lp, lp), 1)
    diff = (row - col).astype(F32)
    idx = lax.broadcasted_iota(jnp.int32, (lp, 1), 0).astype(F32)
    heads = range(RET_HEADS)
    qm, km, kdm, vm, dmask, row_dec = [], [], [], [], [], []
    for h in heads:
        lg = lg_ref[h]
        q = rows(rope(qk_ref[0, :, h * RET_DK:(h + 1) * RET_DK]))
        k = rows(rope(qk_ref[0, :, qk_w + h * RET_DK:qk_w + (h + 1) * RET_DK]) * (RET_DK ** -0.5))
        qm.append(q.astype(BF16))
        km.append(k.astype(BF16))
        kdm.append((k * jnp.exp((l_true - 1.0 - idx) * lg)).astype(BF16))
        vm.append(rows(v_ref[0, :, h * RET_DV:(h + 1) * RET_DV]).astype(BF16))
        dmask.append(jnp.where(diff >= 0, jnp.exp(jnp.maximum(diff, 0.0) * lg), 0.0))
        row_dec.append(jnp.exp((idx + 1.0) * lg))
    sc = [(_nt_dot(qm[h], km[h]) * dmask[h]).astype(BF16) for h in heads]
    cross = [_dot(qm[h], s_ref[0, h].astype(BF16)) * row_dec[h] for h in heads]
    s_add = [_tn_dot(kdm[h], vm[h]) for h in heads]
    inner = [_dot(sc[h], vm[h]) for h in heads]
    outs = []
    for h in heads:
        s_dec = jnp.exp(jnp.zeros((1, RET_DV), F32) + l_true * lg_ref[h])
        s_ref[0, h] = s_ref[0, h] * s_dec + s_add[h]
        o = (inner[h] + cross[h])[:lb]
        o = o * lax.rsqrt(jnp.mean(o * o, axis=1, keepdims=True) + NORM_EPS)
        gv = g_ref[0, :, h * RET_DV:(h + 1) * RET_DV]
        outs.append(o * (gv * jax.nn.sigmoid(gv)))
    o_ref[0] = jnp.concatenate(outs, axis=1).astype(BF16)


def retention(z, cos, sin, lg, s0, lb, l_true):
    nb, m, _ = z.shape
    vw = RET_HEADS * RET_DV
    assert 2 * RET_HEADS * RET_DK == vw
    sspec = pl.BlockSpec((1, RET_HEADS, RET_DK, RET_DV), lambda b, c: (b, 0, 0, 0))
    return pl.pallas_call(
        functools.partial(_retention_kernel, lb=lb, l_true=float(l_true)),
        grid=(nb, m // lb),
        in_specs=[pl.BlockSpec(memory_space=pltpu.SMEM),
                  pl.BlockSpec((1, lb, vw), lambda b, c: (b, c, 0)),
                  pl.BlockSpec((1, lb, vw), lambda b, c: (b, c, 1)),
                  pl.BlockSpec((1, lb, vw), lambda b, c: (b, c, 2)),
                  pl.BlockSpec((1, lb, RET_DK // 2), lambda b, c: (0, c, 0)),
                  pl.BlockSpec((1, lb, RET_DK // 2), lambda b, c: (0, c, 0)),
                  sspec],
        out_specs=[pl.BlockSpec((1, lb, vw), lambda b, c: (b, c, 0)), sspec],
        out_shape=[jax.ShapeDtypeStruct((nb, m, vw), BF16),
                   jax.ShapeDtypeStruct((nb, RET_HEADS, RET_DK, RET_DV), F32)],
        compiler_params=_params(("parallel", "arbitrary")),
        name="retention",
    )(lg, z, z, z, cos, sin, s0)


def _rope_tables(pos, half):
    inv = ROPE_BASE ** (-jnp.arange(half, dtype=F32) / half)
    ang = pos.astype(F32)[:, None] * inv[None, :]
    return jnp.cos(ang), jnp.sin(ang)


def _mla_tables(pos):
    cos, sin = _rope_tables(pos, MLA_ROPE // 2)
    return jnp.concatenate([cos, cos], axis=1), jnp.concatenate([-sin, sin], axis=1)


def _even_layer(x, mods, pos_tabs, prm, past, tm):
    (w_in_p, g_mix, g_kv, wuk_t, wuk_r, wuv_t, wuv_all, mu, w0, w_w2, a0, w_a2, w_g2, k_k, k_a, r_k, ln_w, ln_b,
     w_out_mla, w_out_rw) = prm
    sh1, sc1, gt1 = mods
    cf, sf = pos_tabs
    nb, m, _ = x.shape
    zr, zq, zkv = norm_mod_matmul_split(x, g_mix, sh1, sc1, w_in_p, (RW_SHIFT_W, MLA_HEADS * (MLA_NOPE + MLA_ROPE), MLA_QK), tm)
    lat, kr, kcat, latt = kv_prep(zkv, g_kv, cf, sf, tm)
    if past is None:
        qt = q_prep_t(zq, wuk_r, cf, sf, min(256, m))
        mla_out = mla_prompt(qt, kcat, latt, wuv_t, 512)
        mla_transposed = True
        s0_t = jnp.zeros((nb, RW_HEADS, RW_N, RW_N), F32)
        tens = rwkv_prep(zr, jnp.zeros((nb, 1, RW_SHIFT_W), F32), mu, w0, w_w2, a0, w_a2, w_g2, k_k, k_a, min(tm, 256))
        rw_out, s_t = rwkv_scan(*tens, r_k, ln_w, ln_b, s0_t, RW_CHUNK)
        s_new = jnp.swapaxes(s_t, -1, -2)
        shift_new = zr[:, -1]
    else:
        cache_lat, cache_kr, layer, page_table, s0, shift_prev, t_new = past
        nbs = m // t_new
        mla_transposed = False
        qcat = q_prep(zq, wuk_t, cf, sf, tm)
        q_s = qcat.reshape(MLA_HEADS, nbs, t_new, MLA_QK).transpose(1, 2, 0, 3).reshape(nbs, t_new * MLA_HEADS, MLA_QK)
        kn_s = jnp.concatenate([lat, kr], axis=-1).reshape(nbs, t_new, MLA_QK)
        mla_out = mla_sample(q_s, kn_s, wuv_all, cache_lat, cache_kr, layer, page_table).reshape(1, m, MLA_HEADS * MLA_V)
        zr_b = zr.reshape(nbs, t_new, RW_SHIFT_W)
        prev = jnp.concatenate([shift_prev[:, None, :], zr_b[:, :-1]], axis=1).reshape(1, m, RW_SHIFT_W)
        tens = rwkv_prep(zr, prev, mu, w0, w_w2, a0, w_a2, w_g2, k_k, k_a, min(tm, 256))
        tens = [u.reshape(RW_HEADS, nbs, t_new, RW_N).transpose(2, 0, 3, 1) for u in tens]
        lanes = lambda p: jnp.broadcast_to(p.reshape(RW_HEADS, RW_N, 1), (RW_HEADS, RW_N, nbs))
        rw_l, s_l = rwkv_step(*tens, lanes(r_k), lanes(ln_w), lanes(ln_b), jnp.transpose(s0, (1, 2, 3, 0)))
        rw_out = rw_l.transpose(3, 0, 1, 2).reshape(1, m, RW_W).astype(BF16)
        s_new = jnp.transpose(s_l, (3, 0, 1, 2))
        shift_new = zr_b[:, -1]
    x_new = matmul_gate_res([mla_out, rw_out], [w_out_mla, w_out_rw], x, gt1, tm, (mla_transposed, False))
    return x_new, (lat, kr, s_new, shift_new)


def _odd_layer(x, mods, ret_tabs, prm, s0, t_new, tm):
    w_in, g_mix, w_out, lg = prm
    sh1, sc1, gt1 = mods
    cos, sin = ret_tabs
    nb, m, _ = x.shape
    z = norm_mod_matmul(x, g_mix, sh1, sc1, w_in, min(2 * tm, m), 2048)
    if s0 is None:
        s0 = jnp.zeros((nb, RET_HEADS, RET_DK, RET_DV), F32)
        lb = min(RET_BLOCK, m)
        o, s_new = retention(z, cos, sin, lg, s0, lb, lb)
    else:
        nbs = m // t_new
        lpad = 8
        z_b = jnp.pad(z.reshape(nbs, t_new, -1), ((0, 0), (0, lpad - t_new), (0, 0)))
        o, s_new = retention(z_b, cos, sin, lg, s0, lpad, t_new)
        o = o[:, :t_new].reshape(1, m, RET_HEADS * RET_DV)
    x_new = matmul_gate_res([o], [w_out], x, gt1, tm)
    return x_new, s_new


def kernel(x_prompt, x_sample, c_prompt, c_sample, cache_kv_latent, cache_k_rope, page_table, state_rwkv, state_rwkv_shift, state_ret, w_ada, b_ada, g_norm_mix, g_norm_mlp, g_final, w_in_even, g_kv, w_uk, w_uv, rw_mu, rw_w0, rw_w2, rw_a0, rw_a2, rw_g2, rw_k_k, rw_k_a, rw_r_k, rw_ln_w, rw_ln_b, w_out_even, w_in_odd, w_out_odd, w_ff1, w_ff2):
    nbp, t_p, d = x_prompt.shape
    nbs, t_s, _ = x_sample.shape
    depth = w_ada.shape[0]
    past_len = page_table.shape[1] * PAGE_SIZE
    m_s = nbs * t_s
    tm_p = min(512, t_p)
    tm_s = m_s

    c_all = jnp.concatenate([jnp.repeat(c_sample, t_s, axis=0), c_prompt], axis=0)
    c_all = jnp.pad(c_all, ((0, -c_all.shape[0] % 16), (0, 0)))
    mods_all = ada_proj(c_all, w_ada, b_ada)

    def group_mods(l):
        mp = mods_all[l, m_s:m_s + nbp].reshape(nbp, 1, 6, d)
        return [mp[:, :, i] for i in range(6)], [(mods_all, l, i) for i in range(6)]

    pos_p = jnp.arange(t_p)
    pos_s = past_len + jnp.arange(t_s)
    cf_p, sf_p = _mla_tables(pos_p)
    cf_s, sf_s = _mla_tables(pos_s)
    mla_tabs_p = (cf_p[None], sf_p[None])
    mla_tabs_s = (jnp.tile(cf_s, (nbs, 1))[None], jnp.tile(sf_s, (nbs, 1))[None])
    cr_p, sr_p = _rope_tables(pos_p, RET_DK // 2)
    cr_s, sr_s = _rope_tables(pos_s, RET_DK // 2)
    ret_tabs_p = (cr_p[None], sr_p[None])
    ret_tabs_s = (jnp.pad(cr_s, ((0, 8 - t_s), (0, 0)))[None], jnp.pad(sr_s, ((0, 8 - t_s), (0, 0)))[None])
    lg = jnp.log(1 - 2.0 ** (-5.0 - jnp.arange(RET_HEADS, dtype=F32)))

    xp = x_prompt
    xs = x_sample.reshape(1, m_s, d)
    lat_p, kr_p, rw_p, sh_p, ret_p = [], [], [], [], []
    lat_s, kr_s, rw_s, sh_s, ret_s = [], [], [], [], []
    q_w = MLA_HEADS * (MLA_NOPE + MLA_ROPE)
    for l in range(depth):
        (sh1p, sc1p, gt1p, sh2p, sc2p, gt2p), (sh1s, sc1s, gt1s, sh2s, sc2s, gt2s) = group_mods(l)
        i = l // 2
        if l % 2 == 0:
            w_in = w_in_even[i]
            wq = w_in[:, :q_w].reshape(d, MLA_HEADS, MLA_NOPE + MLA_ROPE)
            w_in_p = jnp.concatenate([w_in[:, q_w + MLA_QK:],
                                      wq[:, :, :MLA_NOPE].reshape(d, -1), wq[:, :, MLA_NOPE:].reshape(d, -1),
                                      w_in[:, q_w:q_w + MLA_QK]], axis=1).astype(BF16)
            wuv = w_uv[i]
            mla_w = MLA_HEADS * MLA_V
            prm = (w_in_p, g_norm_mix[l], g_kv[i], jnp.transpose(w_uk[i], (1, 2, 0)).astype(BF16),
                   jnp.transpose(w_uk[i], (1, 0, 2)).astype(BF16), jnp.transpose(wuv, (1, 2, 0)).astype(BF16), wuv.reshape(KV_RANK, mla_w).astype(BF16),
                   rw_mu[i], rw_w0[i], rw_w2[i].astype(BF16), rw_a0[i], rw_a2[i].astype(BF16), rw_g2[i].astype(BF16),
                   rw_k_k[i], rw_k_a[i], rw_r_k[i], rw_ln_w[i], rw_ln_b[i],
                   w_out_even[i, :mla_w].astype(BF16), w_out_even[i, mla_w:].astype(BF16))
            xp, (la, kr, st, sh) = _even_layer(xp, (sh1p, sc1p, gt1p), mla_tabs_p, prm, None, tm_p)
            lat_p.append(la); kr_p.append(kr); rw_p.append(st); sh_p.append(sh)
            cache_kr_t = jnp.swapaxes(cache_k_rope, 2, 3)
            past = (cache_kv_latent, cache_kr_t, i, page_table, state_rwkv[i], state_rwkv_shift[i], t_s)
            xs, (la, kr, st, sh) = _even_layer(xs, (sh1s, sc1s, gt1s), mla_tabs_s, prm, past, tm_s)
            lat_s.append(la.reshape(nbs, t_s, KV_RANK)); kr_s.append(kr.reshape(nbs, t_s, MLA_ROPE))
            rw_s.append(st); sh_s.append(sh)
        else:
            prm = (w_in_odd[i].astype(BF16), g_norm_mix[l], w_out_odd[i].astype(BF16), lg)
            xp, st = _odd_layer(xp, (sh1p, sc1p, gt1p), ret_tabs_p, prm, None, t_s, tm_p)
            ret_p.append(st)
            xs, st = _odd_layer(xs, (sh1s, sc1s, gt1s), ret_tabs_s, prm, state_ret[i], t_s, tm_s)
            ret_s.append(st)
        final = l == depth - 1
        w1 = w_ff1[l].astype(BF16)
        w2 = w_ff2[l].astype(BF16)
        xp = mlp_block(xp, g_norm_mlp[l], sh2p, sc2p, gt2p, w1, w2, g_final, final, min(2 * tm_p, t_p), 1024)
        xs = mlp_block(xs, g_norm_mlp[l], sh2s, sc2s, gt2s, w1, w2, g_final, final, tm_s, 1024)
    return (xp, xs.reshape(nbs, t_s, d),
            jnp.stack(lat_p), jnp.stack(kr_p), jnp.stack(rw_p), jnp.stack(sh_p), jnp.stack(ret_p),
            jnp.stack(lat_s), jnp.stack(kr_s), jnp.stack(rw_s), jnp.stack(sh_s), jnp.stack(ret_s))
```

```python
import functools
import math

import jax
import jax.numpy as jnp
from jax import lax
from jax.experimental import pallas as pl
from jax.experimental.pallas import tpu as pltpu

F32 = jnp.float32
BF16 = jnp.bfloat16
HIGHEST = lax.Precision.HIGHEST

D_MODEL = 1024
PAGE_SIZE = 128
MLA_HEADS = 8
MLA_NOPE = 64
MLA_ROPE = 32
MLA_V = 64
KV_RANK = 256
MLA_QK = KV_RANK + MLA_ROPE
MLA_SCALE = (MLA_NOPE + MLA_ROPE) ** -0.5
MLA_QSCALE = MLA_SCALE * math.log2(math.e)
RW_HEADS = 8
RW_N = 64
RW_W = RW_HEADS * RW_N
RW_DECAY_LORA = 64
RW_A_LORA = 64
RW_G_LORA = 128
RW_SHIFT_W = 3 * RW_W + RW_DECAY_LORA + RW_A_LORA + RW_G_LORA
RW_GN_EPS = 64e-5
RW_CHUNK = 64
RET_HEADS = 4
RET_DK = 256
RET_DV = 512
RET_BLOCK = 256
D_FF = 4 * D_MODEL
ROPE_BASE = 10000.0
NORM_EPS = 1e-6
MIB = 1024 * 1024


def _params(sem, vmem_mib=48):
    return pltpu.CompilerParams(dimension_semantics=sem, vmem_limit_bytes=vmem_mib * MIB)


def _rms(x, g):
    return x * lax.rsqrt(jnp.mean(x * x, axis=-1, keepdims=True) + NORM_EPS) * g


def _nt_dot(a, b, precision=None):
    return lax.dot_general(a, b, (((1,), (1,)), ((), ())), precision=precision, preferred_element_type=F32)


def _tn_dot(a, b, precision=None):
    return lax.dot_general(a, b, (((0,), (0,)), ((), ())), precision=precision, preferred_element_type=F32)


def _dot(a, b, precision=None):
    return jnp.dot(a, b, precision=precision, preferred_element_type=F32)


def _mod_arg(mod):
    return mod[0] if isinstance(mod, tuple) else mod


def _mod_spec(mod, tm, nmid):
    if isinstance(mod, tuple):
        _, layer, which = mod
        if nmid == 2:
            return pl.BlockSpec((1, tm, D_MODEL), lambda b, m, j: (layer, m, which))
        return pl.BlockSpec((1, tm, D_MODEL), lambda b, m: (layer, m, which))
    if mod.shape[1] == 1:
        if nmid == 2:
            return pl.BlockSpec((1, 1, mod.shape[2]), lambda b, m, j: (b, 0, 0))
        return pl.BlockSpec((1, 1, mod.shape[2]), lambda b, m: (b, 0, 0))
    if nmid == 2:
        return pl.BlockSpec((1, tm, mod.shape[2]), lambda b, m, j: (b, m, 0))
    return pl.BlockSpec((1, tm, mod.shape[2]), lambda b, m: (b, m, 0))


def _ada_kernel(c_ref, w_ref, b_ref, o_ref):
    o_ref[0] = _dot(c_ref[...].astype(BF16), w_ref[0].astype(BF16)) + b_ref[0]


def ada_proj(c, w_ada, b_ada):
    nl, d, n = w_ada.shape
    r = c.shape[0]
    tn = 1536
    return pl.pallas_call(
        _ada_kernel,
        grid=(nl, n // tn),
        in_specs=[pl.BlockSpec((r, d), lambda l, j: (0, 0)),
                  pl.BlockSpec((1, d, tn), lambda l, j: (l, 0, j)),
                  pl.BlockSpec((1, 1, tn), lambda l, j: (l, 0, j))],
        out_specs=pl.BlockSpec((1, r, tn), lambda l, j: (l, 0, j)),
        out_shape=jax.ShapeDtypeStruct((nl, r, n), F32),
        compiler_params=_params(("parallel", "parallel")),
        name="ada_proj",
    )(c, w_ada, b_ada.reshape(nl, 1, n))


def _nmm_split_kernel(x_ref, g_ref, sh_ref, sc_ref, w_ref, *o_refs, splits):
    h = (_rms(x_ref[0], g_ref[...]) * (1.0 + sc_ref[0]) + sh_ref[0]).astype(BF16)
    off = 0
    for o_ref, n in zip(o_refs, splits):
        o_ref[0] = _dot(h, w_ref[:, off:off + n])
        off += n


def norm_mod_matmul_split(x, g, shift, scale, w, splits, tm):
    nb, m, d = x.shape
    n = w.shape[1]
    return pl.pallas_call(
        functools.partial(_nmm_split_kernel, splits=splits),
        grid=(nb, m // tm),
        in_specs=[pl.BlockSpec((1, tm, d), lambda b, i: (b, i, 0)),
                  pl.BlockSpec((1, d), lambda b, i: (0, 0)),
                  _mod_spec(shift, tm, 1), _mod_spec(scale, tm, 1),
                  pl.BlockSpec((d, n), lambda b, i: (0, 0))],
        out_specs=[pl.BlockSpec((1, tm, s), lambda b, i: (b, i, 0)) for s in splits],
        out_shape=[jax.ShapeDtypeStruct((nb, m, s), F32) for s in splits],
        compiler_params=_params(("parallel", "parallel")),
        name="norm_mod_matmul_split",
    )(x, g.reshape(1, d), _mod_arg(shift), _mod_arg(scale), w)


def _nmm_kernel(x_ref, g_ref, sh_ref, sc_ref, w_ref, o_ref):
    h = (_rms(x_ref[0], g_ref[...]) * (1.0 + sc_ref[0]) + sh_ref[0]).astype(BF16)
    o_ref[0] = _dot(h, w_ref[...])


def norm_mod_matmul(x, g, shift, scale, w, tm, tn):
    nb, m, d = x.shape
    n = w.shape[1]
    return pl.pallas_call(
        _nmm_kernel,
        grid=(nb, m // tm, n // tn),
        in_specs=[pl.BlockSpec((1, tm, d), lambda b, i, j: (b, i, 0)),
                  pl.BlockSpec((1, d), lambda b, i, j: (0, 0)),
                  _mod_spec(shift, tm, 2), _mod_spec(scale, tm, 2),
                  pl.BlockSpec((d, tn), lambda b, i, j: (0, j))],
        out_specs=pl.BlockSpec((1, tm, tn), lambda b, i, j: (b, i, j)),
        out_shape=jax.ShapeDtypeStruct((nb, m, n), F32),
        compiler_params=_params(("parallel", "parallel", "arbitrary")),
        name="norm_mod_matmul",
    )(x, g.reshape(1, d), _mod_arg(shift), _mod_arg(scale), w)


def _mgr_kernel(*refs, n_pairs, transposed):
    a_refs = refs[:n_pairs]
    w_refs = refs[n_pairs:2 * n_pairs]
    res_ref, gt_ref, o_ref = refs[2 * n_pairs:]
    acc = None
    for a_ref, w_ref, tr in zip(a_refs, w_refs, transposed):
        d = (_tn_dot if tr else _dot)(a_ref[0], w_ref[...])
        acc = d if acc is None else acc + d
    o_ref[0] = res_ref[0] + gt_ref[0] * acc


def matmul_gate_res(a_list, w_list, res, gate, tm, transposed=None):
    nb, m, d = res.shape
    n_pairs = len(a_list)
    transposed = tuple(transposed or (False,) * n_pairs)
    in_specs = [pl.BlockSpec((1, a.shape[1], tm), lambda b, i: (b, 0, i)) if tr else
                pl.BlockSpec((1, tm, a.shape[2]), lambda b, i: (b, i, 0)) for a, tr in zip(a_list, transposed)]
    in_specs += [pl.BlockSpec(w.shape, lambda b, i: (0, 0)) for w in w_list]
    in_specs += [pl.BlockSpec((1, tm, d), lambda b, i: (b, i, 0)), _mod_spec(gate, tm, 1)]
    return pl.pallas_call(
        functools.partial(_mgr_kernel, n_pairs=n_pairs, transposed=transposed),
        grid=(nb, m // tm),
        in_specs=in_specs,
        out_specs=pl.BlockSpec((1, tm, d), lambda b, i: (b, i, 0)),
        out_shape=jax.ShapeDtypeStruct((nb, m, d), F32),
        compiler_params=_params(("parallel", "parallel")),
        name="matmul_gate_res",
    )(*a_list, *w_list, res, _mod_arg(gate))


def _mlp_kernel(x_ref, g_ref, sh_ref, sc_ref, gt_ref, w1_ref, w2_ref, gf_ref, o_ref, h_scr, acc_scr, *, final):
    f = pl.program_id(2)

    @pl.when(f == 0)
    def _():
        h_scr[...] = (_rms(x_ref[0], g_ref[...]) * (1.0 + sc_ref[0]) + sh_ref[0]).astype(BF16)
        acc_scr[...] = jnp.zeros_like(acc_scr)

    a = _dot(h_scr[...], w1_ref[...])
    a = jnp.square(jnp.maximum(a, 0.0)).astype(BF16)
    acc_scr[...] += _dot(a, w2_ref[...])

    @pl.when(f == pl.num_programs(2) - 1)
    def _():
        y = x_ref[0] + gt_ref[0] * acc_scr[...]
        if final:
            y = _rms(y, gf_ref[...])
        o_ref[0] = y


def mlp_block(x, g, shift, scale, gate, w1, w2, g_final, final, tm, tf):
    nb, m, d = x.shape
    dff = w1.shape[1]
    return pl.pallas_call(
        functools.partial(_mlp_kernel, final=final),
        grid=(nb, m // tm, dff // tf),
        in_specs=[pl.BlockSpec((1, tm, d), lambda b, i, f: (b, i, 0)),
                  pl.BlockSpec((1, d), lambda b, i, f: (0, 0)),
                  _mod_spec(shift, tm, 2), _mod_spec(scale, tm, 2), _mod_spec(gate, tm, 2),
                  pl.BlockSpec((d, tf), lambda b, i, f: (0, f)),
                  pl.BlockSpec((tf, d), lambda b, i, f: (f, 0)),
                  pl.BlockSpec((1, d), lambda b, i, f: (0, 0))],
        out_specs=pl.BlockSpec((1, tm, d), lambda b, i, f: (b, i, 0)),
        out_shape=jax.ShapeDtypeStruct((nb, m, d), F32),
        scratch_shapes=[pltpu.VMEM((tm, d), BF16), pltpu.VMEM((tm, d), F32)],
        compiler_params=_params(("parallel", "parallel", "arbitrary"), 56),
        name="mlp_block",
    )(x, g.reshape(1, d), _mod_arg(shift), _mod_arg(scale), _mod_arg(gate), w1, w2, g_final.reshape(1, d))


def _rope32(x, cf, sf):
    half = MLA_ROPE // 2
    sw = jnp.concatenate([x[:, half:], x[:, :half]], axis=1)
    return x * cf + sw * sf


def _eye_bf16(n):
    return jnp.where(lax.broadcasted_iota(jnp.int32, (n, n), 0) == lax.broadcasted_iota(jnp.int32, (n, n), 1),
                     1.0, 0.0).astype(BF16)


def _kvprep_kernel(zkv_ref, g_ref, cf_ref, sf_ref, lat_ref, kr_ref, kcat_ref, latt_ref):
    z = zkv_ref[0]
    lat = _rms(z[:, :KV_RANK], g_ref[...])
    kr = _rope32(z[:, KV_RANK:], cf_ref[0], sf_ref[0])
    lat_ref[0] = lat
    kr_ref[0] = kr
    lat_b = lat.astype(BF16)
    kcat_ref[0, :, :KV_RANK] = lat_b
    kcat_ref[0, :, KV_RANK:] = kr.astype(BF16)
    latt_ref[0] = _nt_dot(_eye_bf16(KV_RANK), lat_b).astype(BF16)


def kv_prep(zkv, g_kv, cf, sf, tm):
    nb, m, _ = zkv.shape
    return pl.pallas_call(
        _kvprep_kernel,
        grid=(nb, m // tm),
        in_specs=[pl.BlockSpec((1, tm, MLA_QK), lambda b, i: (b, i, 0)),
                  pl.BlockSpec((1, KV_RANK), lambda b, i: (0, 0)),
                  pl.BlockSpec((1, tm, MLA_ROPE), lambda b, i: (0, i, 0)),
                  pl.BlockSpec((1, tm, MLA_ROPE), lambda b, i: (0, i, 0))],
        out_specs=[pl.BlockSpec((1, tm, KV_RANK), lambda b, i: (b, i, 0)),
                   pl.BlockSpec((1, tm, MLA_ROPE), lambda b, i: (b, i, 0)),
                   pl.BlockSpec((1, tm, MLA_QK), lambda b, i: (b, i, 0)),
                   pl.BlockSpec((1, KV_RANK, tm), lambda b, i: (b, 0, i))],
        out_shape=[jax.ShapeDtypeStruct((nb, m, KV_RANK), F32),
                   jax.ShapeDtypeStruct((nb, m, MLA_ROPE), F32),
                   jax.ShapeDtypeStruct((nb, m, MLA_QK), BF16),
                   jax.ShapeDtypeStruct((nb, KV_RANK, m), BF16)],
        compiler_params=_params(("parallel", "parallel")),
        name="kv_prep",
    )(zkv, g_kv.reshape(1, KV_RANK), cf, sf)


def _qprep_kernel(zq_ref, wuk_ref, cf_ref, sf_ref, o_ref):
    z = zq_ref[0]
    cf = cf_ref[0]
    sf = sf_ref[0]
    nope_w = MLA_HEADS * MLA_NOPE
    for h in range(MLA_HEADS):
        qn = z[:, h * MLA_NOPE:(h + 1) * MLA_NOPE].astype(BF16)
        ql = _dot(qn, wuk_ref[h]) * MLA_QSCALE
        qr = _rope32(z[:, nope_w + h * MLA_ROPE:nope_w + (h + 1) * MLA_ROPE], cf, sf) * MLA_QSCALE
        o_ref[0, h, :, :KV_RANK] = ql.astype(BF16)
        o_ref[0, h, :, KV_RANK:] = qr.astype(BF16)


def _qprep_t_kernel(zq_ref, wuk_ref, cf_ref, sf_ref, o_ref, *, tq):
    z = zq_ref[0]
    cf = cf_ref[0]
    sf = sf_ref[0]
    nope_w = MLA_HEADS * MLA_NOPE
    eye = _eye_bf16(MLA_ROPE)
    for h in range(MLA_HEADS):
        qn = z[:, h * MLA_NOPE:(h + 1) * MLA_NOPE].astype(BF16)
        ql_t = _nt_dot(wuk_ref[h], qn) * MLA_QSCALE
        qr = _rope32(z[:, nope_w + h * MLA_ROPE:nope_w + (h + 1) * MLA_ROPE], cf, sf) * MLA_QSCALE
        qr_t = _nt_dot(eye, qr.astype(BF16))
        o_ref[0, 0, :KV_RANK, h * tq:(h + 1) * tq] = ql_t.astype(BF16)
        o_ref[0, 0, KV_RANK:, h * tq:(h + 1) * tq] = qr_t.astype(BF16)


def q_prep_t(zq, wuk_r, cf, sf, tq):
    nb, m, w = zq.shape
    return pl.pallas_call(
        functools.partial(_qprep_t_kernel, tq=tq),
        grid=(nb, m // tq),
        in_specs=[pl.BlockSpec((1, tq, w), lambda b, i: (b, i, 0)),
                  pl.BlockSpec((MLA_HEADS, KV_RANK, MLA_NOPE), lambda b, i: (0, 0, 0)),
                  pl.BlockSpec((1, tq, MLA_ROPE), lambda b, i: (0, i, 0)),
                  pl.BlockSpec((1, tq, MLA_ROPE), lambda b, i: (0, i, 0))],
        out_specs=pl.BlockSpec((1, 1, MLA_QK, MLA_HEADS * tq), lambda b, i: (b, i, 0, 0)),
        out_shape=jax.ShapeDtypeStruct((nb, m // tq, MLA_QK, MLA_HEADS * tq), BF16),
        compiler_params=_params(("parallel", "parallel")),
        name="q_prep_t",
    )(zq, wuk_r, cf, sf)


def q_prep(zq, wuk_t, cf, sf, tm):
    nb, m, w = zq.shape
    return pl.pallas_call(
        _qprep_kernel,
        grid=(nb, m // tm),
        in_specs=[pl.BlockSpec((1, tm, w), lambda b, i: (b, i, 0)),
                  pl.BlockSpec((MLA_HEADS, MLA_NOPE, KV_RANK), lambda b, i: (0, 0, 0)),
                  pl.BlockSpec((1, tm, MLA_ROPE), lambda b, i: (0, i, 0)),
                  pl.BlockSpec((1, tm, MLA_ROPE), lambda b, i: (0, i, 0))],
        out_specs=pl.BlockSpec((1, MLA_HEADS, tm, MLA_QK), lambda b, i: (b, 0, i, 0)),
        out_shape=jax.ShapeDtypeStruct((nb, MLA_HEADS, m, MLA_QK), BF16),
        compiler_params=_params(("parallel", "parallel")),
        name="q_prep",
    )(zq, wuk_t, cf, sf)


MLA_PROMPT_COL_WIDTH = 512
MLA_PROMPT_TQ = 512


def _mla_prompt_kernel(qi_ref, ki_ref, qt_ref, k_ref, latt_ref, wuvt_ref, o_ref, m_scr, l_scr, acc_scr, *, tq, tk):
    step = pl.program_id(1)
    qi = qi_ref[step]
    ki = ki_ref[step]
    last_k = (qi * tq + (tq - 1)) // tk

    @pl.when(ki == 0)
    def _():
        m_scr[...] = jnp.full_like(m_scr, -jnp.inf)
        l_scr[...] = jnp.zeros_like(l_scr)
        acc_scr[...] = jnp.zeros_like(acc_scr)

    def update(masked):
        rows = MLA_HEADS * tq
        cw = min(MLA_PROMPT_COL_WIDTH, rows)
        groups = [slice(g * cw, (g + 1) * cw) for g in range(rows // cw)]
        st_next = _dot(k_ref[0], qt_ref[0, 0, :, groups[0]])
        for g, cs in enumerate(groups):
            st = st_next
            if masked:
                kpos = ki * tk + lax.broadcasted_iota(jnp.int32, st.shape, 0)
                col = cs.start + lax.broadcasted_iota(jnp.int32, st.shape, 1)
                qpos = qi * tq + jnp.bitwise_and(col, tq - 1)
                st = jnp.where(kpos <= qpos, st, -jnp.inf)
            m_prev = m_scr[:, cs]
            m_new = jnp.maximum(m_prev, jnp.max(st, axis=0, keepdims=True))
            alpha = jnp.exp2(m_prev - m_new)
            pt = jnp.exp2(st - m_new)
            l_scr[:, cs] = alpha * l_scr[:, cs] + jnp.sum(pt, axis=0, keepdims=True)
            m_scr[:, cs] = m_new
            if g + 1 < len(groups):
                st_next = _dot(k_ref[0], qt_ref[0, 0, :, groups[g + 1]])
            acc_scr[:, cs] = alpha * acc_scr[:, cs] + _dot(latt_ref[0], pt.astype(BF16))

    needs_mask = ki * tk + (tk - 1) > qi * tq

    @pl.when(needs_mask)
    def _():
        update(True)

    @pl.when(jnp.logical_not(needs_mask))
    def _():
        update(False)

    @pl.when(ki == last_k)
    def _():
        ot = (acc_scr[...] / l_scr[...]).astype(BF16)
        for h in range(MLA_HEADS):
            o_ref[0, h * MLA_V:(h + 1) * MLA_V, :] = _dot(wuvt_ref[h], ot[:, h * tq:(h + 1) * tq]).astype(BF16)


def mla_prompt(qt, kcat, latt, wuv_t, tk):
    nb, nq, _, rows = qt.shape
    tq = rows // MLA_HEADS
    t = nq * tq
    tk = min(tk, t)
    pairs =[(i, j) for i in range(nq) for j in range((i * tq + tq - 1) // tk + 1)]
    qi_tab = jnp.asarray([p[0] for p in pairs], jnp.int32)
    ki_tab = jnp.asarray([p[1] for p in pairs], jnp.int32)
    grid_spec = pltpu.PrefetchScalarGridSpec(
        num_scalar_prefetch=2,
        grid=(nb, len(pairs)),
        in_specs=[pl.BlockSpec((1, 1, MLA_QK, rows), lambda b, s, qi, ki: (b, qi[s], 0, 0)),
                  pl.BlockSpec((1, tk, MLA_QK), lambda b, s, qi, ki: (b, ki[s], 0)),
                  pl.BlockSpec((1, KV_RANK, tk), lambda b, s, qi, ki: (b, 0, ki[s])),
                  pl.BlockSpec((MLA_HEADS, MLA_V, KV_RANK), lambda b, s, qi, ki: (0, 0, 0))],
        out_specs=pl.BlockSpec((1, MLA_HEADS * MLA_V, tq), lambda b, s, qi, ki: (b, 0, qi[s])),
        scratch_shapes=[pltpu.VMEM((1, rows), F32), pltpu.VMEM((1, rows), F32), pltpu.VMEM((KV_RANK, rows), F32)],
    )
    return pl.pallas_call(
        functools.partial(_mla_prompt_kernel, tq=tq, tk=tk),
        grid_spec=grid_spec,
        out_shape=jax.ShapeDtypeStruct((nb, MLA_HEADS * MLA_V, t), BF16),
        compiler_params=_params(("parallel", "arbitrary")),
        name="mla_prompt",
    )(qi_tab, ki_tab, qt, kcat, latt, wuv_t)


def _mla_sample_kernel(pt_ref, q_ref, kn_ref, wuv_ref, lat_hbm, kr_hbm, o_ref, kl_buf, kp_buf, sem,
                       *, layer, n_pages, n_pg, n_grp, t_new):
    b = pl.program_id(0)
    rows = t_new * MLA_HEADS
    n_chunks = n_pages // n_pg
    per = n_pg // n_grp

    def page_copies(bb, c, slot):
        cps = []
        for i in range(n_pg):
            page = pt_ref[bb * n_pages + c * n_pg + i]
            cps.append(pltpu.make_async_copy(lat_hbm.at[layer, page], kl_buf.at[slot, i], sem.at[slot, 0]))
            cps.append(pltpu.make_async_copy(kr_hbm.at[layer, page], kp_buf.at[slot, i], sem.at[slot, 1]))
        return cps

    @pl.when(b == 0)
    def _():
        for cp in page_copies(0, 0, 0):
            cp.start()

    q = q_ref[0]
    ql = q[:, :KV_RANK]
    qr = q[:, KV_RANK:]
    m_g = [jnp.full((rows, 1), -jnp.inf, F32) for _ in range(n_grp)]
    l_g = [jnp.zeros((rows, 1), F32) for _ in range(n_grp)]
    acc_g = [jnp.zeros((rows, KV_RANK), F32) for _ in range(n_grp)]
    for c in range(n_chunks):
        slot = c % 2
        if c + 1 < n_chunks:
            for cp in page_copies(b, c + 1, 1 - slot):
                cp.start()
        else:
            @pl.when(b + 1 < pl.num_programs(0))
            def _():
                for cp in page_copies(b + 1, 0, 1 - slot):
                    cp.start()
        for cp in page_copies(b, c, slot):
            cp.wait()
        kls = [kl_buf[slot, i].astype(BF16) for i in range(n_pg)]
        ss = [_nt_dot(ql, kls[i]) + _dot(qr, kp_buf[slot, i].astype(BF16)) for i in range(n_pg)]
        alphas, ps = [], []
        for gi in range(n_grp):
            s = jnp.concatenate(ss[gi * per:(gi + 1) * per], axis=1)
            m_new = jnp.maximum(m_g[gi], jnp.max(s, axis=1, keepdims=True))
            alpha = jnp.exp2(m_g[gi] - m_new)
            p = jnp.exp2(s - m_new).astype(BF16)
            l_g[gi] = alpha * l_g[gi] + jnp.sum(p.astype(F32), axis=1, keepdims=True)
            m_g[gi] = m_new
            alphas.append(alpha)
            ps.append(p)
        for gi in range(n_grp):
            pv = _dot(ps[gi][:, :PAGE_SIZE], kls[gi * per])
            for i in range(1, per):
                pv = pv + _dot(ps[gi][:, i * PAGE_SIZE:(i + 1) * PAGE_SIZE], kls[gi * per + i])
            acc_g[gi] = alphas[gi] * acc_g[gi] + pv

    qf = q.astype(F32)
    kn = kn_ref[0]
    trow = lax.broadcasted_iota(jnp.int32, (rows, 1), 0) // MLA_HEADS
    cols = []
    for jj in range(t_new):
        sj = jnp.sum(qf * kn[jj:jj + 1, :], axis=1, keepdims=True)
        cols.append(jnp.where(trow >= jj, sj, -jnp.inf))
    m1 = m_g[0]
    for gi in range(1, n_grp):
        m1 = jnp.maximum(m1, m_g[gi])
    for sj in cols:
        m1 = jnp.maximum(m1, sj)
    l1 = jnp.zeros_like(m1)
    acc1 = jnp.zeros((rows, KV_RANK), F32)
    for gi in range(n_grp):
        ag = jnp.exp2(m_g[gi] - m1)
        l1 = l1 + ag * l_g[gi]
        acc1 = acc1 + ag * acc_g[gi]
    for jj, sj in enumerate(cols):
        pj = jnp.exp2(sj - m1)
        l1 = l1 + pj
        acc1 = acc1 + pj * kn[jj:jj + 1, :KV_RANK]
    o = (acc1 / l1).astype(BF16)
    proj = _dot(o, wuv_ref[...])
    rr = lax.broadcasted_iota(jnp.int32, proj.shape, 0)
    cc = lax.broadcasted_iota(jnp.int32, proj.shape, 1)
    proj = jnp.where(jnp.bitwise_and(rr, MLA_HEADS - 1) == cc // MLA_V, proj, 0.0)
    o_ref[0] = jnp.sum(proj.reshape(t_new, MLA_HEADS, MLA_HEADS * MLA_V), axis=1).astype(BF16)


MLA_SAMPLE_PAGES_PER_CHUNK = 32


def mla_sample(q_s, kn_s, wuv_all, cache_lat, cache_kr, layer, page_table):
    nb, rows, _ = q_s.shape
    t_new = rows // MLA_HEADS
    n_pages = page_table.shape[1]
    n_pg = min(MLA_SAMPLE_PAGES_PER_CHUNK, n_pages // 2)
    assert n_pages % (2 * n_pg) == 0
    n_grp = 2 if n_pg % 2 == 0 else 1
    grid_spec = pltpu.PrefetchScalarGridSpec(
        num_scalar_prefetch=1,
        grid=(nb,),
        in_specs=[pl.BlockSpec((1, rows, MLA_QK), lambda b, pt: (b, 0, 0)),
                  pl.BlockSpec((1, t_new, MLA_QK), lambda b, pt: (b, 0, 0)),
                  pl.BlockSpec((KV_RANK, MLA_HEADS * MLA_V), lambda b, pt: (0, 0)),
                  pl.BlockSpec(memory_space=pl.ANY),
                  pl.BlockSpec(memory_space=pl.ANY)],
        out_specs=pl.BlockSpec((1, t_new, MLA_HEADS * MLA_V), lambda b, pt: (b, 0, 0)),
        scratch_shapes=[pltpu.VMEM((2, n_pg, PAGE_SIZE, KV_RANK), F32),
                        pltpu.VMEM((2, n_pg, MLA_ROPE, PAGE_SIZE), F32),
                        pltpu.SemaphoreType.DMA((2, 2))],
    )
    return pl.pallas_call(
        functools.partial(_mla_sample_kernel, layer=layer, n_pages=n_pages, n_pg=n_pg, n_grp=n_grp, t_new=t_new),
        grid_spec=grid_spec,
        out_shape=jax.ShapeDtypeStruct((nb, t_new, MLA_HEADS * MLA_V), BF16),
        compiler_params=_params(("arbitrary",)),
        name="mla_sample",
    )(page_table.reshape(-1), q_s, kn_s, wuv_all, cache_lat, cache_kr)


RW_TENSORS = 7


def _shifted_rows(zr, carry):
    first = lax.broadcasted_iota(jnp.int32, zr.shape, 0) == 0
    prev = jnp.where(first, carry[...], pltpu.roll(zr, 1, 0))
    carry[...] = zr[zr.shape[0] - 1:, :]
    return prev


def _rw_prep_heads(zr, prev, prm, put):
    mu_ref, w0_ref, ww2_ref, a0_ref, wa2_ref, wg2_ref, kk_ref, ka_ref = prm
    zs = zr + (prev - zr) * mu_ref[...]
    o3 = 3 * RW_W
    o4 = o3 + RW_DECAY_LORA
    o5 = o4 + RW_A_LORA
    xr, xk, xv = zs[:, :RW_W], zs[:, RW_W:2 * RW_W], zs[:, 2 * RW_W:o3]
    xw, xa, xg = zs[:, o3:o4], zs[:, o4:o5], zs[:, o5:]
    wl = w0_ref[...] + _dot(jnp.tanh(xw).astype(BF16), ww2_ref[...])
    w_log = -(jnp.maximum(-wl, 0.0) + jnp.log1p(jnp.exp(-jnp.abs(wl)))) - 0.5
    logw = -jnp.exp(w_log)
    a = jax.nn.sigmoid(a0_ref[...] + _dot(xa.astype(BF16), wa2_ref[...]))
    g = _dot(jax.nn.sigmoid(xg).astype(BF16), wg2_ref[...])
    kkf = xk * kk_ref[...]
    kf = xk * (1.0 + (a - 1.0) * ka_ref[...])
    for h in range(RW_HEADS):
        sl = slice(h * RW_N, (h + 1) * RW_N)
        kkh = kkf[:, sl]
        kkh = kkh / jnp.maximum(jnp.sqrt(jnp.sum(kkh * kkh, axis=1, keepdims=True)), 1e-12)
        for i, val in enumerate((xr[:, sl], kf[:, sl], xv[:, sl], kkh, kkh * a[:, sl], logw[:, sl], g[:, sl])):
            put(i, h, val)


def _rwprep_kernel(zr_ref, pv_ref, *rest, shift_in_kernel):
    prm = rest[:8]
    outs = rest[8:8 + RW_TENSORS]
    zr = zr_ref[0]
    if shift_in_kernel:
        carry = rest[8 + RW_TENSORS]

        @pl.when(pl.program_id(1) == 0)
        def _():
            carry[...] = pv_ref[0]

        prev = _shifted_rows(zr, carry)
    else:
        prev = pv_ref[0]

    def put(i, h, val):
        outs[i][0, h] = val

    _rw_prep_heads(zr, prev, prm, put)


def rwkv_prep(zr, prev, mu, w0, w_w2, a0, w_a2, w_g2, k_k, k_a, tm):
    nb, m, w = zr.shape
    shift_in_kernel = prev.shape[1] == 1 and m > 1
    vec = lambda n: pl.BlockSpec((1, n), lambda b, i: (0, 0))
    mat = lambda a: pl.BlockSpec(a.shape, lambda b, i: (0, 0))
    out_spec = pl.BlockSpec((1, RW_HEADS, tm, RW_N), lambda b, i: (b, 0, i, 0))
    out_sds = jax.ShapeDtypeStruct((nb, RW_HEADS, m, RW_N), F32)
    prev_spec = (pl.BlockSpec((1, 1, w), lambda b, i: (b, 0, 0)) if shift_in_kernel else
                 pl.BlockSpec((1, tm, w), lambda b, i: (b, i, 0)))
    return pl.pallas_call(
        functools.partial(_rwprep_kernel, shift_in_kernel=shift_in_kernel),
        grid=(nb, m // tm),
        in_specs=[pl.BlockSpec((1, tm, w), lambda b, i: (b, i, 0)),
                  prev_spec,
                  vec(w), vec(RW_W), mat(w_w2), vec(RW_W), mat(w_a2), mat(w_g2), vec(RW_W), vec(RW_W)],
        out_specs=[out_spec] * 7,
        out_shape=[out_sds] * 7,
        scratch_shapes=[pltpu.VMEM((1, w), F32)] if shift_in_kernel else [],
        compiler_params=_params(("parallel", "arbitrary")),
        name="rwkv_prep",
    )(zr, prev, mu.reshape(1, w), w0.reshape(1, RW_W), w_w2, a0.reshape(1, RW_W), w_a2, w_g2,
      k_k.reshape(1, RW_W), k_a.reshape(1, RW_W))


def _split_bf16(x, terms):
    parts = []
    rem = x
    for i in range(terms):
        p = rem.astype(BF16)
        parts.append(p)
        if i + 1 < terms:
            rem = rem - p.astype(F32)
    return parts


def _mm(a, b, ta, tb, dot=_dot):
    ap = _split_bf16(a, ta)
    bp = _split_bf16(b, tb)
    n = max(ta, tb)
    acc = None
    for i, x in enumerate(ap):
        for j, y in enumerate(bp):
            if i + j < n:
                d = dot(x, y)
                acc = d if acc is None else acc + d
    return acc


RW_P_CUMSUM = 2
RW_P_INTRA = 1
RW_P_INV = 1
RW_P_STATE = 1


def _rwkv_chunk(get, rk_ref, lnw_ref, lnb_ref, st_ref, chunk):
    heads = range(RW_HEADS)
    r_h, k_h, v_h, kk_h, kka_h, lw_h, g_h = ([get(i, h) for h in heads] for i in range(RW_TENSORS))
    c2 = 2 * chunk
    row = lax.broadcasted_iota(jnp.int32, (chunk, chunk), 0)
    col = lax.broadcasted_iota(jnp.int32, (chunk, chunk), 1)
    tri = jnp.where(col <= row, 1.0, 0.0).astype(BF16)
    eye_c = jnp.where(row == col, 1.0, 0.0).astype(F32)
    row2 = lax.broadcasted_iota(jnp.int32, (c2, c2), 0)
    col2 = jnp.bitwise_and(lax.broadcasted_iota(jnp.int32, (c2, c2), 1), chunk - 1)
    mask2 = col2 < jnp.where(row2 < chunk, row2, row2 - (chunk - 1))
    eye_n = lax.broadcasted_iota(jnp.int32, (RW_N, RW_N), 0) == lax.broadcasted_iota(jnp.int32, (RW_N, RW_N), 1)
    zeros_cn = jnp.zeros((chunk, RW_N), F32)
    n_double = int(math.log2(chunk)) - 1
    cs = [_mm(tri, lw_h[h], 1, RW_P_CUMSUM) for h in heads]
    lhs, rhs, g_end, g_end_col = [], [], [], []
    for h in heads:
        g_incl = jnp.exp(cs[h])
        g_prev = jnp.exp(cs[h] - lw_h[h])
        g_inv = jnp.exp(-cs[h])
        cs_last = cs[h][chunk - 1:chunk, :]
        g_end.append(jnp.exp(cs_last))
        g_end_col.append(jnp.exp(jnp.sum(jnp.where(eye_n, jnp.broadcast_to(cs_last, (RW_N, RW_N)), 0.0),
                                         axis=1, keepdims=True)))
        lhs.append(jnp.concatenate([-kk_h[h] * g_prev, r_h[h] * g_incl], axis=0))
        rhs.append(jnp.concatenate([kka_h[h] * g_inv, k_h[h] * g_inv], axis=0))
    mx = [jnp.where(mask2, _mm(lhs[h], rhs[h], RW_P_INTRA, RW_P_INTRA, _nt_dot), 0.0) for h in heads]
    from_state = [_mm(lhs[h], st_ref[0, h], RW_P_STATE, RW_P_STATE) for h in heads]
    from_v = [_mm(mx[h], jnp.concatenate([zeros_cn, v_h[h]], axis=0), RW_P_INTRA, RW_P_INTRA)
              for h in heads]
    l_ab = [mx[h][:chunk, :chunk] for h in heads]
    tinv = [eye_c + l_ab[h] for h in heads]
    pw = [_mm(l_ab[h], l_ab[h], RW_P_INV, RW_P_INV) for h in heads]
    for _ in range(n_double - 1):
        both = [_mm(jnp.concatenate([pw[h], tinv[h]], axis=0), pw[h], RW_P_INV, RW_P_INV) for h in heads]
        tinv = [tinv[h] + both[h][chunk:] for h in heads]
        pw = [both[h][:chunk] for h in heads]
    tinv = [tinv[h] + _mm(tinv[h], pw[h], RW_P_INV, RW_P_INV) for h in heads]
    u = [_mm(tinv[h], from_state[h][:chunk] + from_v[h][:chunk], RW_P_INV, RW_P_INV) for h in heads]
    y_u = [_mm(mx[h][chunk:, :chunk], u[h], RW_P_INTRA, RW_P_INTRA) for h in heads]
    st_add = [_mm(rhs[h] * g_end[h], jnp.concatenate([u[h], v_h[h]], axis=0), RW_P_STATE, RW_P_STATE, _tn_dot)
              for h in heads]
    outs = []
    for h in heads:
        st_ref[0, h] = st_ref[0, h] * g_end_col[h] + st_add[h]
        y = from_state[h][chunk:] + from_v[h][chunk:] + y_u[h]
        mean = jnp.mean(y, axis=1, keepdims=True)
        yc = y - mean
        var = jnp.mean(yc * yc, axis=1, keepdims=True)
        yn = yc * lax.rsqrt(var + RW_GN_EPS) * lnw_ref[h:h + 1, :] + lnb_ref[h:h + 1, :]
        bonus = jnp.sum(r_h[h] * k_h[h] * rk_ref[h:h + 1, :], axis=1, keepdims=True) * v_h[h]
        outs.append((yn + bonus) * g_h[h])
    return jnp.concatenate(outs, axis=1)


def _rwkv_scan_kernel(r_ref, k_ref, v_ref, kk_ref, kka_ref, lw_ref, g_ref, rk_ref, lnw_ref, lnb_ref, s0_ref,
                      o_ref, st_ref, *, chunk):
    @pl.when(pl.program_id(1) == 0)
    def _():
        st_ref[0] = s0_ref[0]

    refs = (r_ref, k_ref, v_ref, kk_ref, kka_ref, lw_ref, g_ref)
    out = _rwkv_chunk(lambda i, h: refs[i][0, h], rk_ref, lnw_ref, lnb_ref, st_ref, chunk)
    o_ref[0] = out.astype(BF16)


def _rwkv_fused_kernel(zr0_ref, zra_ref, zrb_ref, sp_ref, *rest, chunk):
    prm = rest[:8]
    rk_ref, lnw_ref, lnb_ref, s0_ref, o_ref, st_ref, buf, carry = rest[8:]
    j = pl.program_id(1)

    def prep(zr, slot):
        prev = _shifted_rows(zr, carry)

        def put(i, h, val):
            buf[slot, i, h] = val

        _rw_prep_heads(zr, prev, prm, put)

    @pl.when(j == 0)
    def _():
        st_ref[0] = s0_ref[0]
        carry[...] = sp_ref[0]
        prep(zr0_ref[0], 0)

    for slot, nxt_ref in ((0, zra_ref), (1, zrb_ref)):
        out = _rwkv_chunk(lambda i, h: buf[slot, i, h], rk_ref, lnw_ref, lnb_ref, st_ref, chunk)
        o_ref[0, slot * chunk:(slot + 1) * chunk, :] = out.astype(BF16)
        prep(nxt_ref[0], 1 - slot)


def rwkv_fused(zr, shift_prev, mu, w0, w_w2, a0, w_a2, w_g2, k_k, k_a, r_k, ln_w, ln_b, s0_t, chunk):
    nb, t, w = zr.shape
    nc = t // chunk
    assert nc % 2 == 0
    vec = lambda n: pl.BlockSpec((1, n), lambda b, j: (0, 0))
    mat = lambda a: pl.BlockSpec(a.shape, lambda b, j: (0, 0))
    hspec = pl.BlockSpec((RW_HEADS, RW_N), lambda b, j: (0, 0))
    sspec = pl.BlockSpec((1, RW_HEADS, RW_N, RW_N), lambda b, j: (b, 0, 0, 0))
    return pl.pallas_call(
        functools.partial(_rwkv_fused_kernel, chunk=chunk),
        grid=(nb, nc // 2),
        in_specs=[pl.BlockSpec((1, chunk, w), lambda b, j: (b, 0, 0)),
                  pl.BlockSpec((1, chunk, w), lambda b, j: (b, 2 * j + 1, 0)),
                  pl.BlockSpec((1, chunk, w), lambda b, j: (b, jnp.minimum(2 * j + 2, nc - 1), 0)),
                  pl.BlockSpec((1, 1, w), lambda b, j: (b, 0, 0)),
                  vec(w), vec(RW_W), mat(w_w2), vec(RW_W), mat(w_a2), mat(w_g2), vec(RW_W), vec(RW_W),
                  hspec, hspec, hspec, sspec],
        out_specs=[pl.BlockSpec((1, 2 * chunk, RW_W), lambda b, j: (b, j, 0)), sspec],
        out_shape=[jax.ShapeDtypeStruct((nb, t, RW_W), BF16),
                   jax.ShapeDtypeStruct((nb, RW_HEADS, RW_N, RW_N), F32)],
        scratch_shapes=[pltpu.VMEM((2, RW_TENSORS, RW_HEADS, chunk, RW_N), F32), pltpu.VMEM((1, w), F32)],
        compiler_params=_params(("parallel", "arbitrary")),
        name="rwkv_fused",
    )(zr, zr, zr, shift_prev, mu.reshape(1, w), w0.reshape(1, RW_W), w_w2, a0.reshape(1, RW_W), w_a2, w_g2,
      k_k.reshape(1, RW_W), k_a.reshape(1, RW_W), r_k, ln_w.reshape(RW_HEADS, RW_N), ln_b.reshape(RW_HEADS, RW_N),
      s0_t)


def rwkv_scan(r, k, v, kk, kka, lw, g, r_k, ln_w, ln_b, s0_t, chunk):
    nb, _, t, _ = r.shape
    tspec = pl.BlockSpec((1, RW_HEADS, chunk, RW_N), lambda b, c: (b, 0, c, 0))
    hspec = pl.BlockSpec((RW_HEADS, RW_N), lambda b, c: (0, 0))
    sspec = pl.BlockSpec((1, RW_HEADS, RW_N, RW_N), lambda b, c: (b, 0, 0, 0))
    return pl.pallas_call(
        functools.partial(_rwkv_scan_kernel, chunk=chunk),
        grid=(nb, t // chunk),
        in_specs=[tspec] * 7 + [hspec] * 3 + [sspec],
        out_specs=[pl.BlockSpec((1, chunk, RW_W), lambda b, c: (b, c, 0)), sspec],
        out_shape=[jax.ShapeDtypeStruct((nb, t, RW_W), BF16),
                   jax.ShapeDtypeStruct((nb, RW_HEADS, RW_N, RW_N), F32)],
        compiler_params=_params(("parallel", "arbitrary")),
        name="rwkv_scan",
    )(r, k, v, kk, kka, lw, g, r_k, ln_w.reshape(RW_HEADS, RW_N), ln_b.reshape(RW_HEADS, RW_N), s0_t)


def _rwkv_step_kernel(r_ref, k_ref, v_ref, kk_ref, kka_ref, lw_ref, g_ref, rk_ref, lnw_ref, lnb_ref, s0_ref,
                      o_ref, s_ref, w_scr, y_scr, *, t_new):
    for t in range(t_new):
        w_scr[t] = jnp.exp(lw_ref[t, 0])

    def value_row(vi, carry):
        s = s0_ref[0, vi]
        for t in range(t_new):
            sa = -jnp.sum(s * kk_ref[t, 0], axis=0, keepdims=True)
            s = s * w_scr[t] + sa * kka_ref[t, 0] + v_ref[t, 0, pl.ds(vi, 1), :] * k_ref[t, 0]
            y_scr[t, pl.ds(vi, 1), :] = jnp.sum(s * r_ref[t, 0], axis=0, keepdims=True)
        s_ref[0, vi] = s
        return carry

    lax.fori_loop(0, RW_N, value_row, 0)
    for t in range(t_new):
        y = y_scr[t]
        mean = jnp.mean(y, axis=0, keepdims=True)
        yc = y - mean
        var = jnp.mean(yc * yc, axis=0, keepdims=True)
        yn = yc * lax.rsqrt(var + RW_GN_EPS) * lnw_ref[0] + lnb_ref[0]
        bonus = jnp.sum(r_ref[t, 0] * k_ref[t, 0] * rk_ref[0], axis=0, keepdims=True) * v_ref[t, 0]
        o_ref[t, 0] = (yn + bonus) * g_ref[t, 0]


def rwkv_step(r, k, v, kk, kka, lw, g, r_k, ln_w, ln_b, s0):
    t_new, nh, n, nb = r.shape
    tspec = pl.BlockSpec((t_new, 1, n, nb), lambda h: (0, h, 0, 0))
    hspec = pl.BlockSpec((1, n, nb), lambda h: (h, 0, 0))
    sspec = pl.BlockSpec((1, n, n, nb), lambda h: (h, 0, 0, 0))
    return pl.pallas_call(
        functools.partial(_rwkv_step_kernel, t_new=t_new),
        grid=(nh,),
        in_specs=[tspec] * 7 + [hspec] * 3 + [sspec],
        out_specs=[tspec, sspec],
        out_shape=[jax.ShapeDtypeStruct((t_new, nh, n, nb), F32), jax.ShapeDtypeStruct((nh, n, n, nb), F32)],
        scratch_shapes=[pltpu.VMEM((t_new, n, nb), F32), pltpu.VMEM((t_new, n, nb), F32)],
        compiler_params=_params(("parallel",)),
        name="rwkv_step",
    )(r, k, v, kk, kka, lw, g, r_k, ln_w, ln_b, s0)


def _retention_kernel(lg_ref, qk_ref, v_ref, g_ref, cos_ref, sin_ref, s0_ref, o_ref, s_ref, *, lb, l_true):
    c = pl.program_id(1)

    @pl.when(c == 0)
    def _():
        s_ref[0] = s0_ref[0]

    lp = max(lb, 16)
    cos = cos_ref[0]
    sin = sin_ref[0]
    half = RET_DK // 2
    qk_w = RET_HEADS * RET_DK

    def rope(x):
        x1, x2 = x[:, :half], x[:, half:]
        return jnp.concatenate([x1 * cos - x2 * sin, x1 * sin + x2 * cos], axis=1)

    def rows(x):
        if lp == lb:
            return x
        return jnp.concatenate([x, jnp.zeros((lp - lb, x.shape[1]), x.dtype)], axis=0)

    row = lax.broadcasted_iota(jnp.int32, (lp, lp), 0)
    col = lax.broadcasted_iota(jnp.int32, (lp, lp), 1)
    diff = (row - col).astype(F32)
    idx = lax.broadcasted_iota(jnp.int32, (lp, 1), 0).astype(F32)
    heads = range(RET_HEADS)
    qm, km, kdm, vm, dmask, row_dec = [], [], [], [], [], []
    for h in heads:
        lg = lg_ref[h]
        q = rows(rope(qk_ref[0, :, h * RET_DK:(h + 1) * RET_DK]))
        k = rows(rope(qk_ref[0, :, qk_w + h * RET_DK:qk_w + (h + 1) * RET_DK]) * (RET_DK ** -0.5))
        qm.append(q.astype(BF16))
        km.append(k.astype(BF16))
        kdm.append((k * jnp.exp((l_true - 1.0 - idx) * lg)).astype(BF16))
        vm.append(rows(v_ref[0, :, h * RET_DV:(h + 1) * RET_DV]).astype(BF16))
        dmask.append(jnp.where(diff >= 0, jnp.exp(jnp.maximum(diff, 0.0) * lg), 0.0))
        row_dec.append(jnp.exp((idx + 1.0) * lg))
    sc = [(_nt_dot(qm[h], km[h]) * dmask[h]).astype(BF16) for h in heads]
    cross = [_dot(qm[h], s_ref[0, h].astype(BF16)) * row_dec[h] for h in heads]
    s_add = [_tn_dot(kdm[h], vm[h]) for h in heads]
    inner = [_dot(sc[h], vm[h]) for h in heads]
    outs = []
    for h in heads:
        s_dec = jnp.exp(jnp.zeros((1, RET_DV), F32) + l_true * lg_ref[h])
        s_ref[0, h] = s_ref[0, h] * s_dec + s_add[h]
        o = (inner[h] + cross[h])[:lb]
        o = o * lax.rsqrt(jnp.mean(o * o, axis=1, keepdims=True) + NORM_EPS)
        gv = g_ref[0, :, h * RET_DV:(h + 1) * RET_DV]
        outs.append(o * (gv * jax.nn.sigmoid(gv)))
    o_ref[0] = jnp.concatenate(outs, axis=1).astype(BF16)


def retention(z, cos, sin, lg, s0, lb, l_true):
    nb, m, _ = z.shape
    vw = RET_HEADS * RET_DV
    assert 2 * RET_HEADS * RET_DK == vw
    sspec = pl.BlockSpec((1, RET_HEADS, RET_DK, RET_DV), lambda b, c: (b, 0, 0, 0))
    return pl.pallas_call(
        functools.partial(_retention_kernel, lb=lb, l_true=float(l_true)),
        grid=(nb, m // lb),
        in_specs=[pl.BlockSpec(memory_space=pltpu.SMEM),
                  pl.BlockSpec((1, lb, vw), lambda b, c: (b, c, 0)),
                  pl.BlockSpec((1, lb, vw), lambda b, c: (b, c, 1)),
                  pl.BlockSpec((1, lb, vw), lambda b, c: (b, c, 2)),
                  pl.BlockSpec((1, lb, RET_DK // 2), lambda b, c: (0, c, 0)),
                  pl.BlockSpec((1, lb, RET_DK // 2), lambda b, c: (0, c, 0)),
                  sspec],
        out_specs=[pl.BlockSpec((1, lb, vw), lambda b, c: (b, c, 0)), sspec],
        out_shape=[jax.ShapeDtypeStruct((nb, m, vw), BF16),
                   jax.ShapeDtypeStruct((nb, RET_HEADS, RET_DK, RET_DV), F32)],
        compiler_params=_params(("parallel", "arbitrary")),
        name="retention",
    )(lg, z, z, z, cos, sin, s0)


def _rope_tables(pos, half):
    inv = ROPE_BASE ** (-jnp.arange(half, dtype=F32) / half)
    ang = pos.astype(F32)[:, None] * inv[None, :]
    return jnp.cos(ang), jnp.sin(ang)


def _mla_tables(pos):
    cos, sin = _rope_tables(pos, MLA_ROPE // 2)
    return jnp.concatenate([cos, cos], axis=1), jnp.concatenate([-sin, sin], axis=1)


def _even_layer(x, mods, pos_tabs, prm, past, tm):
    (w_in_p, g_mix, g_kv, wuk_t, wuk_r, wuv_t, wuv_all, mu, w0, w_w2, a0, w_a2, w_g2, k_k, k_a, r_k, ln_w, ln_b,
     w_out_mla, w_out_rw) = prm
    sh1, sc1, gt1 = mods
    cf, sf = pos_tabs
    nb, m, _ = x.shape
    zr, zq, zkv = norm_mod_matmul_split(x, g_mix, sh1, sc1, w_in_p, (RW_SHIFT_W, MLA_HEADS * (MLA_NOPE + MLA_ROPE), MLA_QK), tm)
    lat, kr, kcat, latt = kv_prep(zkv, g_kv, cf, sf, tm)
    if past is None:
        qt = q_prep_t(zq, wuk_r, cf, sf, min(MLA_PROMPT_TQ, m))
        mla_out = mla_prompt(qt, kcat, latt, wuv_t, 512)
        mla_transposed = True
        s0_t = jnp.zeros((nb, RW_HEADS, RW_N, RW_N), F32)
        rw_out, s_t = rwkv_fused(zr, jnp.zeros((nb, 1, RW_SHIFT_W), F32), mu, w0, w_w2, a0, w_a2, w_g2, k_k, k_a,
                                 r_k, ln_w, ln_b, s0_t, RW_CHUNK)
        s_new = jnp.swapaxes(s_t, -1, -2)
        shift_new = zr[:, -1]
    else:
        cache_lat, cache_kr, layer, page_table, s0, shift_prev, t_new = past
        nbs = m // t_new
        mla_transposed = False
        qcat = q_prep(zq, wuk_t, cf, sf, tm)
        q_s = qcat.reshape(MLA_HEADS, nbs, t_new, MLA_QK).transpose(1, 2, 0, 3).reshape(nbs, t_new * MLA_HEADS, MLA_QK)
        kn_s = jnp.concatenate([lat, kr], axis=-1).reshape(nbs, t_new, MLA_QK)
        mla_out = mla_sample(q_s, kn_s, wuv_all, cache_lat, cache_kr, layer, page_table).reshape(1, m, MLA_HEADS * MLA_V)
        zr_b = zr.reshape(nbs, t_new, RW_SHIFT_W)
        prev = jnp.concatenate([shift_prev[:, None, :], zr_b[:, :-1]], axis=1).reshape(1, m, RW_SHIFT_W)
        tens = rwkv_prep(zr, prev, mu, w0, w_w2, a0, w_a2, w_g2, k_k, k_a, min(tm, 256))
        tens = [u.reshape(RW_HEADS, nbs, t_new, RW_N).transpose(2, 0, 3, 1) for u in tens]
        lanes = lambda p: jnp.broadcast_to(p.reshape(RW_HEADS, RW_N, 1), (RW_HEADS, RW_N, nbs))
        rw_l, s_l = rwkv_step(*tens, lanes(r_k), lanes(ln_w), lanes(ln_b), jnp.transpose(s0, (1, 2, 3, 0)))
        rw_out = rw_l.transpose(3, 0, 1, 2).reshape(1, m, RW_W).astype(BF16)
        s_new = jnp.transpose(s_l, (3, 0, 1, 2))
        shift_new = zr_b[:, -1]
    x_new = matmul_gate_res([mla_out, rw_out], [w_out_mla, w_out_rw], x, gt1, tm, (mla_transposed, False))
    return x_new, (lat, kr, s_new, shift_new)


def _odd_layer(x, mods, ret_tabs, prm, s0, t_new, tm):
    w_in, g_mix, w_out, lg = prm
    sh1, sc1, gt1 = mods
    cos, sin = ret_tabs
    nb, m, _ = x.shape
    z = norm_mod_matmul(x, g_mix, sh1, sc1, w_in, min(2 * tm, m), 2048)
    if s0 is None:
        s0 = jnp.zeros((nb, RET_HEADS, RET_DK, RET_DV), F32)
        lb = min(RET_BLOCK, m)
        o, s_new = retention(z, cos, sin, lg, s0, lb, lb)
    else:
        nbs = m // t_new
        lpad = 8
        z_b = jnp.pad(z.reshape(nbs, t_new, -1), ((0, 0), (0, lpad - t_new), (0, 0)))
        o, s_new = retention(z_b, cos, sin, lg, s0, lpad, t_new)
        o = o[:, :t_new].reshape(1, m, RET_HEADS * RET_DV)
    x_new = matmul_gate_res([o], [w_out], x, gt1, tm)
    return x_new, s_new


def kernel(x_prompt, x_sample, c_prompt, c_sample, cache_kv_latent, cache_k_rope, page_table, state_rwkv, state_rwkv_shift, state_ret, w_ada, b_ada, g_norm_mix, g_norm_mlp, g_final, w_in_even, g_kv, w_uk, w_uv, rw_mu, rw_w0, rw_w2, rw_a0, rw_a2, rw_g2, rw_k_k, rw_k_a, rw_r_k, rw_ln_w, rw_ln_b, w_out_even, w_in_odd, w_out_odd, w_ff1, w_ff2):
    nbp, t_p, d = x_prompt.shape
    nbs, t_s, _ = x_sample.shape
    depth = w_ada.shape[0]
    past_len = page_table.shape[1] * PAGE_SIZE
    m_s = nbs * t_s
    tm_p = min(512, t_p)
    tm_s = m_s

    c_all = jnp.concatenate([jnp.repeat(c_sample, t_s, axis=0), c_prompt], axis=0)
    c_all = jnp.pad(c_all, ((0, -c_all.shape[0] % 16), (0, 0)))
    mods_all = ada_proj(c_all, w_ada, b_ada)

    def group_mods(l):
        mp = mods_all[l, m_s:m_s + nbp].reshape(nbp, 1, 6, d)
        return [mp[:, :, i] for i in range(6)], [(mods_all, l, i) for i in range(6)]

    pos_p = jnp.arange(t_p)
    pos_s = past_len + jnp.arange(t_s)
    cf_p, sf_p = _mla_tables(pos_p)
    cf_s, sf_s = _mla_tables(pos_s)
    mla_tabs_p = (cf_p[None], sf_p[None])
    mla_tabs_s = (jnp.tile(cf_s, (nbs, 1))[None], jnp.tile(sf_s, (nbs, 1))[None])
    cr_p, sr_p = _rope_tables(pos_p, RET_DK // 2)
    cr_s, sr_s = _rope_tables(pos_s, RET_DK // 2)
    ret_tabs_p = (cr_p[None], sr_p[None])
    ret_tabs_s = (jnp.pad(cr_s, ((0, 8 - t_s), (0, 0)))[None], jnp.pad(sr_s, ((0, 8 - t_s), (0, 0)))[None])
    lg = jnp.log(1 - 2.0 ** (-5.0 - jnp.arange(RET_HEADS, dtype=F32)))

    xp = x_prompt
    xs = x_sample.reshape(1, m_s, d)
    lat_p, kr_p, rw_p, sh_p, ret_p = [], [], [], [], []
    lat_s, kr_s, rw_s, sh_s, ret_s = [], [], [], [], []
    q_w = MLA_HEADS * (MLA_NOPE + MLA_ROPE)
    for l in range(depth):
        (sh1p, sc1p, gt1p, sh2p, sc2p, gt2p), (sh1s, sc1s, gt1s, sh2s, sc2s, gt2s) = group_mods(l)
        i = l // 2
        if l % 2 == 0:
            w_in = w_in_even[i]
            wq = w_in[:, :q_w].reshape(d, MLA_HEADS, MLA_NOPE + MLA_ROPE)
            w_in_p = jnp.concatenate([w_in[:, q_w + MLA_QK:],
                                      wq[:, :, :MLA_NOPE].reshape(d, -1), wq[:, :, MLA_NOPE:].reshape(d, -1),
                                      w_in[:, q_w:q_w + MLA_QK]], axis=1).astype(BF16)
            wuv = w_uv[i]
            mla_w = MLA_HEADS * MLA_V
            prm = (w_in_p, g_norm_mix[l], g_kv[i], jnp.transpose(w_uk[i], (1, 2, 0)).astype(BF16),
                   jnp.transpose(w_uk[i], (1, 0, 2)).astype(BF16), jnp.transpose(wuv, (1, 2, 0)).astype(BF16), wuv.reshape(KV_RANK, mla_w).astype(BF16),
                   rw_mu[i], rw_w0[i], rw_w2[i].astype(BF16), rw_a0[i], rw_a2[i].astype(BF16), rw_g2[i].astype(BF16),
                   rw_k_k[i], rw_k_a[i], rw_r_k[i], rw_ln_w[i], rw_ln_b[i],
                   w_out_even[i, :mla_w].astype(BF16), w_out_even[i, mla_w:].astype(BF16))
            xp, (la, kr, st, sh) = _even_layer(xp, (sh1p, sc1p, gt1p), mla_tabs_p, prm, None, tm_p)
            lat_p.append(la); kr_p.append(kr); rw_p.append(st); sh_p.append(sh)
            cache_kr_t = jnp.swapaxes(cache_k_rope, 2, 3)
            past = (cache_kv_latent, cache_kr_t, i, page_table, state_rwkv[i], state_rwkv_shift[i], t_s)
            xs, (la, kr, st, sh) = _even_layer(xs, (sh1s, sc1s, gt1s), mla_tabs_s, prm, past, tm_s)
            lat_s.append(la.reshape(nbs, t_s, KV_RANK)); kr_s.append(kr.reshape(nbs, t_s, MLA_ROPE))
            rw_s.append(st); sh_s.append(sh)
        else:
            prm = (w_in_odd[i].astype(BF16), g_norm_mix[l], w_out_odd[i].astype(BF16), lg)
            xp, st = _odd_layer(xp, (sh1p, sc1p, gt1p), ret_tabs_p, prm, None, t_s, tm_p)
            ret_p.append(st)
            xs, st = _odd_layer(xs, (sh1s, sc1s, gt1s), ret_tabs_s, prm, state_ret[i], t_s, tm_s)
            ret_s.append(st)
        final = l == depth - 1
        w1 = w_ff1[l].astype(BF16)
        w2 = w_ff2[l].astype(BF16)
        xp = mlp_block(xp, g_norm_mlp[l], sh2p, sc2p, gt2p, w1, w2, g_final, final, min(2 * tm_p, t_p), 1024)
        xs = mlp_block(xs, g_norm_mlp[l], sh2s, sc2s, gt2s, w1, w2, g_final, final, tm_s, 1024)
    return (xp, xs.reshape(nbs, t_s, d),
            jnp.stack(lat_p), jnp.stack(kr_p), jnp.stack(rw_p), jnp.stack(sh_p), jnp.stack(ret_p),
            jnp.stack(lat_s), jnp.stack(kr_s), jnp.stack(rw_s), jnp.stack(sh_s), jnp.stack(ret_s))
```

```python
import functools
import math

import jax
import jax.numpy as jnp
from jax import lax
from jax.experimental import pallas as pl
from jax.experimental.pallas import tpu as pltpu

F32 = jnp.float32
BF16 = jnp.bfloat16
HIGHEST = lax.Precision.HIGHEST

D_MODEL = 1024
PAGE_SIZE = 128
MLA_HEADS = 8
MLA_NOPE = 64
MLA_ROPE = 32
MLA_V = 64
KV_RANK = 256
MLA_QK = KV_RANK + MLA_ROPE
MLA_SCALE = (MLA_NOPE + MLA_ROPE) ** -0.5
MLA_QSCALE = MLA_SCALE * math.log2(math.e)
RW_HEADS = 8
RW_N = 64
RW_W = RW_HEADS * RW_N
RW_DECAY_LORA = 64
RW_A_LORA = 64
RW_G_LORA = 128
RW_SHIFT_W = 3 * RW_W + RW_DECAY_LORA + RW_A_LORA + RW_G_LORA
RW_GN_EPS = 64e-5
RW_CHUNK = 64
RET_HEADS = 4
RET_DK = 256
RET_DV = 512
RET_BLOCK = 256
D_FF = 4 * D_MODEL
ROPE_BASE = 10000.0
NORM_EPS = 1e-6
MIB = 1024 * 1024


def _params(sem, vmem_mib=48):
    return pltpu.CompilerParams(dimension_semantics=sem, vmem_limit_bytes=vmem_mib * MIB)


def _rms(x, g):
    return x * lax.rsqrt(jnp.mean(x * x, axis=-1, keepdims=True) + NORM_EPS) * g


def _nt_dot(a, b, precision=None):
    return lax.dot_general(a, b, (((1,), (1,)), ((), ())), precision=precision, preferred_element_type=F32)


def _tn_dot(a, b, precision=None):
    return lax.dot_general(a, b, (((0,), (0,)), ((), ())), precision=precision, preferred_element_type=F32)


def _dot(a, b, precision=None):
    return jnp.dot(a, b, precision=precision, preferred_element_type=F32)


def _mod_arg(mod):
    return mod[0] if isinstance(mod, tuple) else mod


def _mod_spec(mod, tm, nmid):
    if isinstance(mod, tuple):
        _, layer, which = mod
        if nmid == 2:
            return pl.BlockSpec((1, tm, D_MODEL), lambda b, m, j: (layer, m, which))
        return pl.BlockSpec((1, tm, D_MODEL), lambda b, m: (layer, m, which))
    if mod.shape[1] == 1:
        if nmid == 2:
            return pl.BlockSpec((1, 1, mod.shape[2]), lambda b, m, j: (b, 0, 0))
        return pl.BlockSpec((1, 1, mod.shape[2]), lambda b, m: (b, 0, 0))
    if nmid == 2:
        return pl.BlockSpec((1, tm, mod.shape[2]), lambda b, m, j: (b, m, 0))
    return pl.BlockSpec((1, tm, mod.shape[2]), lambda b, m: (b, m, 0))


def _ada_kernel(c_ref, w_ref, b_ref, o_ref):
    o_ref[0] = _dot(c_ref[...].astype(BF16), w_ref[0].astype(BF16)) + b_ref[0]


def ada_proj(c, w_ada, b_ada):
    nl, d, n = w_ada.shape
    r = c.shape[0]
    tn = 1536
    return pl.pallas_call(
        _ada_kernel,
        grid=(nl, n // tn),
        in_specs=[pl.BlockSpec((r, d), lambda l, j: (0, 0)),
                  pl.BlockSpec((1, d, tn), lambda l, j: (l, 0, j)),
                  pl.BlockSpec((1, 1, tn), lambda l, j: (l, 0, j))],
        out_specs=pl.BlockSpec((1, r, tn), lambda l, j: (l, 0, j)),
        out_shape=jax.ShapeDtypeStruct((nl, r, n), F32),
        compiler_params=_params(("parallel", "parallel")),
        name="ada_proj",
    )(c, w_ada, b_ada.reshape(nl, 1, n))


def _nmm_split_kernel(x_ref, g_ref, sh_ref, sc_ref, w_ref, *o_refs, splits):
    h = (_rms(x_ref[0], g_ref[...]) * (1.0 + sc_ref[0]) + sh_ref[0]).astype(BF16)
    off = 0
    for o_ref, n in zip(o_refs, splits):
        o_ref[0] = _dot(h, w_ref[:, off:off + n])
        off += n


def norm_mod_matmul_split(x, g, shift, scale, w, splits, tm):
    nb, m, d = x.shape
    n = w.shape[1]
    return pl.pallas_call(
        functools.partial(_nmm_split_kernel, splits=splits),
        grid=(nb, m // tm),
        in_specs=[pl.BlockSpec((1, tm, d), lambda b, i: (b, i, 0)),
                  pl.BlockSpec((1, d), lambda b, i: (0, 0)),
                  _mod_spec(shift, tm, 1), _mod_spec(scale, tm, 1),
                  pl.BlockSpec((d, n), lambda b, i: (0, 0))],
        out_specs=[pl.BlockSpec((1, tm, s), lambda b, i: (b, i, 0)) for s in splits],
        out_shape=[jax.ShapeDtypeStruct((nb, m, s), F32) for s in splits],
        compiler_params=_params(("parallel", "parallel")),
        name="norm_mod_matmul_split",
    )(x, g.reshape(1, d), _mod_arg(shift), _mod_arg(scale), w)


def _nmm_kernel(x_ref, g_ref, sh_ref, sc_ref, w_ref, o_ref):
    h = (_rms(x_ref[0], g_ref[...]) * (1.0 + sc_ref[0]) + sh_ref[0]).astype(BF16)
    o_ref[0] = _dot(h, w_ref[...])


def norm_mod_matmul(x, g, shift, scale, w, tm, tn):
    nb, m, d = x.shape
    n = w.shape[1]
    return pl.pallas_call(
        _nmm_kernel,
        grid=(nb, m // tm, n // tn),
        in_specs=[pl.BlockSpec((1, tm, d), lambda b, i, j: (b, i, 0)),
                  pl.BlockSpec((1, d), lambda b, i, j: (0, 0)),
                  _mod_spec(shift, tm, 2), _mod_spec(scale, tm, 2),
                  pl.BlockSpec((d, tn), lambda b, i, j: (0, j))],
        out_specs=pl.BlockSpec((1, tm, tn), lambda b, i, j: (b, i, j)),
        out_shape=jax.ShapeDtypeStruct((nb, m, n), F32),
        compiler_params=_params(("parallel", "parallel", "arbitrary")),
        name="norm_mod_matmul",
    )(x, g.reshape(1, d), _mod_arg(shift), _mod_arg(scale), w)


def _mgr_kernel(*refs, n_pairs, transposed):
    a_refs = refs[:n_pairs]
    w_refs = refs[n_pairs:2 * n_pairs]
    res_ref, gt_ref, o_ref = refs[2 * n_pairs:]
    acc = None
    for a_ref, w_ref, tr in zip(a_refs, w_refs, transposed):
        d = (_tn_dot if tr else _dot)(a_ref[0], w_ref[...])
        acc = d if acc is None else acc + d
    o_ref[0] = res_ref[0] + gt_ref[0] * acc


def matmul_gate_res(a_list, w_list, res, gate, tm, transposed=None):
    nb, m, d = res.shape
    n_pairs = len(a_list)
    transposed = tuple(transposed or (False,) * n_pairs)
    in_specs = [pl.BlockSpec((1, a.shape[1], tm), lambda b, i: (b, 0, i)) if tr else
                pl.BlockSpec((1, tm, a.shape[2]), lambda b, i: (b, i, 0)) for a, tr in zip(a_list, transposed)]
    in_specs += [pl.BlockSpec(w.shape, lambda b, i: (0, 0)) for w in w_list]
    in_specs += [pl.BlockSpec((1, tm, d), lambda b, i: (b, i, 0)), _mod_spec(gate, tm, 1)]
    return pl.pallas_call(
        functools.partial(_mgr_kernel, n_pairs=n_pairs, transposed=transposed),
        grid=(nb, m // tm),
        in_specs=in_specs,
        out_specs=pl.BlockSpec((1, tm, d), lambda b, i: (b, i, 0)),
        out_shape=jax.ShapeDtypeStruct((nb, m, d), F32),
        compiler_params=_params(("parallel", "parallel")),
        name="matmul_gate_res",
    )(*a_list, *w_list, res, _mod_arg(gate))


def _mlp_kernel(x_ref, g_ref, sh_ref, sc_ref, gt_ref, w1_ref, w2_ref, gf_ref, o_ref, h_scr, acc_scr, *, final):
    f = pl.program_id(2)

    @pl.when(f == 0)
    def _():
        h_scr[...] = (_rms(x_ref[0], g_ref[...]) * (1.0 + sc_ref[0]) + sh_ref[0]).astype(BF16)
        acc_scr[...] = jnp.zeros_like(acc_scr)

    a = _dot(h_scr[...], w1_ref[...])
    a = jnp.square(jnp.maximum(a, 0.0)).astype(BF16)
    acc_scr[...] += _dot(a, w2_ref[...])

    @pl.when(f == pl.num_programs(2) - 1)
    def _():
        y = x_ref[0] + gt_ref[0] * acc_scr[...]
        if final:
            y = _rms(y, gf_ref[...])
        o_ref[0] = y


def mlp_block(x, g, shift, scale, gate, w1, w2, g_final, final, tm, tf):
    nb, m, d = x.shape
    dff = w1.shape[1]
    return pl.pallas_call(
        functools.partial(_mlp_kernel, final=final),
        grid=(nb, m // tm, dff // tf),
        in_specs=[pl.BlockSpec((1, tm, d), lambda b, i, f: (b, i, 0)),
                  pl.BlockSpec((1, d), lambda b, i, f: (0, 0)),
                  _mod_spec(shift, tm, 2), _mod_spec(scale, tm, 2), _mod_spec(gate, tm, 2),
                  pl.BlockSpec((d, tf), lambda b, i, f: (0, f)),
                  pl.BlockSpec((tf, d), lambda b, i, f: (f, 0)),
                  pl.BlockSpec((1, d), lambda b, i, f: (0, 0))],
        out_specs=pl.BlockSpec((1, tm, d), lambda b, i, f: (b, i, 0)),
        out_shape=jax.ShapeDtypeStruct((nb, m, d), F32),
        scratch_shapes=[pltpu.VMEM((tm, d), BF16), pltpu.VMEM((tm, d), F32)],
        compiler_params=_params(("parallel", "parallel", "arbitrary"), 56),
        name="mlp_block",
    )(x, g.reshape(1, d), _mod_arg(shift), _mod_arg(scale), _mod_arg(gate), w1, w2, g_final.reshape(1, d))


def _rope32(x, cf, sf):
    half = MLA_ROPE // 2
    sw = jnp.concatenate([x[:, half:], x[:, :half]], axis=1)
    return x * cf + sw * sf


def _eye_bf16(n):
    return jnp.where(lax.broadcasted_iota(jnp.int32, (n, n), 0) == lax.broadcasted_iota(jnp.int32, (n, n), 1),
                     1.0, 0.0).astype(BF16)


def _kvprep_kernel(zkv_ref, g_ref, cf_ref, sf_ref, lat_ref, kr_ref, kcat_ref, latt_ref):
    z = zkv_ref[0]
    lat = _rms(z[:, :KV_RANK], g_ref[...])
    kr = _rope32(z[:, KV_RANK:], cf_ref[0], sf_ref[0])
    lat_ref[0] = lat
    kr_ref[0] = kr
    lat_b = lat.astype(BF16)
    kcat_ref[0, :, :KV_RANK] = lat_b
    kcat_ref[0, :, KV_RANK:] = kr.astype(BF16)
    latt_ref[0] = _nt_dot(_eye_bf16(KV_RANK), lat_b).astype(BF16)


def kv_prep(zkv, g_kv, cf, sf, tm):
    nb, m, _ = zkv.shape
    return pl.pallas_call(
        _kvprep_kernel,
        grid=(nb, m // tm),
        in_specs=[pl.BlockSpec((1, tm, MLA_QK), lambda b, i: (b, i, 0)),
                  pl.BlockSpec((1, KV_RANK), lambda b, i: (0, 0)),
                  pl.BlockSpec((1, tm, MLA_ROPE), lambda b, i: (0, i, 0)),
                  pl.BlockSpec((1, tm, MLA_ROPE), lambda b, i: (0, i, 0))],
        out_specs=[pl.BlockSpec((1, tm, KV_RANK), lambda b, i: (b, i, 0)),
                   pl.BlockSpec((1, tm, MLA_ROPE), lambda b, i: (b, i, 0)),
                   pl.BlockSpec((1, tm, MLA_QK), lambda b, i: (b, i, 0)),
                   pl.BlockSpec((1, KV_RANK, tm), lambda b, i: (b, 0, i))],
        out_shape=[jax.ShapeDtypeStruct((nb, m, KV_RANK), F32),
                   jax.ShapeDtypeStruct((nb, m, MLA_ROPE), F32),
                   jax.ShapeDtypeStruct((nb, m, MLA_QK), BF16),
                   jax.ShapeDtypeStruct((nb, KV_RANK, m), BF16)],
        compiler_params=_params(("parallel", "parallel")),
        name="kv_prep",
    )(zkv, g_kv.reshape(1, KV_RANK), cf, sf)


def _qprep_kernel(zq_ref, wuk_ref, cf_ref, sf_ref, o_ref):
    z = zq_ref[0]
    cf = cf_ref[0]
    sf = sf_ref[0]
    nope_w = MLA_HEADS * MLA_NOPE
    for h in range(MLA_HEADS):
        qn = z[:, h * MLA_NOPE:(h + 1) * MLA_NOPE].astype(BF16)
        ql = _dot(qn, wuk_ref[h]) * MLA_QSCALE
        qr = _rope32(z[:, nope_w + h * MLA_ROPE:nope_w + (h + 1) * MLA_ROPE], cf, sf) * MLA_QSCALE
        o_ref[0, h, :, :KV_RANK] = ql.astype(BF16)
        o_ref[0, h, :, KV_RANK:] = qr.astype(BF16)


def _qprep_t_kernel(zq_ref, wuk_ref, cf_ref, sf_ref, o_ref, *, tq):
    z = zq_ref[0]
    cf = cf_ref[0]
    sf = sf_ref[0]
    nope_w = MLA_HEADS * MLA_NOPE
    eye = _eye_bf16(MLA_ROPE)
    for h in range(MLA_HEADS):
        qn = z[:, h * MLA_NOPE:(h + 1) * MLA_NOPE].astype(BF16)
        ql_t = _nt_dot(wuk_ref[h], qn) * MLA_QSCALE
        qr = _rope32(z[:, nope_w + h * MLA_ROPE:nope_w + (h + 1) * MLA_ROPE], cf, sf) * MLA_QSCALE
        qr_t = _nt_dot(eye, qr.astype(BF16))
        o_ref[0, 0, :KV_RANK, h * tq:(h + 1) * tq] = ql_t.astype(BF16)
        o_ref[0, 0, KV_RANK:, h * tq:(h + 1) * tq] = qr_t.astype(BF16)


def q_prep_t(zq, wuk_r, cf, sf, tq):
    nb, m, w = zq.shape
    return pl.pallas_call(
        functools.partial(_qprep_t_kernel, tq=tq),
        grid=(nb, m // tq),
        in_specs=[pl.BlockSpec((1, tq, w), lambda b, i: (b, i, 0)),
                  pl.BlockSpec((MLA_HEADS, KV_RANK, MLA_NOPE), lambda b, i: (0, 0, 0)),
                  pl.BlockSpec((1, tq, MLA_ROPE), lambda b, i: (0, i, 0)),
                  pl.BlockSpec((1, tq, MLA_ROPE), lambda b, i: (0, i, 0))],
        out_specs=pl.BlockSpec((1, 1, MLA_QK, MLA_HEADS * tq), lambda b, i: (b, i, 0, 0)),
        out_shape=jax.ShapeDtypeStruct((nb, m // tq, MLA_QK, MLA_HEADS * tq), BF16),
        compiler_params=_params(("parallel", "parallel")),
        name="q_prep_t",
    )(zq, wuk_r, cf, sf)


def q_prep(zq, wuk_t, cf, sf, tm):
    nb, m, w = zq.shape
    return pl.pallas_call(
        _qprep_kernel,
        grid=(nb, m // tm),
        in_specs=[pl.BlockSpec((1, tm, w), lambda b, i: (b, i, 0)),
                  pl.BlockSpec((MLA_HEADS, MLA_NOPE, KV_RANK), lambda b, i: (0, 0, 0)),
                  pl.BlockSpec((1, tm, MLA_ROPE), lambda b, i: (0, i, 0)),
                  pl.BlockSpec((1, tm, MLA_ROPE), lambda b, i: (0, i, 0))],
        out_specs=pl.BlockSpec((1, MLA_HEADS, tm, MLA_QK), lambda b, i: (b, 0, i, 0)),
        out_shape=jax.ShapeDtypeStruct((nb, MLA_HEADS, m, MLA_QK), BF16),
        compiler_params=_params(("parallel", "parallel")),
        name="q_prep",
    )(zq, wuk_t, cf, sf)


MLA_PROMPT_COL_WIDTH = 512
MLA_PROMPT_TQ = 512


def _mla_prompt_kernel(qi_ref, ki_ref, qt_ref, k_ref, latt_ref, wuvt_ref, o_ref, m_scr, l_scr, acc_scr, *, tq, tk):
    step = pl.program_id(1)
    qi = qi_ref[step]
    ki = ki_ref[step]
    last_k = (qi * tq + (tq - 1)) // tk

    @pl.when(ki == 0)
    def _():
        m_scr[...] = jnp.full_like(m_scr, -jnp.inf)
        l_scr[...] = jnp.zeros_like(l_scr)
        acc_scr[...] = jnp.zeros_like(acc_scr)

    def update(masked):
        rows = MLA_HEADS * tq
        cw = min(MLA_PROMPT_COL_WIDTH, rows)
        groups = [slice(g * cw, (g + 1) * cw) for g in range(rows // cw)]
        st_next = _dot(k_ref[0], qt_ref[0, 0, :, groups[0]])
        for g, cs in enumerate(groups):
            st = st_next
            if masked:
                kpos = ki * tk + lax.broadcasted_iota(jnp.int32, st.shape, 0)
                col = cs.start + lax.broadcasted_iota(jnp.int32, st.shape, 1)
                qpos = qi * tq + jnp.bitwise_and(col, tq - 1)
                st = jnp.where(kpos <= qpos, st, -jnp.inf)
            m_prev = m_scr[:, cs]
            m_new = jnp.maximum(m_prev, jnp.max(st, axis=0, keepdims=True))
            alpha = jnp.exp2(m_prev - m_new)
            pt = jnp.exp2(st - m_new)
            l_scr[:, cs] = alpha * l_scr[:, cs] + jnp.sum(pt, axis=0, keepdims=True)
            m_scr[:, cs] = m_new
            if g + 1 < len(groups):
                st_next = _dot(k_ref[0], qt_ref[0, 0, :, groups[g + 1]])
            acc_scr[:, cs] = alpha * acc_scr[:, cs] + _dot(latt_ref[0], pt.astype(BF16))

    needs_mask = ki * tk + (tk - 1) > qi * tq

    @pl.when(needs_mask)
    def _():
        update(True)

    @pl.when(jnp.logical_not(needs_mask))
    def _():
        update(False)

    @pl.when(ki == last_k)
    def _():
        ot = (acc_scr[...] / l_scr[...]).astype(BF16)
        for h in range(MLA_HEADS):
            o_ref[0, h * MLA_V:(h + 1) * MLA_V, :] = _dot(wuvt_ref[h], ot[:, h * tq:(h + 1) * tq]).astype(BF16)


def mla_prompt(qt, kcat, latt, wuv_t, tk):
    nb, nq, _, rows = qt.shape
    tq = rows // MLA_HEADS
    t = nq * tq
    tk = min(tk, t)
    pairs =[(i, j) for i in range(nq) for j in range((i * tq + tq - 1) // tk + 1)]
    qi_tab = jnp.asarray([p[0] for p in pairs], jnp.int32)
    ki_tab = jnp.asarray([p[1] for p in pairs], jnp.int32)
    grid_spec = pltpu.PrefetchScalarGridSpec(
        num_scalar_prefetch=2,
        grid=(nb, len(pairs)),
        in_specs=[pl.BlockSpec((1, 1, MLA_QK, rows), lambda b, s, qi, ki: (b, qi[s], 0, 0)),
                  pl.BlockSpec((1, tk, MLA_QK), lambda b, s, qi, ki: (b, ki[s], 0)),
                  pl.BlockSpec((1, KV_RANK, tk), lambda b, s, qi, ki: (b, 0, ki[s])),
                  pl.BlockSpec((MLA_HEADS, MLA_V, KV_RANK), lambda b, s, qi, ki: (0, 0, 0))],
        out_specs=pl.BlockSpec((1, MLA_HEADS * MLA_V, tq), lambda b, s, qi, ki: (b, 0, qi[s])),
        scratch_shapes=[pltpu.VMEM((1, rows), F32), pltpu.VMEM((1, rows), F32), pltpu.VMEM((KV_RANK, rows), F32)],
    )
    return pl.pallas_call(
        functools.partial(_mla_prompt_kernel, tq=tq, tk=tk),
        grid_spec=grid_spec,
        out_shape=jax.ShapeDtypeStruct((nb, MLA_HEADS * MLA_V, t), BF16),
        compiler_params=_params(("parallel", "arbitrary")),
        name="mla_prompt",
    )(qi_tab, ki_tab, qt, kcat, latt, wuv_t)


def _mla_sample_kernel(pt_ref, q_ref, kn_ref, wuv_ref, lat_hbm, kr_hbm, o_ref, kl_buf, kp_buf, sem,
                       *, layer, n_pages, n_pg, n_grp, n_slots, t_new):
    b = pl.program_id(0)
    rows = t_new * MLA_HEADS
    n_chunks = n_pages // n_pg
    per = n_pg // n_grp

    def page_copies(bb, c, slot):
        cps = []
        for i in range(n_pg):
            page = pt_ref[bb * n_pages + c * n_pg + i]
            cps.append(pltpu.make_async_copy(lat_hbm.at[layer, page], kl_buf.at[slot, i], sem.at[slot, 0]))
            cps.append(pltpu.make_async_copy(kr_hbm.at[layer, page], kp_buf.at[slot, i], sem.at[slot, 1]))
        return cps

    ahead = n_slots - 1

    @pl.when(b == 0)
    def _():
        for c0 in range(ahead):
            for cp in page_copies(0, c0, c0):
                cp.start()

    q = q_ref[0]
    ql = q[:, :KV_RANK]
    qr = q[:, KV_RANK:]
    m_g = [jnp.full((rows, 1), -jnp.inf, F32) for _ in range(n_grp)]
    l_g = [jnp.zeros((rows, 1), F32) for _ in range(n_grp)]
    acc_g = [jnp.zeros((rows, KV_RANK), F32) for _ in range(n_grp)]
    for c in range(n_chunks):
        slot = c % n_slots
        nxt = c + ahead
        if nxt < n_chunks:
            for cp in page_copies(b, nxt, nxt % n_slots):
                cp.start()
        else:
            @pl.when(b + 1 < pl.num_programs(0))
            def _():
                for cp in page_copies(b + 1, nxt - n_chunks, nxt % n_slots):
                    cp.start()
        for cp in page_copies(b, c, slot):
            cp.wait()
        kls = [kl_buf[slot, i].astype(BF16) for i in range(n_pg)]
        ss = [_nt_dot(ql, kls[i]) + _dot(qr, kp_buf[slot, i].astype(BF16)) for i in range(n_pg)]
        alphas, ps = [], []
        for gi in range(n_grp):
            s = jnp.concatenate(ss[gi * per:(gi + 1) * per], axis=1)
            m_new = jnp.maximum(m_g[gi], jnp.max(s, axis=1, keepdims=True))
            alpha = jnp.exp2(m_g[gi] - m_new)
            p = jnp.exp2(s - m_new).astype(BF16)
            l_g[gi] = alpha * l_g[gi] + jnp.sum(p.astype(F32), axis=1, keepdims=True)
            m_g[gi] = m_new
            alphas.append(alpha)
            ps.append(p)
        for gi in range(n_grp):
            pv = _dot(ps[gi][:, :PAGE_SIZE], kls[gi * per])
            for i in range(1, per):
                pv = pv + _dot(ps[gi][:, i * PAGE_SIZE:(i + 1) * PAGE_SIZE], kls[gi * per + i])
            acc_g[gi] = alphas[gi] * acc_g[gi] + pv

    qf = q.astype(F32)
    kn = kn_ref[0]
    trow = lax.broadcasted_iota(jnp.int32, (rows, 1), 0) // MLA_HEADS
    cols = []
    for jj in range(t_new):
        sj = jnp.sum(qf * kn[jj:jj + 1, :], axis=1, keepdims=True)
        cols.append(jnp.where(trow >= jj, sj, -jnp.inf))
    m1 = m_g[0]
    for gi in range(1, n_grp):
        m1 = jnp.maximum(m1, m_g[gi])
    for sj in cols:
        m1 = jnp.maximum(m1, sj)
    l1 = jnp.zeros_like(m1)
    acc1 = jnp.zeros((rows, KV_RANK), F32)
    for gi in range(n_grp):
        ag = jnp.exp2(m_g[gi] - m1)
        l1 = l1 + ag * l_g[gi]
        acc1 = acc1 + ag * acc_g[gi]
    for jj, sj in enumerate(cols):
        pj = jnp.exp2(sj - m1)
        l1 = l1 + pj
        acc1 = acc1 + pj * kn[jj:jj + 1, :KV_RANK]
    o = (acc1 / l1).astype(BF16)
    proj = _dot(o, wuv_ref[...])
    rr = lax.broadcasted_iota(jnp.int32, proj.shape, 0)
    cc = lax.broadcasted_iota(jnp.int32, proj.shape, 1)
    proj = jnp.where(jnp.bitwise_and(rr, MLA_HEADS - 1) == cc // MLA_V, proj, 0.0)
    o_ref[0] = jnp.sum(proj.reshape(t_new, MLA_HEADS, MLA_HEADS * MLA_V), axis=1).astype(BF16)


MLA_SAMPLE_PAGES_PER_CHUNK = 32
MLA_SAMPLE_SLOTS = 4


def mla_sample(q_s, kn_s, wuv_all, cache_lat, cache_kr, layer, page_table):
    nb, rows, _ = q_s.shape
    t_new = rows // MLA_HEADS
    n_pages = page_table.shape[1]
    n_pg = min(MLA_SAMPLE_PAGES_PER_CHUNK, n_pages // 2)
    n_slots = min(MLA_SAMPLE_SLOTS, n_pages // n_pg)
    assert n_pages % (n_slots * n_pg) == 0
    n_grp = 2 if n_pg % 2 == 0 else 1
    grid_spec = pltpu.PrefetchScalarGridSpec(
        num_scalar_prefetch=1,
        grid=(nb,),
        in_specs=[pl.BlockSpec((1, rows, MLA_QK), lambda b, pt: (b, 0, 0)),
                  pl.BlockSpec((1, t_new, MLA_QK), lambda b, pt: (b, 0, 0)),
                  pl.BlockSpec((KV_RANK, MLA_HEADS * MLA_V), lambda b, pt: (0, 0)),
                  pl.BlockSpec(memory_space=pl.ANY),
                  pl.BlockSpec(memory_space=pl.ANY)],
        out_specs=pl.BlockSpec((1, t_new, MLA_HEADS * MLA_V), lambda b, pt: (b, 0, 0)),
        scratch_shapes=[pltpu.VMEM((n_slots, n_pg, PAGE_SIZE, KV_RANK), F32),
                        pltpu.VMEM((n_slots, n_pg, MLA_ROPE, PAGE_SIZE), F32),
                        pltpu.SemaphoreType.DMA((n_slots, 2))],
    )
    return pl.pallas_call(
        functools.partial(_mla_sample_kernel, layer=layer, n_pages=n_pages, n_pg=n_pg, n_grp=n_grp,
                          n_slots=n_slots, t_new=t_new),
        grid_spec=grid_spec,
        out_shape=jax.ShapeDtypeStruct((nb, t_new, MLA_HEADS * MLA_V), BF16),
        compiler_params=_params(("arbitrary",)),
        name="mla_sample",
    )(page_table.reshape(-1), q_s, kn_s, wuv_all, cache_lat, cache_kr)


RW_TENSORS = 7


def _shifted_rows(zr, carry):
    first = lax.broadcasted_iota(jnp.int32, zr.shape, 0) == 0
    prev = jnp.where(first, carry[...], pltpu.roll(zr, 1, 0))
    carry[...] = zr[zr.shape[0] - 1:, :]
    return prev


def _rw_prep_heads(zr, prev, prm, put):
    mu_ref, w0_ref, ww2_ref, a0_ref, wa2_ref, wg2_ref, kk_ref, ka_ref = prm
    zs = zr + (prev - zr) * mu_ref[...]
    o3 = 3 * RW_W
    o4 = o3 + RW_DECAY_LORA
    o5 = o4 + RW_A_LORA
    xr, xk, xv = zs[:, :RW_W], zs[:, RW_W:2 * RW_W], zs[:, 2 * RW_W:o3]
    xw, xa, xg = zs[:, o3:o4], zs[:, o4:o5], zs[:, o5:]
    wl = w0_ref[...] + _dot(jnp.tanh(xw).astype(BF16), ww2_ref[...])
    w_log = -(jnp.maximum(-wl, 0.0) + jnp.log1p(jnp.exp(-jnp.abs(wl)))) - 0.5
    logw = -jnp.exp(w_log)
    a = jax.nn.sigmoid(a0_ref[...] + _dot(xa.astype(BF16), wa2_ref[...]))
    g = _dot(jax.nn.sigmoid(xg).astype(BF16), wg2_ref[...])
    kkf = xk * kk_ref[...]
    kf = xk * (1.0 + (a - 1.0) * ka_ref[...])
    for h in range(RW_HEADS):
        sl = slice(h * RW_N, (h + 1) * RW_N)
        kkh = kkf[:, sl]
        kkh = kkh / jnp.maximum(jnp.sqrt(jnp.sum(kkh * kkh, axis=1, keepdims=True)), 1e-12)
        for i, val in enumerate((xr[:, sl], kf[:, sl], xv[:, sl], kkh, kkh * a[:, sl], logw[:, sl], g[:, sl])):
            put(i, h, val)


def _rwprep_kernel(zr_ref, pv_ref, *rest):
    prm = rest[:8]
    outs = rest[8:8 + RW_TENSORS]

    def put(i, h, val):
        outs[i][0, h] = val

    _rw_prep_heads(zr_ref[0], pv_ref[0], prm, put)


def rwkv_prep(zr, prev, mu, w0, w_w2, a0, w_a2, w_g2, k_k, k_a, tm):
    nb, m, w = zr.shape
    vec = lambda n: pl.BlockSpec((1, n), lambda b, i: (0, 0))
    mat = lambda a: pl.BlockSpec(a.shape, lambda b, i: (0, 0))
    out_spec = pl.BlockSpec((1, RW_HEADS, tm, RW_N), lambda b, i: (b, 0, i, 0))
    out_sds = jax.ShapeDtypeStruct((nb, RW_HEADS, m, RW_N), F32)
    return pl.pallas_call(
        _rwprep_kernel,
        grid=(nb, m // tm),
        in_specs=[pl.BlockSpec((1, tm, w), lambda b, i: (b, i, 0)),
                  pl.BlockSpec((1, tm, w), lambda b, i: (b, i, 0)),
                  vec(w), vec(RW_W), mat(w_w2), vec(RW_W), mat(w_a2), mat(w_g2), vec(RW_W), vec(RW_W)],
        out_specs=[out_spec] * RW_TENSORS,
        out_shape=[out_sds] * RW_TENSORS,
        compiler_params=_params(("parallel", "parallel")),
        name="rwkv_prep",
    )(zr, prev, mu.reshape(1, w), w0.reshape(1, RW_W), w_w2, a0.reshape(1, RW_W), w_a2, w_g2,
      k_k.reshape(1, RW_W), k_a.reshape(1, RW_W))


def _split_bf16(x, terms):
    parts = []
    rem = x
    for i in range(terms):
        p = rem.astype(BF16)
        parts.append(p)
        if i + 1 < terms:
            rem = rem - p.astype(F32)
    return parts


def _mm(a, b, ta, tb, dot=_dot):
    ap = _split_bf16(a, ta)
    bp = _split_bf16(b, tb)
    n = max(ta, tb)
    acc = None
    for i, x in enumerate(ap):
        for j, y in enumerate(bp):
            if i + j < n:
                d = dot(x, y)
                acc = d if acc is None else acc + d
    return acc


RW_P_CUMSUM = 2
RW_P_INTRA = 1
RW_P_INV = 1
RW_P_STATE = 1


def _rwkv_chunk(get, rk_ref, lnw_ref, lnb_ref, st_ref, chunk):
    heads = range(RW_HEADS)
    r_h, k_h, v_h, kk_h, kka_h, lw_h, g_h = ([get(i, h) for h in heads] for i in range(RW_TENSORS))
    c2 = 2 * chunk
    row = lax.broadcasted_iota(jnp.int32, (chunk, chunk), 0)
    col = lax.broadcasted_iota(jnp.int32, (chunk, chunk), 1)
    tri = jnp.where(col <= row, 1.0, 0.0).astype(BF16)
    eye_c = jnp.where(row == col, 1.0, 0.0).astype(F32)
    row2 = lax.broadcasted_iota(jnp.int32, (c2, c2), 0)
    col2 = jnp.bitwise_and(lax.broadcasted_iota(jnp.int32, (c2, c2), 1), chunk - 1)
    mask2 = col2 < jnp.where(row2 < chunk, row2, row2 - (chunk - 1))
    eye_n = lax.broadcasted_iota(jnp.int32, (RW_N, RW_N), 0) == lax.broadcasted_iota(jnp.int32, (RW_N, RW_N), 1)
    zeros_cn = jnp.zeros((chunk, RW_N), F32)
    n_double = int(math.log2(chunk)) - 1
    cs = [_mm(tri, lw_h[h], 1, RW_P_CUMSUM) for h in heads]
    lhs, rhs, g_end, g_end_col = [], [], [], []
    for h in heads:
        g_incl = jnp.exp(cs[h])
        g_prev = jnp.exp(cs[h] - lw_h[h])
        g_inv = jnp.exp(-cs[h])
        cs_last = cs[h][chunk - 1:chunk, :]
        g_end.append(jnp.exp(cs_last))
        g_end_col.append(jnp.exp(jnp.sum(jnp.where(eye_n, jnp.broadcast_to(cs_last, (RW_N, RW_N)), 0.0),
                                         axis=1, keepdims=True)))
        lhs.append(jnp.concatenate([-kk_h[h] * g_prev, r_h[h] * g_incl], axis=0))
        rhs.append(jnp.concatenate([kka_h[h] * g_inv, k_h[h] * g_inv], axis=0))
    mx = [jnp.where(mask2, _mm(lhs[h], rhs[h], RW_P_INTRA, RW_P_INTRA, _nt_dot), 0.0) for h in heads]
    from_state = [_mm(lhs[h], st_ref[0, h], RW_P_STATE, RW_P_STATE) for h in heads]
    from_v = [_mm(mx[h], jnp.concatenate([zeros_cn, v_h[h]], axis=0), RW_P_INTRA, RW_P_INTRA)
              for h in heads]
    l_ab = [mx[h][:chunk, :chunk] for h in heads]
    tinv = [eye_c + l_ab[h] for h in heads]
    pw = [_mm(l_ab[h], l_ab[h], RW_P_INV, RW_P_INV) for h in heads]
    for _ in range(n_double - 1):
        both = [_mm(jnp.concatenate([pw[h], tinv[h]], axis=0), pw[h], RW_P_INV, RW_P_INV) for h in heads]
        tinv = [tinv[h] + both[h][chunk:] for h in heads]
        pw = [both[h][:chunk] for h in heads]
    tinv = [tinv[h] + _mm(tinv[h], pw[h], RW_P_INV, RW_P_INV) for h in heads]
    u = [_mm(tinv[h], from_state[h][:chunk] + from_v[h][:chunk], RW_P_INV, RW_P_INV) for h in heads]
    y_u = [_mm(mx[h][chunk:, :chunk], u[h], RW_P_INTRA, RW_P_INTRA) for h in heads]
    st_add = [_mm(rhs[h] * g_end[h], jnp.concatenate([u[h], v_h[h]], axis=0), RW_P_STATE, RW_P_STATE, _tn_dot)
              for h in heads]
    outs = []
    for h in heads:
        st_ref[0, h] = st_ref[0, h] * g_end_col[h] + st_add[h]
        y = from_state[h][chunk:] + from_v[h][chunk:] + y_u[h]
        mean = jnp.mean(y, axis=1, keepdims=True)
        yc = y - mean
        var = jnp.mean(yc * yc, axis=1, keepdims=True)
        yn = yc * lax.rsqrt(var + RW_GN_EPS) * lnw_ref[h:h + 1, :] + lnb_ref[h:h + 1, :]
        bonus = jnp.sum(r_h[h] * k_h[h] * rk_ref[h:h + 1, :], axis=1, keepdims=True) * v_h[h]
        outs.append((yn + bonus) * g_h[h])
    return jnp.concatenate(outs, axis=1)


def _rwkv_fused_kernel(zr0_ref, zra_ref, zrb_ref, sp_ref, *rest, chunk):
    prm = rest[:8]
    rk_ref, lnw_ref, lnb_ref, s0_ref, o_ref, st_ref, buf, carry = rest[8:]
    j = pl.program_id(1)

    def prep(zr, slot):
        prev = _shifted_rows(zr, carry)

        def put(i, h, val):
            buf[slot, i, h] = val

        _rw_prep_heads(zr, prev, prm, put)

    @pl.when(j == 0)
    def _():
        st_ref[0] = s0_ref[0]
        carry[...] = sp_ref[0]
        prep(zr0_ref[0], 0)

    for slot, nxt_ref in ((0, zra_ref), (1, zrb_ref)):
        out = _rwkv_chunk(lambda i, h: buf[slot, i, h], rk_ref, lnw_ref, lnb_ref, st_ref, chunk)
        o_ref[0, slot * chunk:(slot + 1) * chunk, :] = out.astype(BF16)
        prep(nxt_ref[0], 1 - slot)


def rwkv_fused(zr, shift_prev, mu, w0, w_w2, a0, w_a2, w_g2, k_k, k_a, r_k, ln_w, ln_b, s0_t, chunk):
    nb, t, w = zr.shape
    nc = t // chunk
    assert nc % 2 == 0
    vec = lambda n: pl.BlockSpec((1, n), lambda b, j: (0, 0))
    mat = lambda a: pl.BlockSpec(a.shape, lambda b, j: (0, 0))
    hspec = pl.BlockSpec((RW_HEADS, RW_N), lambda b, j: (0, 0))
    sspec = pl.BlockSpec((1, RW_HEADS, RW_N, RW_N), lambda b, j: (b, 0, 0, 0))
    return pl.pallas_call(
        functools.partial(_rwkv_fused_kernel, chunk=chunk),
        grid=(nb, nc // 2),
        in_specs=[pl.BlockSpec((1, chunk, w), lambda b, j: (b, 0, 0)),
                  pl.BlockSpec((1, chunk, w), lambda b, j: (b, 2 * j + 1, 0)),
                  pl.BlockSpec((1, chunk, w), lambda b, j: (b, jnp.minimum(2 * j + 2, nc - 1), 0)),
                  pl.BlockSpec((1, 1, w), lambda b, j: (b, 0, 0)),
                  vec(w), vec(RW_W), mat(w_w2), vec(RW_W), mat(w_a2), mat(w_g2), vec(RW_W), vec(RW_W),
                  hspec, hspec, hspec, sspec],
        out_specs=[pl.BlockSpec((1, 2 * chunk, RW_W), lambda b, j: (b, j, 0)), sspec],
        out_shape=[jax.ShapeDtypeStruct((nb, t, RW_W), BF16),
                   jax.ShapeDtypeStruct((nb, RW_HEADS, RW_N, RW_N), F32)],
        scratch_shapes=[pltpu.VMEM((2, RW_TENSORS, RW_HEADS, chunk, RW_N), F32), pltpu.VMEM((1, w), F32)],
        compiler_params=_params(("parallel", "arbitrary")),
        name="rwkv_fused",
    )(zr, zr, zr, shift_prev, mu.reshape(1, w), w0.reshape(1, RW_W), w_w2, a0.reshape(1, RW_W), w_a2, w_g2,
      k_k.reshape(1, RW_W), k_a.reshape(1, RW_W), r_k, ln_w.reshape(RW_HEADS, RW_N), ln_b.reshape(RW_HEADS, RW_N),
      s0_t)


def _rwkv_step_kernel(r_ref, k_ref, v_ref, kk_ref, kka_ref, lw_ref, g_ref, rk_ref, lnw_ref, lnb_ref, s0_ref,
                      o_ref, s_ref, w_scr, y_scr, *, t_new):
    for t in range(t_new):
        w_scr[t] = jnp.exp(lw_ref[t, 0])

    def value_row(vi, carry):
        s = s0_ref[0, vi]
        for t in range(t_new):
            sa = -jnp.sum(s * kk_ref[t, 0], axis=0, keepdims=True)
            s = s * w_scr[t] + sa * kka_ref[t, 0] + v_ref[t, 0, pl.ds(vi, 1), :] * k_ref[t, 0]
            y_scr[t, pl.ds(vi, 1), :] = jnp.sum(s * r_ref[t, 0], axis=0, keepdims=True)
        s_ref[0, vi] = s
        return carry

    lax.fori_loop(0, RW_N, value_row, 0)
    for t in range(t_new):
        y = y_scr[t]
        mean = jnp.mean(y, axis=0, keepdims=True)
        yc = y - mean
        var = jnp.mean(yc * yc, axis=0, keepdims=True)
        yn = yc * lax.rsqrt(var + RW_GN_EPS) * lnw_ref[0] + lnb_ref[0]
        bonus = jnp.sum(r_ref[t, 0] * k_ref[t, 0] * rk_ref[0], axis=0, keepdims=True) * v_ref[t, 0]
        o_ref[t, 0] = (yn + bonus) * g_ref[t, 0]


def rwkv_step(r, k, v, kk, kka, lw, g, r_k, ln_w, ln_b, s0):
    t_new, nh, n, nb = r.shape
    tspec = pl.BlockSpec((t_new, 1, n, nb), lambda h: (0, h, 0, 0))
    hspec = pl.BlockSpec((1, n, nb), lambda h: (h, 0, 0))
    sspec = pl.BlockSpec((1, n, n, nb), lambda h: (h, 0, 0, 0))
    return pl.pallas_call(
        functools.partial(_rwkv_step_kernel, t_new=t_new),
        grid=(nh,),
        in_specs=[tspec] * 7 + [hspec] * 3 + [sspec],
        out_specs=[tspec, sspec],
        out_shape=[jax.ShapeDtypeStruct((t_new, nh, n, nb), F32), jax.ShapeDtypeStruct((nh, n, n, nb), F32)],
        scratch_shapes=[pltpu.VMEM((t_new, n, nb), F32), pltpu.VMEM((t_new, n, nb), F32)],
        compiler_params=_params(("parallel",)),
        name="rwkv_step",
    )(r, k, v, kk, kka, lw, g, r_k, ln_w, ln_b, s0)


def _retention_kernel(lg_ref, qk_ref, v_ref, g_ref, cos_ref, sin_ref, s0_ref, o_ref, s_ref, *, lb, l_true):
    c = pl.program_id(1)

    @pl.when(c == 0)
    def _():
        s_ref[0] = s0_ref[0]

    lp = max(lb, 16)
    cos = cos_ref[0]
    sin = sin_ref[0]
    half = RET_DK // 2
    qk_w = RET_HEADS * RET_DK

    def rope(x):
        x1, x2 = x[:, :half], x[:, half:]
        return jnp.concatenate([x1 * cos - x2 * sin, x1 * sin + x2 * cos], axis=1)

    def rows(x):
        if lp == lb:
            return x
        return jnp.concatenate([x, jnp.zeros((lp - lb, x.shape[1]), x.dtype)], axis=0)

    row = lax.broadcasted_iota(jnp.int32, (lp, lp), 0)
    col = lax.broadcasted_iota(jnp.int32, (lp, lp), 1)
    diff = (row - col).astype(F32)
    idx = lax.broadcasted_iota(jnp.int32, (lp, 1), 0).astype(F32)
    heads = range(RET_HEADS)
    qm, km, kdm, vm, dmask, row_dec = [], [], [], [], [], []
    for h in heads:
        lg = lg_ref[h]
        q = rows(rope(qk_ref[0, :, h * RET_DK:(h + 1) * RET_DK]))
        k = rows(rope(qk_ref[0, :, qk_w + h * RET_DK:qk_w + (h + 1) * RET_DK]) * (RET_DK ** -0.5))
        qm.append(q.astype(BF16))
        km.append(k.astype(BF16))
        kdm.append((k * jnp.exp((l_true - 1.0 - idx) * lg)).astype(BF16))
        vm.append(rows(v_ref[0, :, h * RET_DV:(h + 1) * RET_DV]).astype(BF16))
        dmask.append(jnp.where(diff >= 0, jnp.exp(jnp.maximum(diff, 0.0) * lg), 0.0))
        row_dec.append(jnp.exp((idx + 1.0) * lg))
    sc = [(_nt_dot(qm[h], km[h]) * dmask[h]).astype(BF16) for h in heads]
    cross = [_dot(qm[h], s_ref[0, h].astype(BF16)) * row_dec[h] for h in heads]
    s_add = [_tn_dot(kdm[h], vm[h]) for h in heads]
    inner = [_dot(sc[h], vm[h]) for h in heads]
    outs = []
    for h in heads:
        s_dec = jnp.exp(jnp.zeros((1, RET_DV), F32) + l_true * lg_ref[h])
        s_ref[0, h] = s_ref[0, h] * s_dec + s_add[h]
        o = (inner[h] + cross[h])[:lb]
        o = o * lax.rsqrt(jnp.mean(o * o, axis=1, keepdims=True) + NORM_EPS)
        gv = g_ref[0, :, h * RET_DV:(h + 1) * RET_DV]
        outs.append(o * (gv * jax.nn.sigmoid(gv)))
    o_ref[0] = jnp.concatenate(outs, axis=1).astype(BF16)


def retention(z, cos, sin, lg, s0, lb, l_true):
    nb, m, _ = z.shape
    vw = RET_HEADS * RET_DV
    assert 2 * RET_HEADS * RET_DK == vw
    sspec = pl.BlockSpec((1, RET_HEADS, RET_DK, RET_DV), lambda b, c: (b, 0, 0, 0))
    return pl.pallas_call(
        functools.partial(_retention_kernel, lb=lb, l_true=float(l_true)),
        grid=(nb, m // lb),
        in_specs=[pl.BlockSpec(memory_space=pltpu.SMEM),
                  pl.BlockSpec((1, lb, vw), lambda b, c: (b, c, 0)),
                  pl.BlockSpec((1, lb, vw), lambda b, c: (b, c, 1)),
                  pl.BlockSpec((1, lb, vw), lambda b, c: (b, c, 2)),
                  pl.BlockSpec((1, lb, RET_DK // 2), lambda b, c: (0, c, 0)),
                  pl.BlockSpec((1, lb, RET_DK // 2), lambda b, c: (0, c, 0)),
                  sspec],
        out_specs=[pl.BlockSpec((1, lb, vw), lambda b, c: (b, c, 0)), sspec],
        out_shape=[jax.ShapeDtypeStruct((nb, m, vw), BF16),
                   jax.ShapeDtypeStruct((nb, RET_HEADS, RET_DK, RET_DV), F32)],
        compiler_params=_params(("parallel", "arbitrary")),
        name="retention",
    )(lg, z, z, z, cos, sin, s0)


def _rope_tables(pos, half):
    inv = ROPE_BASE ** (-jnp.arange(half, dtype=F32) / half)
    ang = pos.astype(F32)[:, None] * inv[None, :]
    return jnp.cos(ang), jnp.sin(ang)


def _mla_tables(pos):
    cos, sin = _rope_tables(pos, MLA_ROPE // 2)
    return jnp.concatenate([cos, cos], axis=1), jnp.concatenate([-sin, sin], axis=1)


def _even_layer(x, mods, pos_tabs, prm, past, tm):
    (w_in_p, g_mix, g_kv, wuk_t, wuk_r, wuv_t, wuv_all, mu, w0, w_w2, a0, w_a2, w_g2, k_k, k_a, r_k, ln_w, ln_b,
     w_out_mla, w_out_rw) = prm
    sh1, sc1, gt1 = mods
    cf, sf = pos_tabs
    nb, m, _ = x.shape
    zr, zq, zkv = norm_mod_matmul_split(x, g_mix, sh1, sc1, w_in_p, (RW_SHIFT_W, MLA_HEADS * (MLA_NOPE + MLA_ROPE), MLA_QK), tm)
    lat, kr, kcat, latt = kv_prep(zkv, g_kv, cf, sf, tm)
    if past is None:
        qt = q_prep_t(zq, wuk_r, cf, sf, min(MLA_PROMPT_TQ, m))
        mla_out = mla_prompt(qt, kcat, latt, wuv_t, 512)
        mla_transposed = True
        s0_t = jnp.zeros((nb, RW_HEADS, RW_N, RW_N), F32)
        rw_out, s_t = rwkv_fused(zr, jnp.zeros((nb, 1, RW_SHIFT_W), F32), mu, w0, w_w2, a0, w_a2, w_g2, k_k, k_a,
                                 r_k, ln_w, ln_b, s0_t, RW_CHUNK)
        s_new = jnp.swapaxes(s_t, -1, -2)
        shift_new = zr[:, -1]
    else:
        cache_lat, cache_kr, layer, page_table, s0, shift_prev, t_new = past
        nbs = m // t_new
        mla_transposed = False
        qcat = q_prep(zq, wuk_t, cf, sf, tm)
        q_s = qcat.reshape(MLA_HEADS, nbs, t_new, MLA_QK).transpose(1, 2, 0, 3).reshape(nbs, t_new * MLA_HEADS, MLA_QK)
        kn_s = jnp.concatenate([lat, kr], axis=-1).reshape(nbs, t_new, MLA_QK)
        mla_out = mla_sample(q_s, kn_s, wuv_all, cache_lat, cache_kr, layer, page_table).reshape(1, m, MLA_HEADS * MLA_V)
        zr_b = zr.reshape(nbs, t_new, RW_SHIFT_W)
        prev = jnp.concatenate([shift_prev[:, None, :], zr_b[:, :-1]], axis=1).reshape(1, m, RW_SHIFT_W)
        tens = rwkv_prep(zr, prev, mu, w0, w_w2, a0, w_a2, w_g2, k_k, k_a, min(tm, 256))
        tens = [u.reshape(RW_HEADS, nbs, t_new, RW_N).transpose(2, 0, 3, 1) for u in tens]
        lanes = lambda p: jnp.broadcast_to(p.reshape(RW_HEADS, RW_N, 1), (RW_HEADS, RW_N, nbs))
        rw_l, s_l = rwkv_step(*tens, lanes(r_k), lanes(ln_w), lanes(ln_b), jnp.transpose(s0, (1, 2, 3, 0)))
        rw_out = rw_l.transpose(3, 0, 1, 2).reshape(1, m, RW_W).astype(BF16)
        s_new = jnp.transpose(s_l, (3, 0, 1, 2))
        shift_new = zr_b[:, -1]
    x_new = matmul_gate_res([mla_out, rw_out], [w_out_mla, w_out_rw], x, gt1, min(2 * tm, m), (mla_transposed, False))
    return x_new, (lat, kr, s_new, shift_new)


def _odd_layer(x, mods, ret_tabs, prm, s0, t_new, tm):
    w_in, g_mix, w_out, lg = prm
    sh1, sc1, gt1 = mods
    cos, sin = ret_tabs
    nb, m, _ = x.shape
    z = norm_mod_matmul(x, g_mix, sh1, sc1, w_in, min(2 * tm, m), 2048)
    if s0 is None:
        s0 = jnp.zeros((nb, RET_HEADS, RET_DK, RET_DV), F32)
        lb = min(RET_BLOCK, m)
        o, s_new = retention(z, cos, sin, lg, s0, lb, lb)
    else:
        nbs = m // t_new
        lpad = 8
        z_b = jnp.pad(z.reshape(nbs, t_new, -1), ((0, 0), (0, lpad - t_new), (0, 0)))
        o, s_new = retention(z_b, cos, sin, lg, s0, lpad, t_new)
        o = o[:, :t_new].reshape(1, m, RET_HEADS * RET_DV)
    x_new = matmul_gate_res([o], [w_out], x, gt1, min(2 * tm, m))
    return x_new, s_new


def kernel(x_prompt, x_sample, c_prompt, c_sample, cache_kv_latent, cache_k_rope, page_table, state_rwkv, state_rwkv_shift, state_ret, w_ada, b_ada, g_norm_mix, g_norm_mlp, g_final, w_in_even, g_kv, w_uk, w_uv, rw_mu, rw_w0, rw_w2, rw_a0, rw_a2, rw_g2, rw_k_k, rw_k_a, rw_r_k, rw_ln_w, rw_ln_b, w_out_even, w_in_odd, w_out_odd, w_ff1, w_ff2):
    nbp, t_p, d = x_prompt.shape
    nbs, t_s, _ = x_sample.shape
    depth = w_ada.shape[0]
    past_len = page_table.shape[1] * PAGE_SIZE
    m_s = nbs * t_s
    tm_p = min(512, t_p)
    tm_s = m_s

    c_all = jnp.concatenate([jnp.repeat(c_sample, t_s, axis=0), c_prompt], axis=0)
    c_all = jnp.pad(c_all, ((0, -c_all.shape[0] % 16), (0, 0)))
    mods_all = ada_proj(c_all, w_ada, b_ada)

    def group_mods(l):
        mp = mods_all[l, m_s:m_s + nbp].reshape(nbp, 1, 6, d)
        return [mp[:, :, i] for i in range(6)], [(mods_all, l, i) for i in range(6)]

    pos_p = jnp.arange(t_p)
    pos_s = past_len + jnp.arange(t_s)
    cf_p, sf_p = _mla_tables(pos_p)
    cf_s, sf_s = _mla_tables(pos_s)
    mla_tabs_p = (cf_p[None], sf_p[None])
    mla_tabs_s = (jnp.tile(cf_s, (nbs, 1))[None], jnp.tile(sf_s, (nbs, 1))[None])
    cr_p, sr_p = _rope_tables(pos_p, RET_DK // 2)
    cr_s, sr_s = _rope_tables(pos_s, RET_DK // 2)
    ret_tabs_p = (cr_p[None], sr_p[None])
    ret_tabs_s = (jnp.pad(cr_s, ((0, 8 - t_s), (0, 0)))[None], jnp.pad(sr_s, ((0, 8 - t_s), (0, 0)))[None])
    lg = jnp.log(1 - 2.0 ** (-5.0 - jnp.arange(RET_HEADS, dtype=F32)))

    xp = x_prompt
    xs = x_sample.reshape(1, m_s, d)
    lat_p, kr_p, rw_p, sh_p, ret_p = [], [], [], [], []
    lat_s, kr_s, rw_s, sh_s, ret_s = [], [], [], [], []
    q_w = MLA_HEADS * (MLA_NOPE + MLA_ROPE)
    for l in range(depth):
        (sh1p, sc1p, gt1p, sh2p, sc2p, gt2p), (sh1s, sc1s, gt1s, sh2s, sc2s, gt2s) = group_mods(l)
        i = l // 2
        if l % 2 == 0:
            w_in = w_in_even[i]
            wq = w_in[:, :q_w].reshape(d, MLA_HEADS, MLA_NOPE + MLA_ROPE)
            w_in_p = jnp.concatenate([w_in[:, q_w + MLA_QK:],
                                      wq[:, :, :MLA_NOPE].reshape(d, -1), wq[:, :, MLA_NOPE:].reshape(d, -1),
                                      w_in[:, q_w:q_w + MLA_QK]], axis=1).astype(BF16)
            wuv = w_uv[i]
            mla_w = MLA_HEADS * MLA_V
            prm = (w_in_p, g_norm_mix[l], g_kv[i], jnp.transpose(w_uk[i], (1, 2, 0)).astype(BF16),
                   jnp.transpose(w_uk[i], (1, 0, 2)).astype(BF16), jnp.transpose(wuv, (1, 2, 0)).astype(BF16), wuv.reshape(KV_RANK, mla_w).astype(BF16),
                   rw_mu[i], rw_w0[i], rw_w2[i].astype(BF16), rw_a0[i], rw_a2[i].astype(BF16), rw_g2[i].astype(BF16),
                   rw_k_k[i], rw_k_a[i], rw_r_k[i], rw_ln_w[i], rw_ln_b[i],
                   w_out_even[i, :mla_w].astype(BF16), w_out_even[i, mla_w:].astype(BF16))
            xp, (la, kr, st, sh) = _even_layer(xp, (sh1p, sc1p, gt1p), mla_tabs_p, prm, None, tm_p)
            lat_p.append(la); kr_p.append(kr); rw_p.append(st); sh_p.append(sh)
            cache_kr_t = jnp.swapaxes(cache_k_rope, 2, 3)
            past = (cache_kv_latent, cache_kr_t, i, page_table, state_rwkv[i], state_rwkv_shift[i], t_s)
            xs, (la, kr, st, sh) = _even_layer(xs, (sh1s, sc1s, gt1s), mla_tabs_s, prm, past, tm_s)
            lat_s.append(la.reshape(nbs, t_s, KV_RANK)); kr_s.append(kr.reshape(nbs, t_s, MLA_ROPE))
            rw_s.append(st); sh_s.append(sh)
        else:
            prm = (w_in_odd[i].astype(BF16), g_norm_mix[l], w_out_odd[i].astype(BF16), lg)
            xp, st = _odd_layer(xp, (sh1p, sc1p, gt1p), ret_tabs_p, prm, None, t_s, tm_p)
            ret_p.append(st)
            xs, st = _odd_layer(xs, (sh1s, sc1s, gt1s), ret_tabs_s, prm, state_ret[i], t_s, tm_s)
            ret_s.append(st)
        final = l == depth - 1
        w1 = w_ff1[l].astype(BF16)
        w2 = w_ff2[l].astype(BF16)
        xp = mlp_block(xp, g_norm_mlp[l], sh2p, sc2p, gt2p, w1, w2, g_final, final, min(2 * tm_p, t_p), 1024)
        xs = mlp_block(xs, g_norm_mlp[l], sh2s, sc2s, gt2s, w1, w2, g_final, final, tm_s, 1024)
    return (xp, xs.reshape(nbs, t_s, d),
            jnp.stack(lat_p), jnp.stack(kr_p), jnp.stack(rw_p), jnp.stack(sh_p), jnp.stack(ret_p),
            jnp.stack(lat_s), jnp.stack(kr_s), jnp.stack(rw_s), jnp.stack(sh_s), jnp.stack(ret_s))
```

```python
import functools
import math

import jax
import jax.numpy as jnp
from jax import lax
from jax.experimental import pallas as pl
from jax.experimental.pallas import tpu as pltpu

F32 = jnp.float32
BF16 = jnp.bfloat16
HIGHEST = lax.Precision.HIGHEST

D_MODEL = 1024
PAGE_SIZE = 128
MLA_HEADS = 8
MLA_NOPE = 64
MLA_ROPE = 32
MLA_V = 64
KV_RANK = 256
MLA_QK = KV_RANK + MLA_ROPE
MLA_SCALE = (MLA_NOPE + MLA_ROPE) ** -0.5
MLA_QSCALE = MLA_SCALE * math.log2(math.e)
RW_HEADS = 8
RW_N = 64
RW_W = RW_HEADS * RW_N
RW_DECAY_LORA = 64
RW_A_LORA = 64
RW_G_LORA = 128
RW_SHIFT_W = 3 * RW_W + RW_DECAY_LORA + RW_A_LORA + RW_G_LORA
RW_GN_EPS = 64e-5
RW_CHUNK = 64
RET_HEADS = 4
RET_DK = 256
RET_DV = 512
RET_BLOCK = 256
D_FF = 4 * D_MODEL
ROPE_BASE = 10000.0
NORM_EPS = 1e-6
MIB = 1024 * 1024


def _params(sem, vmem_mib=48):
    return pltpu.CompilerParams(dimension_semantics=sem, vmem_limit_bytes=vmem_mib * MIB)


def _rms(x, g):
    return x * lax.rsqrt(jnp.mean(x * x, axis=-1, keepdims=True) + NORM_EPS) * g


def _nt_dot(a, b, precision=None):
    return lax.dot_general(a, b, (((1,), (1,)), ((), ())), precision=precision, preferred_element_type=F32)


def _tn_dot(a, b, precision=None):
    return lax.dot_general(a, b, (((0,), (0,)), ((), ())), precision=precision, preferred_element_type=F32)


def _dot(a, b, precision=None):
    return jnp.dot(a, b, precision=precision, preferred_element_type=F32)


def _mod_arg(mod):
    return mod[0] if isinstance(mod, tuple) else mod


def _mod_spec(mod, tm, nmid):
    if isinstance(mod, tuple):
        _, layer, which = mod
        if nmid == 2:
            return pl.BlockSpec((1, tm, D_MODEL), lambda b, m, j: (layer, m, which))
        return pl.BlockSpec((1, tm, D_MODEL), lambda b, m: (layer, m, which))
    if mod.shape[1] == 1:
        if nmid == 2:
            return pl.BlockSpec((1, 1, mod.shape[2]), lambda b, m, j: (b, 0, 0))
        return pl.BlockSpec((1, 1, mod.shape[2]), lambda b, m: (b, 0, 0))
    if nmid == 2:
        return pl.BlockSpec((1, tm, mod.shape[2]), lambda b, m, j: (b, m, 0))
    return pl.BlockSpec((1, tm, mod.shape[2]), lambda b, m: (b, m, 0))


def _ada_kernel(c_ref, w_ref, b_ref, o_ref):
    o_ref[0] = _dot(c_ref[...].astype(BF16), w_ref[0].astype(BF16)) + b_ref[0]


def ada_proj(c, w_ada, b_ada):
    nl, d, n = w_ada.shape
    r = c.shape[0]
    tn = 1536
    return pl.pallas_call(
        _ada_kernel,
        grid=(nl, n // tn),
        in_specs=[pl.BlockSpec((r, d), lambda l, j: (0, 0)),
                  pl.BlockSpec((1, d, tn), lambda l, j: (l, 0, j)),
                  pl.BlockSpec((1, 1, tn), lambda l, j: (l, 0, j))],
        out_specs=pl.BlockSpec((1, r, tn), lambda l, j: (l, 0, j)),
        out_shape=jax.ShapeDtypeStruct((nl, r, n), F32),
        compiler_params=_params(("parallel", "parallel")),
        name="ada_proj",
    )(c, w_ada, b_ada.reshape(nl, 1, n))


def _nmm_split_kernel(x_ref, g_ref, sh_ref, sc_ref, w_ref, *o_refs, splits):
    h = (_rms(x_ref[0], g_ref[...]) * (1.0 + sc_ref[0]) + sh_ref[0]).astype(BF16)
    off = 0
    for o_ref, n in zip(o_refs, splits):
        o_ref[0] = _dot(h, w_ref[:, off:off + n])
        off += n


def norm_mod_matmul_split(x, g, shift, scale, w, splits, tm):
    nb, m, d = x.shape
    n = w.shape[1]
    return pl.pallas_call(
        functools.partial(_nmm_split_kernel, splits=splits),
        grid=(nb, m // tm),
        in_specs=[pl.BlockSpec((1, tm, d), lambda b, i: (b, i, 0)),
                  pl.BlockSpec((1, d), lambda b, i: (0, 0)),
                  _mod_spec(shift, tm, 1), _mod_spec(scale, tm, 1),
                  pl.BlockSpec((d, n), lambda b, i: (0, 0))],
        out_specs=[pl.BlockSpec((1, tm, s), lambda b, i: (b, i, 0)) for s in splits],
        out_shape=[jax.ShapeDtypeStruct((nb, m, s), F32) for s in splits],
        compiler_params=_params(("parallel", "parallel")),
        name="norm_mod_matmul_split",
    )(x, g.reshape(1, d), _mod_arg(shift), _mod_arg(scale), w)


def _nmm_kernel(x_ref, g_ref, sh_ref, sc_ref, w_ref, o_ref):
    h = (_rms(x_ref[0], g_ref[...]) * (1.0 + sc_ref[0]) + sh_ref[0]).astype(BF16)
    o_ref[0] = _dot(h, w_ref[...])


def norm_mod_matmul(x, g, shift, scale, w, tm, tn):
    nb, m, d = x.shape
    n = w.shape[1]
    return pl.pallas_call(
        _nmm_kernel,
        grid=(nb, m // tm, n // tn),
        in_specs=[pl.BlockSpec((1, tm, d), lambda b, i, j: (b, i, 0)),
                  pl.BlockSpec((1, d), lambda b, i, j: (0, 0)),
                  _mod_spec(shift, tm, 2), _mod_spec(scale, tm, 2),
                  pl.BlockSpec((d, tn), lambda b, i, j: (0, j))],
        out_specs=pl.BlockSpec((1, tm, tn), lambda b, i, j: (b, i, j)),
        out_shape=jax.ShapeDtypeStruct((nb, m, n), F32),
        compiler_params=_params(("parallel", "parallel", "arbitrary")),
        name="norm_mod_matmul",
    )(x, g.reshape(1, d), _mod_arg(shift), _mod_arg(scale), w)


def _mgr_kernel(*refs, n_pairs, transposed):
    a_refs = refs[:n_pairs]
    w_refs = refs[n_pairs:2 * n_pairs]
    res_ref, gt_ref, o_ref = refs[2 * n_pairs:]
    acc = None
    for a_ref, w_ref, tr in zip(a_refs, w_refs, transposed):
        d = (_tn_dot if tr else _dot)(a_ref[0], w_ref[...])
        acc = d if acc is None else acc + d
    o_ref[0] = res_ref[0] + gt_ref[0] * acc


def matmul_gate_res(a_list, w_list, res, gate, tm, transposed=None):
    nb, m, d = res.shape
    n_pairs = len(a_list)
    transposed = tuple(transposed or (False,) * n_pairs)
    in_specs = [pl.BlockSpec((1, a.shape[1], tm), lambda b, i: (b, 0, i)) if tr else
                pl.BlockSpec((1, tm, a.shape[2]), lambda b, i: (b, i, 0)) for a, tr in zip(a_list, transposed)]
    in_specs += [pl.BlockSpec(w.shape, lambda b, i: (0, 0)) for w in w_list]
    in_specs += [pl.BlockSpec((1, tm, d), lambda b, i: (b, i, 0)), _mod_spec(gate, tm, 1)]
    return pl.pallas_call(
        functools.partial(_mgr_kernel, n_pairs=n_pairs, transposed=transposed),
        grid=(nb, m // tm),
        in_specs=in_specs,
        out_specs=pl.BlockSpec((1, tm, d), lambda b, i: (b, i, 0)),
        out_shape=jax.ShapeDtypeStruct((nb, m, d), F32),
        compiler_params=_params(("parallel", "parallel")),
        name="matmul_gate_res",
    )(*a_list, *w_list, res, _mod_arg(gate))


def _mlp_kernel(x_ref, g_ref, sh_ref, sc_ref, gt_ref, w1_ref, w2_ref, gf_ref, o_ref, h_scr, acc_scr, *, final):
    f = pl.program_id(2)

    @pl.when(f == 0)
    def _():
        h_scr[...] = (_rms(x_ref[0], g_ref[...]) * (1.0 + sc_ref[0]) + sh_ref[0]).astype(BF16)
        acc_scr[...] = jnp.zeros_like(acc_scr)

    a = _dot(h_scr[...], w1_ref[...])
    a = jnp.square(jnp.maximum(a, 0.0)).astype(BF16)
    acc_scr[...] += _dot(a, w2_ref[...])

    @pl.when(f == pl.num_programs(2) - 1)
    def _():
        y = x_ref[0] + gt_ref[0] * acc_scr[...]
        if final:
            y = _rms(y, gf_ref[...])
        o_ref[0] = y


def mlp_block(x, g, shift, scale, gate, w1, w2, g_final, final, tm, tf):
    nb, m, d = x.shape
    dff = w1.shape[1]
    return pl.pallas_call(
        functools.partial(_mlp_kernel, final=final),
        grid=(nb, m // tm, dff // tf),
        in_specs=[pl.BlockSpec((1, tm, d), lambda b, i, f: (b, i, 0)),
                  pl.BlockSpec((1, d), lambda b, i, f: (0, 0)),
                  _mod_spec(shift, tm, 2), _mod_spec(scale, tm, 2), _mod_spec(gate, tm, 2),
                  pl.BlockSpec((d, tf), lambda b, i, f: (0, f)),
                  pl.BlockSpec((tf, d), lambda b, i, f: (f, 0)),
                  pl.BlockSpec((1, d), lambda b, i, f: (0, 0))],
        out_specs=pl.BlockSpec((1, tm, d), lambda b, i, f: (b, i, 0)),
        out_shape=jax.ShapeDtypeStruct((nb, m, d), F32),
        scratch_shapes=[pltpu.VMEM((tm, d), BF16), pltpu.VMEM((tm, d), F32)],
        compiler_params=_params(("parallel", "parallel", "arbitrary"), 56),
        name="mlp_block",
    )(x, g.reshape(1, d), _mod_arg(shift), _mod_arg(scale), _mod_arg(gate), w1, w2, g_final.reshape(1, d))


def _rope32(x, cf, sf):
    half = MLA_ROPE // 2
    sw = jnp.concatenate([x[:, half:], x[:, :half]], axis=1)
    return x * cf + sw * sf


def _eye_bf16(n):
    return jnp.where(lax.broadcasted_iota(jnp.int32, (n, n), 0) == lax.broadcasted_iota(jnp.int32, (n, n), 1),
                     1.0, 0.0).astype(BF16)


def _kvprep_kernel(zkv_ref, g_ref, cf_ref, sf_ref, lat_ref, kr_ref, kcat_ref, latt_ref):
    z = zkv_ref[0]
    lat = _rms(z[:, :KV_RANK], g_ref[...])
    kr = _rope32(z[:, KV_RANK:], cf_ref[0], sf_ref[0])
    lat_ref[0] = lat
    kr_ref[0] = kr
    lat_b = lat.astype(BF16)
    kcat_ref[0, :, :KV_RANK] = lat_b
    kcat_ref[0, :, KV_RANK:] = kr.astype(BF16)
    latt_ref[0] = _nt_dot(_eye_bf16(KV_RANK), lat_b).astype(BF16)


def kv_prep(zkv, g_kv, cf, sf, tm):
    nb, m, _ = zkv.shape
    return pl.pallas_call(
        _kvprep_kernel,
        grid=(nb, m // tm),
        in_specs=[pl.BlockSpec((1, tm, MLA_QK), lambda b, i: (b, i, 0)),
                  pl.BlockSpec((1, KV_RANK), lambda b, i: (0, 0)),
                  pl.BlockSpec((1, tm, MLA_ROPE), lambda b, i: (0, i, 0)),
                  pl.BlockSpec((1, tm, MLA_ROPE), lambda b, i: (0, i, 0))],
        out_specs=[pl.BlockSpec((1, tm, KV_RANK), lambda b, i: (b, i, 0)),
                   pl.BlockSpec((1, tm, MLA_ROPE), lambda b, i: (b, i, 0)),
                   pl.BlockSpec((1, tm, MLA_QK), lambda b, i: (b, i, 0)),
                   pl.BlockSpec((1, KV_RANK, tm), lambda b, i: (b, 0, i))],
        out_shape=[jax.ShapeDtypeStruct((nb, m, KV_RANK), F32),
                   jax.ShapeDtypeStruct((nb, m, MLA_ROPE), F32),
                   jax.ShapeDtypeStruct((nb, m, MLA_QK), BF16),
                   jax.ShapeDtypeStruct((nb, KV_RANK, m), BF16)],
        compiler_params=_params(("parallel", "parallel")),
        name="kv_prep",
    )(zkv, g_kv.reshape(1, KV_RANK), cf, sf)


def _qprep_kernel(zq_ref, wuk_ref, cf_ref, sf_ref, o_ref):
    z = zq_ref[0]
    cf = cf_ref[0]
    sf = sf_ref[0]
    nope_w = MLA_HEADS * MLA_NOPE
    for h in range(MLA_HEADS):
        qn = z[:, h * MLA_NOPE:(h + 1) * MLA_NOPE].astype(BF16)
        ql = _dot(qn, wuk_ref[h]) * MLA_QSCALE
        qr = _rope32(z[:, nope_w + h * MLA_ROPE:nope_w + (h + 1) * MLA_ROPE], cf, sf) * MLA_QSCALE
        o_ref[0, h, :, :KV_RANK] = ql.astype(BF16)
        o_ref[0, h, :, KV_RANK:] = qr.astype(BF16)


def _qprep_t_kernel(zq_ref, wuk_ref, cf_ref, sf_ref, o_ref, *, tq):
    z = zq_ref[0]
    cf = cf_ref[0]
    sf = sf_ref[0]
    nope_w = MLA_HEADS * MLA_NOPE
    eye = _eye_bf16(MLA_ROPE)
    for h in range(MLA_HEADS):
        qn = z[:, h * MLA_NOPE:(h + 1) * MLA_NOPE].astype(BF16)
        ql_t = _nt_dot(wuk_ref[h], qn) * MLA_QSCALE
        qr = _rope32(z[:, nope_w + h * MLA_ROPE:nope_w + (h + 1) * MLA_ROPE], cf, sf) * MLA_QSCALE
        qr_t = _nt_dot(eye, qr.astype(BF16))
        o_ref[0, 0, :KV_RANK, h * tq:(h + 1) * tq] = ql_t.astype(BF16)
        o_ref[0, 0, KV_RANK:, h * tq:(h + 1) * tq] = qr_t.astype(BF16)


def q_prep_t(zq, wuk_r, cf, sf, tq):
    nb, m, w = zq.shape
    return pl.pallas_call(
        functools.partial(_qprep_t_kernel, tq=tq),
        grid=(nb, m // tq),
        in_specs=[pl.BlockSpec((1, tq, w), lambda b, i: (b, i, 0)),
                  pl.BlockSpec((MLA_HEADS, KV_RANK, MLA_NOPE), lambda b, i: (0, 0, 0)),
                  pl.BlockSpec((1, tq, MLA_ROPE), lambda b, i: (0, i, 0)),
                  pl.BlockSpec((1, tq, MLA_ROPE), lambda b, i: (0, i, 0))],
        out_specs=pl.BlockSpec((1, 1, MLA_QK, MLA_HEADS * tq), lambda b, i: (b, i, 0, 0)),
        out_shape=jax.ShapeDtypeStruct((nb, m // tq, MLA_QK, MLA_HEADS * tq), BF16),
        compiler_params=_params(("parallel", "parallel")),
        name="q_prep_t",
    )(zq, wuk_r, cf, sf)


def q_prep(zq, wuk_t, cf, sf, tm):
    nb, m, w = zq.shape
    return pl.pallas_call(
        _qprep_kernel,
        grid=(nb, m // tm),
        in_specs=[pl.BlockSpec((1, tm, w), lambda b, i: (b, i, 0)),
                  pl.BlockSpec((MLA_HEADS, MLA_NOPE, KV_RANK), lambda b, i: (0, 0, 0)),
                  pl.BlockSpec((1, tm, MLA_ROPE), lambda b, i: (0, i, 0)),
                  pl.BlockSpec((1, tm, MLA_ROPE), lambda b, i: (0, i, 0))],
        out_specs=pl.BlockSpec((1, MLA_HEADS, tm, MLA_QK), lambda b, i: (b, 0, i, 0)),
        out_shape=jax.ShapeDtypeStruct((nb, MLA_HEADS, m, MLA_QK), BF16),
        compiler_params=_params(("parallel", "parallel")),
        name="q_prep",
    )(zq, wuk_t, cf, sf)


MLA_PROMPT_COL_WIDTH = 512
MLA_PROMPT_TQ = 512


def _mla_prompt_kernel(qi_ref, ki_ref, qt_ref, k_ref, latt_ref, wuvt_ref, o_ref, m_scr, l_scr, acc_scr, *, tq, tk):
    step = pl.program_id(1)
    qi = qi_ref[step]
    ki = ki_ref[step]
    last_k = (qi * tq + (tq - 1)) // tk

    @pl.when(ki == 0)
    def _():
        m_scr[...] = jnp.full_like(m_scr, -jnp.inf)
        l_scr[...] = jnp.zeros_like(l_scr)
        acc_scr[...] = jnp.zeros_like(acc_scr)

    def update(masked):
        rows = MLA_HEADS * tq
        cw = min(MLA_PROMPT_COL_WIDTH, rows)
        groups = [slice(g * cw, (g + 1) * cw) for g in range(rows // cw)]
        st_next = _dot(k_ref[0], qt_ref[0, 0, :, groups[0]])
        for g, cs in enumerate(groups):
            st = st_next
            if masked:
                kpos = ki * tk + lax.broadcasted_iota(jnp.int32, st.shape, 0)
                col = cs.start + lax.broadcasted_iota(jnp.int32, st.shape, 1)
                qpos = qi * tq + jnp.bitwise_and(col, tq - 1)
                st = jnp.where(kpos <= qpos, st, -jnp.inf)
            m_prev = m_scr[:, cs]
            m_new = jnp.maximum(m_prev, jnp.max(st, axis=0, keepdims=True))
            alpha = jnp.exp2(m_prev - m_new)
            pt = jnp.exp2(st - m_new)
            l_scr[:, cs] = alpha * l_scr[:, cs] + jnp.sum(pt, axis=0, keepdims=True)
            m_scr[:, cs] = m_new
            if g + 1 < len(groups):
                st_next = _dot(k_ref[0], qt_ref[0, 0, :, groups[g + 1]])
            acc_scr[:, cs] = alpha * acc_scr[:, cs] + _dot(latt_ref[0], pt.astype(BF16))

    needs_mask = ki * tk + (tk - 1) > qi * tq

    @pl.when(needs_mask)
    def _():
        update(True)

    @pl.when(jnp.logical_not(needs_mask))
    def _():
        update(False)

    @pl.when(ki == last_k)
    def _():
        ot = (acc_scr[...] / l_scr[...]).astype(BF16)
        for h in range(MLA_HEADS):
            o_ref[0, h * MLA_V:(h + 1) * MLA_V, :] = _dot(wuvt_ref[h], ot[:, h * tq:(h + 1) * tq]).astype(BF16)


def mla_prompt(qt, kcat, latt, wuv_t, tk):
    nb, nq, _, rows = qt.shape
    tq = rows // MLA_HEADS
    t = nq * tq
    tk = min(tk, t)
    pairs =[(i, j) for i in range(nq) for j in range((i * tq + tq - 1) // tk + 1)]
    qi_tab = jnp.asarray([p[0] for p in pairs], jnp.int32)
    ki_tab = jnp.asarray([p[1] for p in pairs], jnp.int32)
    grid_spec = pltpu.PrefetchScalarGridSpec(
        num_scalar_prefetch=2,
        grid=(nb, len(pairs)),
        in_specs=[pl.BlockSpec((1, 1, MLA_QK, rows), lambda b, s, qi, ki: (b, qi[s], 0, 0)),
                  pl.BlockSpec((1, tk, MLA_QK), lambda b, s, qi, ki: (b, ki[s], 0)),
                  pl.BlockSpec((1, KV_RANK, tk), lambda b, s, qi, ki: (b, 0, ki[s])),
                  pl.BlockSpec((MLA_HEADS, MLA_V, KV_RANK), lambda b, s, qi, ki: (0, 0, 0))],
        out_specs=pl.BlockSpec((1, MLA_HEADS * MLA_V, tq), lambda b, s, qi, ki: (b, 0, qi[s])),
        scratch_shapes=[pltpu.VMEM((1, rows), F32), pltpu.VMEM((1, rows), F32), pltpu.VMEM((KV_RANK, rows), F32)],
    )
    return pl.pallas_call(
        functools.partial(_mla_prompt_kernel, tq=tq, tk=tk),
        grid_spec=grid_spec,
        out_shape=jax.ShapeDtypeStruct((nb, MLA_HEADS * MLA_V, t), BF16),
        compiler_params=_params(("parallel", "arbitrary")),
        name="mla_prompt",
    )(qi_tab, ki_tab, qt, kcat, latt, wuv_t)


def _mla_sample_kernel(pt_ref, q_ref, kn_ref, wuv_ref, lat_hbm, kr_hbm, o_ref, kl_buf, kp_buf, sem,
                       *, layer, n_pages, n_pg, n_grp, n_slots, t_new):
    b = pl.program_id(0)
    rows = t_new * MLA_HEADS
    n_chunks = n_pages // n_pg
    per = n_pg // n_grp

    def page_copies(bb, c, slot):
        cps = []
        for i in range(n_pg):
            page = pt_ref[bb * n_pages + c * n_pg + i]
            cps.append(pltpu.make_async_copy(lat_hbm.at[layer, page], kl_buf.at[slot, i], sem.at[slot, 0]))
            cps.append(pltpu.make_async_copy(kr_hbm.at[layer, page], kp_buf.at[slot, i], sem.at[slot, 1]))
        return cps

    ahead = n_slots - 1

    @pl.when(b == 0)
    def _():
        for c0 in range(ahead):
            for cp in page_copies(0, c0, c0):
                cp.start()

    q = q_ref[0]
    ql = q[:, :KV_RANK]
    qr = q[:, KV_RANK:]
    m_g = [jnp.full((rows, 1), -jnp.inf, F32) for _ in range(n_grp)]
    l_g = [jnp.zeros((rows, 1), F32) for _ in range(n_grp)]
    acc_g = [jnp.zeros((rows, KV_RANK), F32) for _ in range(n_grp)]
    for c in range(n_chunks):
        slot = c % n_slots
        nxt = c + ahead
        if nxt < n_chunks:
            for cp in page_copies(b, nxt, nxt % n_slots):
                cp.start()
        else:
            @pl.when(b + 1 < pl.num_programs(0))
            def _():
                for cp in page_copies(b + 1, nxt - n_chunks, nxt % n_slots):
                    cp.start()
        for cp in page_copies(b, c, slot):
            cp.wait()
        kls = [kl_buf[slot, i].astype(BF16) for i in range(n_pg)]
        ss = [_nt_dot(ql, kls[i]) + _dot(qr, kp_buf[slot, i].astype(BF16)) for i in range(n_pg)]
        alphas, ps = [], []
        for gi in range(n_grp):
            s = jnp.concatenate(ss[gi * per:(gi + 1) * per], axis=1)
            m_new = jnp.maximum(m_g[gi], jnp.max(s, axis=1, keepdims=True))
            alpha = jnp.exp2(m_g[gi] - m_new)
            p = jnp.exp2(s - m_new).astype(BF16)
            l_g[gi] = alpha * l_g[gi] + jnp.sum(p.astype(F32), axis=1, keepdims=True)
            m_g[gi] = m_new
            alphas.append(alpha)
            ps.append(p)
        for gi in range(n_grp):
            pv = _dot(ps[gi][:, :PAGE_SIZE], kls[gi * per])
            for i in range(1, per):
                pv = pv + _dot(ps[gi][:, i * PAGE_SIZE:(i + 1) * PAGE_SIZE], kls[gi * per + i])
            acc_g[gi] = alphas[gi] * acc_g[gi] + pv

    qf = q.astype(F32)
    kn = kn_ref[0]
    trow = lax.broadcasted_iota(jnp.int32, (rows, 1), 0) // MLA_HEADS
    cols = []
    for jj in range(t_new):
        sj = jnp.sum(qf * kn[jj:jj + 1, :], axis=1, keepdims=True)
        cols.append(jnp.where(trow >= jj, sj, -jnp.inf))
    m1 = m_g[0]
    for gi in range(1, n_grp):
        m1 = jnp.maximum(m1, m_g[gi])
    for sj in cols:
        m1 = jnp.maximum(m1, sj)
    l1 = jnp.zeros_like(m1)
    acc1 = jnp.zeros((rows, KV_RANK), F32)
    for gi in range(n_grp):
        ag = jnp.exp2(m_g[gi] - m1)
        l1 = l1 + ag * l_g[gi]
        acc1 = acc1 + ag * acc_g[gi]
    for jj, sj in enumerate(cols):
        pj = jnp.exp2(sj - m1)
        l1 = l1 + pj
        acc1 = acc1 + pj * kn[jj:jj + 1, :KV_RANK]
    o = (acc1 / l1).astype(BF16)
    proj = _dot(o, wuv_ref[...])
    rr = lax.broadcasted_iota(jnp.int32, proj.shape, 0)
    cc = lax.broadcasted_iota(jnp.int32, proj.shape, 1)
    proj = jnp.where(jnp.bitwise_and(rr, MLA_HEADS - 1) == cc // MLA_V, proj, 0.0)
    o_ref[0] = jnp.sum(proj.reshape(t_new, MLA_HEADS, MLA_HEADS * MLA_V), axis=1).astype(BF16)


MLA_SAMPLE_PAGES_PER_CHUNK = 32
MLA_SAMPLE_SLOTS = 4


def mla_sample(q_s, kn_s, wuv_all, cache_lat, cache_kr, layer, page_table):
    nb, rows, _ = q_s.shape
    t_new = rows // MLA_HEADS
    n_pages = page_table.shape[1]
    n_pg = min(MLA_SAMPLE_PAGES_PER_CHUNK, n_pages // 2)
    n_slots = min(MLA_SAMPLE_SLOTS, n_pages // n_pg)
    assert n_pages % (n_slots * n_pg) == 0
    n_grp = 2 if n_pg % 2 == 0 else 1
    grid_spec = pltpu.PrefetchScalarGridSpec(
        num_scalar_prefetch=1,
        grid=(nb,),
        in_specs=[pl.BlockSpec((1, rows, MLA_QK), lambda b, pt: (b, 0, 0)),
                  pl.BlockSpec((1, t_new, MLA_QK), lambda b, pt: (b, 0, 0)),
                  pl.BlockSpec((KV_RANK, MLA_HEADS * MLA_V), lambda b, pt: (0, 0)),
                  pl.BlockSpec(memory_space=pl.ANY),
                  pl.BlockSpec(memory_space=pl.ANY)],
        out_specs=pl.BlockSpec((1, t_new, MLA_HEADS * MLA_V), lambda b, pt: (b, 0, 0)),
        scratch_shapes=[pltpu.VMEM((n_slots, n_pg, PAGE_SIZE, KV_RANK), F32),
                        pltpu.VMEM((n_slots, n_pg, MLA_ROPE, PAGE_SIZE), F32),
                        pltpu.SemaphoreType.DMA((n_slots, 2))],
    )
    return pl.pallas_call(
        functools.partial(_mla_sample_kernel, layer=layer, n_pages=n_pages, n_pg=n_pg, n_grp=n_grp,
                          n_slots=n_slots, t_new=t_new),
        grid_spec=grid_spec,
        out_shape=jax.ShapeDtypeStruct((nb, t_new, MLA_HEADS * MLA_V), BF16),
        compiler_params=_params(("arbitrary",)),
        name="mla_sample",
    )(page_table.reshape(-1), q_s, kn_s, wuv_all, cache_lat, cache_kr)


RW_TENSORS = 7


def _shifted_rows(zr, carry):
    first = lax.broadcasted_iota(jnp.int32, zr.shape, 0) == 0
    prev = jnp.where(first, carry[...], pltpu.roll(zr, 1, 0))
    carry[...] = zr[zr.shape[0] - 1:, :]
    return prev


def _rw_prep_heads(zr, prev, prm, put):
    mu_ref, w0_ref, ww2_ref, a0_ref, wa2_ref, wg2_ref, kk_ref, ka_ref = prm
    zs = zr + (prev - zr) * mu_ref[...]
    o3 = 3 * RW_W
    o4 = o3 + RW_DECAY_LORA
    o5 = o4 + RW_A_LORA
    xr, xk, xv = zs[:, :RW_W], zs[:, RW_W:2 * RW_W], zs[:, 2 * RW_W:o3]
    xw, xa, xg = zs[:, o3:o4], zs[:, o4:o5], zs[:, o5:]
    wl = w0_ref[...] + _dot(jnp.tanh(xw).astype(BF16), ww2_ref[...])
    w_log = -(jnp.maximum(-wl, 0.0) + jnp.log1p(jnp.exp(-jnp.abs(wl)))) - 0.5
    logw = -jnp.exp(w_log)
    a = jax.nn.sigmoid(a0_ref[...] + _dot(xa.astype(BF16), wa2_ref[...]))
    g = _dot(jax.nn.sigmoid(xg).astype(BF16), wg2_ref[...])
    kkf = xk * kk_ref[...]
    kf = xk * (1.0 + (a - 1.0) * ka_ref[...])
    for h in range(RW_HEADS):
        sl = slice(h * RW_N, (h + 1) * RW_N)
        kkh = kkf[:, sl]
        kkh = kkh / jnp.maximum(jnp.sqrt(jnp.sum(kkh * kkh, axis=1, keepdims=True)), 1e-12)
        for i, val in enumerate((xr[:, sl], kf[:, sl], xv[:, sl], kkh, kkh * a[:, sl], logw[:, sl], g[:, sl])):
            put(i, h, val)


def _rwprep_kernel(zr_ref, pv_ref, *rest):
    prm = rest[:8]
    outs = rest[8:8 + RW_TENSORS]

    def put(i, h, val):
        outs[i][0, h] = val

    _rw_prep_heads(zr_ref[0], pv_ref[0], prm, put)


def rwkv_prep(zr, prev, mu, w0, w_w2, a0, w_a2, w_g2, k_k, k_a, tm):
    nb, m, w = zr.shape
    vec = lambda n: pl.BlockSpec((1, n), lambda b, i: (0, 0))
    mat = lambda a: pl.BlockSpec(a.shape, lambda b, i: (0, 0))
    out_spec = pl.BlockSpec((1, RW_HEADS, tm, RW_N), lambda b, i: (b, 0, i, 0))
    out_sds = jax.ShapeDtypeStruct((nb, RW_HEADS, m, RW_N), F32)
    return pl.pallas_call(
        _rwprep_kernel,
        grid=(nb, m // tm),
        in_specs=[pl.BlockSpec((1, tm, w), lambda b, i: (b, i, 0)),
                  pl.BlockSpec((1, tm, w), lambda b, i: (b, i, 0)),
                  vec(w), vec(RW_W), mat(w_w2), vec(RW_W), mat(w_a2), mat(w_g2), vec(RW_W), vec(RW_W)],
        out_specs=[out_spec] * RW_TENSORS,
        out_shape=[out_sds] * RW_TENSORS,
        compiler_params=_params(("parallel", "parallel")),
        name="rwkv_prep",
    )(zr, prev, mu.reshape(1, w), w0.reshape(1, RW_W), w_w2, a0.reshape(1, RW_W), w_a2, w_g2,
      k_k.reshape(1, RW_W), k_a.reshape(1, RW_W))


def _split_bf16(x, terms):
    parts = []
    rem = x
    for i in range(terms):
        p = rem.astype(BF16)
        parts.append(p)
        if i + 1 < terms:
            rem = rem - p.astype(F32)
    return parts


def _mm(a, b, ta, tb, dot=_dot):
    ap = _split_bf16(a, ta)
    bp = _split_bf16(b, tb)
    n = max(ta, tb)
    acc = None
    for i, x in enumerate(ap):
        for j, y in enumerate(bp):
            if i + j < n:
                d = dot(x, y)
                acc = d if acc is None else acc + d
    return acc


RW_P_CUMSUM = 2
RW_P_INTRA = 1
RW_P_INV = 1
RW_P_STATE = 1


def _rwkv_chunk(get, rk_ref, lnw_ref, lnb_ref, st_ref, chunk, n_elem):
    heads = range(n_elem * RW_HEADS)
    st_at = lambda h: (h // RW_HEADS, h % RW_HEADS)
    r_h, k_h, v_h, kk_h, kka_h, lw_h, g_h = ([get(i, h) for h in heads] for i in range(RW_TENSORS))
    c2 = 2 * chunk
    row = lax.broadcasted_iota(jnp.int32, (chunk, chunk), 0)
    col = lax.broadcasted_iota(jnp.int32, (chunk, chunk), 1)
    tri = jnp.where(col <= row, 1.0, 0.0).astype(BF16)
    eye_c = jnp.where(row == col, 1.0, 0.0).astype(F32)
    row2 = lax.broadcasted_iota(jnp.int32, (c2, c2), 0)
    col2 = jnp.bitwise_and(lax.broadcasted_iota(jnp.int32, (c2, c2), 1), chunk - 1)
    mask2 = col2 < jnp.where(row2 < chunk, row2, row2 - (chunk - 1))
    eye_n = lax.broadcasted_iota(jnp.int32, (RW_N, RW_N), 0) == lax.broadcasted_iota(jnp.int32, (RW_N, RW_N), 1)
    zeros_cn = jnp.zeros((chunk, RW_N), F32)
    n_double = int(math.log2(chunk)) - 1
    cs = [_mm(tri, lw_h[h], 1, RW_P_CUMSUM) for h in heads]
    lhs, rhs, g_end, g_end_col = [], [], [], []
    for h in heads:
        g_incl = jnp.exp(cs[h])
        g_prev = jnp.exp(cs[h] - lw_h[h])
        g_inv = jnp.exp(-cs[h])
        cs_last = cs[h][chunk - 1:chunk, :]
        g_end.append(jnp.exp(cs_last))
        g_end_col.append(jnp.exp(jnp.sum(jnp.where(eye_n, jnp.broadcast_to(cs_last, (RW_N, RW_N)), 0.0),
                                         axis=1, keepdims=True)))
        lhs.append(jnp.concatenate([-kk_h[h] * g_prev, r_h[h] * g_incl], axis=0))
        rhs.append(jnp.concatenate([kka_h[h] * g_inv, k_h[h] * g_inv], axis=0))
    mx = [jnp.where(mask2, _mm(lhs[h], rhs[h], RW_P_INTRA, RW_P_INTRA, _nt_dot), 0.0) for h in heads]
    from_state = [_mm(lhs[h], st_ref[st_at(h)], RW_P_STATE, RW_P_STATE) for h in heads]
    from_v = [_mm(mx[h], jnp.concatenate([zeros_cn, v_h[h]], axis=0), RW_P_INTRA, RW_P_INTRA)
              for h in heads]
    l_ab = [mx[h][:chunk, :chunk] for h in heads]
    tinv = [eye_c + l_ab[h] for h in heads]
    pw = [_mm(l_ab[h], l_ab[h], RW_P_INV, RW_P_INV) for h in heads]
    for _ in range(n_double - 1):
        both = [_mm(jnp.concatenate([pw[h], tinv[h]], axis=0), pw[h], RW_P_INV, RW_P_INV) for h in heads]
        tinv = [tinv[h] + both[h][chunk:] for h in heads]
        pw = [both[h][:chunk] for h in heads]
    tinv = [tinv[h] + _mm(tinv[h], pw[h], RW_P_INV, RW_P_INV) for h in heads]
    u = [_mm(tinv[h], from_state[h][:chunk] + from_v[h][:chunk], RW_P_INV, RW_P_INV) for h in heads]
    y_u = [_mm(mx[h][chunk:, :chunk], u[h], RW_P_INTRA, RW_P_INTRA) for h in heads]
    st_add = [_mm(rhs[h] * g_end[h], jnp.concatenate([u[h], v_h[h]], axis=0), RW_P_STATE, RW_P_STATE, _tn_dot)
              for h in heads]
    outs = []
    for h in heads:
        hp = h % RW_HEADS
        st_ref[st_at(h)] = st_ref[st_at(h)] * g_end_col[h] + st_add[h]
        y = from_state[h][chunk:] + from_v[h][chunk:] + y_u[h]
        mean = jnp.mean(y, axis=1, keepdims=True)
        yc = y - mean
        var = jnp.mean(yc * yc, axis=1, keepdims=True)
        yn = yc * lax.rsqrt(var + RW_GN_EPS) * lnw_ref[hp:hp + 1, :] + lnb_ref[hp:hp + 1, :]
        bonus = jnp.sum(r_h[h] * k_h[h] * rk_ref[hp:hp + 1, :], axis=1, keepdims=True) * v_h[h]
        outs.append((yn + bonus) * g_h[h])
    return outs


def _rwkv_fused_kernel(zr0_ref, zra_ref, zrb_ref, sp_ref, *rest, chunk, n_elem):
    prm = rest[:8]
    rk_ref, lnw_ref, lnb_ref, s0_ref, o_ref, st_ref, buf, carry = rest[8:]
    j = pl.program_id(1)

    def prep(zr_ref, slot):
        for e in range(n_elem):
            zr = zr_ref[e]
            prev = _shifted_rows(zr, carry.at[e])

            def put(i, h, val):
                buf[slot, i, e * RW_HEADS + h] = val

            _rw_prep_heads(zr, prev, prm, put)

    @pl.when(j == 0)
    def _():
        st_ref[...] = s0_ref[...]
        carry[...] = sp_ref[...]
        prep(zr0_ref, 0)

    for slot, nxt_ref in ((0, zra_ref), (1, zrb_ref)):
        outs = _rwkv_chunk(lambda i, h: buf[slot, i, h], rk_ref, lnw_ref, lnb_ref, st_ref, chunk, n_elem)
        for e in range(n_elem):
            out = jnp.concatenate(outs[e * RW_HEADS:(e + 1) * RW_HEADS], axis=1)
            o_ref[e, slot * chunk:(slot + 1) * chunk, :] = out.astype(BF16)
        prep(nxt_ref, 1 - slot)


RW_ELEMS_PER_STEP = 2


def rwkv_fused(zr, shift_prev, mu, w0, w_w2, a0, w_a2, w_g2, k_k, k_a, r_k, ln_w, ln_b, s0_t, chunk):
    nb, t, w = zr.shape
    nc = t // chunk
    assert nc % 2 == 0
    ne = RW_ELEMS_PER_STEP if nb % RW_ELEMS_PER_STEP == 0 else 1
    vec = lambda n: pl.BlockSpec((1, n), lambda b, j: (0, 0))
    mat = lambda a: pl.BlockSpec(a.shape, lambda b, j: (0, 0))
    hspec = pl.BlockSpec((RW_HEADS, RW_N), lambda b, j: (0, 0))
    sspec = pl.BlockSpec((ne, RW_HEADS, RW_N, RW_N), lambda b, j: (b, 0, 0, 0))
    return pl.pallas_call(
        functools.partial(_rwkv_fused_kernel, chunk=chunk, n_elem=ne),
        grid=(nb // ne, nc // 2),
        in_specs=[pl.BlockSpec((ne, chunk, w), lambda b, j: (b, 0, 0)),
                  pl.BlockSpec((ne, chunk, w), lambda b, j: (b, 2 * j + 1, 0)),
                  pl.BlockSpec((ne, chunk, w), lambda b, j: (b, jnp.minimum(2 * j + 2, nc - 1), 0)),
                  pl.BlockSpec((ne, 1, w), lambda b, j: (b, 0, 0)),
                  vec(w), vec(RW_W), mat(w_w2), vec(RW_W), mat(w_a2), mat(w_g2), vec(RW_W), vec(RW_W),
                  hspec, hspec, hspec, sspec],
        out_specs=[pl.BlockSpec((ne, 2 * chunk, RW_W), lambda b, j: (b, j, 0)), sspec],
        out_shape=[jax.ShapeDtypeStruct((nb, t, RW_W), BF16),
                   jax.ShapeDtypeStruct((nb, RW_HEADS, RW_N, RW_N), F32)],
        scratch_shapes=[pltpu.VMEM((2, RW_TENSORS, ne * RW_HEADS, chunk, RW_N), F32), pltpu.VMEM((ne, 1, w), F32)],
        compiler_params=_params(("parallel", "arbitrary")),
        name="rwkv_fused",
    )(zr, zr, zr, shift_prev, mu.reshape(1, w), w0.reshape(1, RW_W), w_w2, a0.reshape(1, RW_W), w_a2, w_g2,
      k_k.reshape(1, RW_W), k_a.reshape(1, RW_W), r_k, ln_w.reshape(RW_HEADS, RW_N), ln_b.reshape(RW_HEADS, RW_N),
      s0_t)


def _rwkv_step_kernel(r_ref, k_ref, v_ref, kk_ref, kka_ref, lw_ref, g_ref, rk_ref, lnw_ref, lnb_ref, s0_ref,
                      o_ref, s_ref, w_scr, y_scr, *, t_new):
    for t in range(t_new):
        w_scr[t] = jnp.exp(lw_ref[t, 0])

    def value_row(vi, carry):
        s = s0_ref[0, vi]
        for t in range(t_new):
            sa = -jnp.sum(s * kk_ref[t, 0], axis=0, keepdims=True)
            s = s * w_scr[t] + sa * kka_ref[t, 0] + v_ref[t, 0, pl.ds(vi, 1), :] * k_ref[t, 0]
            y_scr[t, pl.ds(vi, 1), :] = jnp.sum(s * r_ref[t, 0], axis=0, keepdims=True)
        s_ref[0, vi] = s
        return carry

    lax.fori_loop(0, RW_N, value_row, 0)
    for t in range(t_new):
        y = y_scr[t]
        mean = jnp.mean(y, axis=0, keepdims=True)
        yc = y - mean
        var = jnp.mean(yc * yc, axis=0, keepdims=True)
        yn = yc * lax.rsqrt(var + RW_GN_EPS) * lnw_ref[0] + lnb_ref[0]
        bonus = jnp.sum(r_ref[t, 0] * k_ref[t, 0] * rk_ref[0], axis=0, keepdims=True) * v_ref[t, 0]
        o_ref[t, 0] = (yn + bonus) * g_ref[t, 0]


def rwkv_step(r, k, v, kk, kka, lw, g, r_k, ln_w, ln_b, s0):
    t_new, nh, n, nb = r.shape
    tspec = pl.BlockSpec((t_new, 1, n, nb), lambda h: (0, h, 0, 0))
    hspec = pl.BlockSpec((1, n, nb), lambda h: (h, 0, 0))
    sspec = pl.BlockSpec((1, n, n, nb), lambda h: (h, 0, 0, 0))
    return pl.pallas_call(
        functools.partial(_rwkv_step_kernel, t_new=t_new),
        grid=(nh,),
        in_specs=[tspec] * 7 + [hspec] * 3 + [sspec],
        out_specs=[tspec, sspec],
        out_shape=[jax.ShapeDtypeStruct((t_new, nh, n, nb), F32), jax.ShapeDtypeStruct((nh, n, n, nb), F32)],
        scratch_shapes=[pltpu.VMEM((t_new, n, nb), F32), pltpu.VMEM((t_new, n, nb), F32)],
        compiler_params=_params(("parallel",)),
        name="rwkv_step",
    )(r, k, v, kk, kka, lw, g, r_k, ln_w, ln_b, s0)


def _retention_kernel(lg_ref, qk_ref, v_ref, g_ref, cos_ref, sin_ref, s0_ref, o_ref, s_ref, *, lb, l_true):
    c = pl.program_id(1)

    @pl.when(c == 0)
    def _():
        s_ref[0] = s0_ref[0]

    lp = max(lb, 16)
    cos = cos_ref[0]
    sin = sin_ref[0]
    half = RET_DK // 2
    qk_w = RET_HEADS * RET_DK

    def rope(x):
        x1, x2 = x[:, :half], x[:, half:]
        return jnp.concatenate([x1 * cos - x2 * sin, x1 * sin + x2 * cos], axis=1)

    def rows(x):
        if lp == lb:
            return x
        return jnp.concatenate([x, jnp.zeros((lp - lb, x.shape[1]), x.dtype)], axis=0)

    row = lax.broadcasted_iota(jnp.int32, (lp, lp), 0)
    col = lax.broadcasted_iota(jnp.int32, (lp, lp), 1)
    diff = (row - col).astype(F32)
    idx = lax.broadcasted_iota(jnp.int32, (lp, 1), 0).astype(F32)
    heads = range(RET_HEADS)
    qm, km, kdm, vm, dmask, row_dec = [], [], [], [], [], []
    for h in heads:
        lg = lg_ref[h]
        q = rows(rope(qk_ref[0, :, h * RET_DK:(h + 1) * RET_DK]))
        k = rows(rope(qk_ref[0, :, qk_w + h * RET_DK:qk_w + (h + 1) * RET_DK]) * (RET_DK ** -0.5))
        qm.append(q.astype(BF16))
        km.append(k.astype(BF16))
        kdm.append((k * jnp.exp((l_true - 1.0 - idx) * lg)).astype(BF16))
        vm.append(rows(v_ref[0, :, h * RET_DV:(h + 1) * RET_DV]).astype(BF16))
        dmask.append(jnp.where(diff >= 0, jnp.exp(jnp.maximum(diff, 0.0) * lg), 0.0))
        row_dec.append(jnp.exp((idx + 1.0) * lg))
    sc = [(_nt_dot(qm[h], km[h]) * dmask[h]).astype(BF16) for h in heads]
    cross = [_dot(qm[h], s_ref[0, h].astype(BF16)) * row_dec[h] for h in heads]
    s_add = [_tn_dot(kdm[h], vm[h]) for h in heads]
    inner = [_dot(sc[h], vm[h]) for h in heads]
    outs = []
    for h in heads:
        s_dec = jnp.exp(jnp.zeros((1, RET_DV), F32) + l_true * lg_ref[h])
        s_ref[0, h] = s_ref[0, h] * s_dec + s_add[h]
        o = (inner[h] + cross[h])[:lb]
        o = o * lax.rsqrt(jnp.mean(o * o, axis=1, keepdims=True) + NORM_EPS)
        gv = g_ref[0, :, h * RET_DV:(h + 1) * RET_DV]
        outs.append(o * (gv * jax.nn.sigmoid(gv)))
    o_ref[0] = jnp.concatenate(outs, axis=1).astype(BF16)


def retention(z, cos, sin, lg, s0, lb, l_true):
    nb, m, _ = z.shape
    vw = RET_HEADS * RET_DV
    assert 2 * RET_HEADS * RET_DK == vw
    sspec = pl.BlockSpec((1, RET_HEADS, RET_DK, RET_DV), lambda b, c: (b, 0, 0, 0))
    return pl.pallas_call(
        functools.partial(_retention_kernel, lb=lb, l_true=float(l_true)),
        grid=(nb, m // lb),
        in_specs=[pl.BlockSpec(memory_space=pltpu.SMEM),
                  pl.BlockSpec((1, lb, vw), lambda b, c: (b, c, 0)),
                  pl.BlockSpec((1, lb, vw), lambda b, c: (b, c, 1)),
                  pl.BlockSpec((1, lb, vw), lambda b, c: (b, c, 2)),
                  pl.BlockSpec((1, lb, RET_DK // 2), lambda b, c: (0, c, 0)),
                  pl.BlockSpec((1, lb, RET_DK // 2), lambda b, c: (0, c, 0)),
                  sspec],
        out_specs=[pl.BlockSpec((1, lb, vw), lambda b, c: (b, c, 0)), sspec],
        out_shape=[jax.ShapeDtypeStruct((nb, m, vw), BF16),
                   jax.ShapeDtypeStruct((nb, RET_HEADS, RET_DK, RET_DV), F32)],
        compiler_params=_params(("parallel", "arbitrary")),
        name="retention",
    )(lg, z, z, z, cos, sin, s0)


def _rope_tables(pos, half):
    inv = ROPE_BASE ** (-jnp.arange(half, dtype=F32) / half)
    ang = pos.astype(F32)[:, None] * inv[None, :]
    return jnp.cos(ang), jnp.sin(ang)


def _mla_tables(pos):
    cos, sin = _rope_tables(pos, MLA_ROPE // 2)
    return jnp.concatenate([cos, cos], axis=1), jnp.concatenate([-sin, sin], axis=1)


def _even_layer(x, mods, pos_tabs, prm, past, tm):
    (w_in_p, g_mix, g_kv, wuk_t, wuk_r, wuv_t, wuv_all, mu, w0, w_w2, a0, w_a2, w_g2, k_k, k_a, r_k, ln_w, ln_b,
     w_out_mla, w_out_rw) = prm
    sh1, sc1, gt1 = mods
    cf, sf = pos_tabs
    nb, m, _ = x.shape
    zr, zq, zkv = norm_mod_matmul_split(x, g_mix, sh1, sc1, w_in_p, (RW_SHIFT_W, MLA_HEADS * (MLA_NOPE + MLA_ROPE), MLA_QK), tm)
    lat, kr, kcat, latt = kv_prep(zkv, g_kv, cf, sf, tm)
    if past is None:
        qt = q_prep_t(zq, wuk_r, cf, sf, min(MLA_PROMPT_TQ, m))
        mla_out = mla_prompt(qt, kcat, latt, wuv_t, 512)
        mla_transposed = True
        s0_t = jnp.zeros((nb, RW_HEADS, RW_N, RW_N), F32)
        rw_out, s_t = rwkv_fused(zr, jnp.zeros((nb, 1, RW_SHIFT_W), F32), mu, w0, w_w2, a0, w_a2, w_g2, k_k, k_a,
                                 r_k, ln_w, ln_b, s0_t, RW_CHUNK)
        s_new = jnp.swapaxes(s_t, -1, -2)
        shift_new = zr[:, -1]
    else:
        cache_lat, cache_kr, layer, page_table, s0, shift_prev, t_new = past
        nbs = m // t_new
        mla_transposed = False
        qcat = q_prep(zq, wuk_t, cf, sf, tm)
        q_s = qcat.reshape(MLA_HEADS, nbs, t_new, MLA_QK).transpose(1, 2, 0, 3).reshape(nbs, t_new * MLA_HEADS, MLA_QK)
        kn_s = jnp.concatenate([lat, kr], axis=-1).reshape(nbs, t_new, MLA_QK)
        mla_out = mla_sample(q_s, kn_s, wuv_all, cache_lat, cache_kr, layer, page_table).reshape(1, m, MLA_HEADS * MLA_V)
        zr_b = zr.reshape(nbs, t_new, RW_SHIFT_W)
        prev = jnp.concatenate([shift_prev[:, None, :], zr_b[:, :-1]], axis=1).reshape(1, m, RW_SHIFT_W)
        tens = rwkv_prep(zr, prev, mu, w0, w_w2, a0, w_a2, w_g2, k_k, k_a, min(tm, 256))
        tens = [u.reshape(RW_HEADS, nbs, t_new, RW_N).transpose(2, 0, 3, 1) for u in tens]
        lanes = lambda p: jnp.broadcast_to(p.reshape(RW_HEADS, RW_N, 1), (RW_HEADS, RW_N, nbs))
        rw_l, s_l = rwkv_step(*tens, lanes(r_k), lanes(ln_w), lanes(ln_b), jnp.transpose(s0, (1, 2, 3, 0)))
        rw_out = rw_l.transpose(3, 0, 1, 2).reshape(1, m, RW_W).astype(BF16)
        s_new = jnp.transpose(s_l, (3, 0, 1, 2))
        shift_new = zr_b[:, -1]
    x_new = matmul_gate_res([mla_out, rw_out], [w_out_mla, w_out_rw], x, gt1, min(2 * tm, m), (mla_transposed, False))
    return x_new, (lat, kr, s_new, shift_new)


def _odd_layer(x, mods, ret_tabs, prm, s0, t_new, tm):
    w_in, g_mix, w_out, lg = prm
    sh1, sc1, gt1 = mods
    cos, sin = ret_tabs
    nb, m, _ = x.shape
    z = norm_mod_matmul(x, g_mix, sh1, sc1, w_in, min(2 * tm, m), 2048)
    if s0 is None:
        s0 = jnp.zeros((nb, RET_HEADS, RET_DK, RET_DV), F32)
        lb = min(RET_BLOCK, m)
        o, s_new = retention(z, cos, sin, lg, s0, lb, lb)
    else:
        nbs = m // t_new
        lpad = 8
        z_b = jnp.pad(z.reshape(nbs, t_new, -1), ((0, 0), (0, lpad - t_new), (0, 0)))
        o, s_new = retention(z_b, cos, sin, lg, s0, lpad, t_new)
        o = o[:, :t_new].reshape(1, m, RET_HEADS * RET_DV)
    x_new = matmul_gate_res([o], [w_out], x, gt1, min(2 * tm, m))
    return x_new, s_new


def kernel(x_prompt, x_sample, c_prompt, c_sample, cache_kv_latent, cache_k_rope, page_table, state_rwkv, state_rwkv_shift, state_ret, w_ada, b_ada, g_norm_mix, g_norm_mlp, g_final, w_in_even, g_kv, w_uk, w_uv, rw_mu, rw_w0, rw_w2, rw_a0, rw_a2, rw_g2, rw_k_k, rw_k_a, rw_r_k, rw_ln_w, rw_ln_b, w_out_even, w_in_odd, w_out_odd, w_ff1, w_ff2):
    nbp, t_p, d = x_prompt.shape
    nbs, t_s, _ = x_sample.shape
    depth = w_ada.shape[0]
    past_len = page_table.shape[1] * PAGE_SIZE
    m_s = nbs * t_s
    tm_p = min(512, t_p)
    tm_s = m_s

    c_all = jnp.concatenate([jnp.repeat(c_sample, t_s, axis=0), c_prompt], axis=0)
    c_all = jnp.pad(c_all, ((0, -c_all.shape[0] % 16), (0, 0)))
    mods_all = ada_proj(c_all, w_ada, b_ada)

    def group_mods(l):
        mp = mods_all[l, m_s:m_s + nbp].reshape(nbp, 1, 6, d)
        return [mp[:, :, i] for i in range(6)], [(mods_all, l, i) for i in range(6)]

    pos_p = jnp.arange(t_p)
    pos_s = past_len + jnp.arange(t_s)
    cf_p, sf_p = _mla_tables(pos_p)
    cf_s, sf_s = _mla_tables(pos_s)
    mla_tabs_p = (cf_p[None], sf_p[None])
    mla_tabs_s = (jnp.tile(cf_s, (nbs, 1))[None], jnp.tile(sf_s, (nbs, 1))[None])
    cr_p, sr_p = _rope_tables(pos_p, RET_DK // 2)
    cr_s, sr_s = _rope_tables(pos_s, RET_DK // 2)
    ret_tabs_p = (cr_p[None], sr_p[None])
    ret_tabs_s = (jnp.pad(cr_s, ((0, 8 - t_s), (0, 0)))[None], jnp.pad(sr_s, ((0, 8 - t_s), (0, 0)))[None])
    lg = jnp.log(1 - 2.0 ** (-5.0 - jnp.arange(RET_HEADS, dtype=F32)))

    xp = x_prompt
    xs = x_sample.reshape(1, m_s, d)
    lat_p, kr_p, rw_p, sh_p, ret_p = [], [], [], [], []
    lat_s, kr_s, rw_s, sh_s, ret_s = [], [], [], [], []
    q_w = MLA_HEADS * (MLA_NOPE + MLA_ROPE)
    for l in range(depth):
        (sh1p, sc1p, gt1p, sh2p, sc2p, gt2p), (sh1s, sc1s, gt1s, sh2s, sc2s, gt2s) = group_mods(l)
        i = l // 2
        if l % 2 == 0:
            w_in = w_in_even[i]
            wq = w_in[:, :q_w].reshape(d, MLA_HEADS, MLA_NOPE + MLA_ROPE)
            w_in_p = jnp.concatenate([w_in[:, q_w + MLA_QK:],
                                      wq[:, :, :MLA_NOPE].reshape(d, -1), wq[:, :, MLA_NOPE:].reshape(d, -1),
                                      w_in[:, q_w:q_w + MLA_QK]], axis=1).astype(BF16)
            wuv = w_uv[i]
            mla_w = MLA_HEADS * MLA_V
            prm = (w_in_p, g_norm_mix[l], g_kv[i], jnp.transpose(w_uk[i], (1, 2, 0)).astype(BF16),
                   jnp.transpose(w_uk[i], (1, 0, 2)).astype(BF16), jnp.transpose(wuv, (1, 2, 0)).astype(BF16), wuv.reshape(KV_RANK, mla_w).astype(BF16),
                   rw_mu[i], rw_w0[i], rw_w2[i].astype(BF16), rw_a0[i], rw_a2[i].astype(BF16), rw_g2[i].astype(BF16),
                   rw_k_k[i], rw_k_a[i], rw_r_k[i], rw_ln_w[i], rw_ln_b[i],
                   w_out_even[i, :mla_w].astype(BF16), w_out_even[i, mla_w:].astype(BF16))
            xp, (la, kr, st, sh) = _even_layer(xp, (sh1p, sc1p, gt1p), mla_tabs_p, prm, None, tm_p)
            lat_p.append(la); kr_p.append(kr); rw_p.append(st); sh_p.append(sh)
            cache_kr_t = jnp.swapaxes(cache_k_rope, 2, 3)
            past = (cache_kv_latent, cache_kr_t, i, page_table, state_rwkv[i], state_rwkv_shift[i], t_s)
            xs, (la, kr, st, sh) = _even_layer(xs, (sh1s, sc1s, gt1s), mla_tabs_s, prm, past, tm_s)
            lat_s.append(la.reshape(nbs, t_s, KV_RANK)); kr_s.append(kr.reshape(nbs, t_s, MLA_ROPE))
            rw_s.append(st); sh_s.append(sh)
        else:
            prm = (w_in_odd[i].astype(BF16), g_norm_mix[l], w_out_odd[i].astype(BF16), lg)
            xp, st = _odd_layer(xp, (sh1p, sc1p, gt1p), ret_tabs_p, prm, None, t_s, tm_p)
            ret_p.append(st)
            xs, st = _odd_layer(xs, (sh1s, sc1s, gt1s), ret_tabs_s, prm, state_ret[i], t_s, tm_s)
            ret_s.append(st)
        final = l == depth - 1
        w1 = w_ff1[l].astype(BF16)
        w2 = w_ff2[l].astype(BF16)
        xp = mlp_block(xp, g_norm_mlp[l], sh2p, sc2p, gt2p, w1, w2, g_final, final, min(2 * tm_p, t_p), 1024)
        xs = mlp_block(xs, g_norm_mlp[l], sh2s, sc2s, gt2s, w1, w2, g_final, final, tm_s, 1024)
    return (xp, xs.reshape(nbs, t_s, d),
            jnp.stack(lat_p), jnp.stack(kr_p), jnp.stack(rw_p), jnp.stack(sh_p), jnp.stack(ret_p),
            jnp.stack(lat_s), jnp.stack(kr_s), jnp.stack(rw_s), jnp.stack(sh_s), jnp.stack(ret_s))
```

```python
import functools
import math

import jax
import jax.numpy as jnp
from jax import lax
from jax.experimental import pallas as pl
from jax.experimental.pallas import tpu as pltpu

F32 = jnp.float32
BF16 = jnp.bfloat16

D_MODEL = 1024
PAGE_SIZE = 128
MLA_HEADS = 8
MLA_NOPE = 64
MLA_ROPE = 32
MLA_V = 64
KV_RANK = 256
MLA_QK = KV_RANK + MLA_ROPE
MLA_SCALE = (MLA_NOPE + MLA_ROPE) ** -0.5
MLA_QSCALE = MLA_SCALE * math.log2(math.e)
RW_HEADS = 8
RW_N = 64
RW_W = RW_HEADS * RW_N
RW_DECAY_LORA = 64
RW_A_LORA = 64
RW_G_LORA = 128
RW_SHIFT_W = 3 * RW_W + RW_DECAY_LORA + RW_A_LORA + RW_G_LORA
RW_GN_EPS = 64e-5
RW_CHUNK = 64
RET_HEADS = 4
RET_DK = 256
RET_DV = 512
RET_BLOCK = 256
D_FF = 4 * D_MODEL
ROPE_BASE = 10000.0
NORM_EPS = 1e-6
MIB = 1024 * 1024


V7X_VMEM_MIB = 64
VMEM_LIMIT_MIB = V7X_VMEM_MIB - 16
VMEM_LIMIT_MLP_MIB = V7X_VMEM_MIB - 8


def _params(sem, vmem_mib=VMEM_LIMIT_MIB):
    return pltpu.CompilerParams(dimension_semantics=sem, vmem_limit_bytes=vmem_mib * MIB)


def _rms(x, g):
    return x * lax.rsqrt(jnp.mean(x * x, axis=-1, keepdims=True) + NORM_EPS) * g


def _nt_dot(a, b, precision=None):
    return lax.dot_general(a, b, (((1,), (1,)), ((), ())), precision=precision, preferred_element_type=F32)


def _tn_dot(a, b, precision=None):
    return lax.dot_general(a, b, (((0,), (0,)), ((), ())), precision=precision, preferred_element_type=F32)


def _dot(a, b, precision=None):
    return jnp.dot(a, b, precision=precision, preferred_element_type=F32)


def _mod_arg(mod):
    return mod[0] if isinstance(mod, tuple) else mod


def _mod_spec(mod, tm, nmid):
    if isinstance(mod, tuple):
        _, layer, which = mod
        if nmid == 2:
            return pl.BlockSpec((1, tm, D_MODEL), lambda b, m, j: (layer, m, which))
        return pl.BlockSpec((1, tm, D_MODEL), lambda b, m: (layer, m, which))
    if mod.shape[1] == 1:
        if nmid == 2:
            return pl.BlockSpec((1, 1, mod.shape[2]), lambda b, m, j: (b, 0, 0))
        return pl.BlockSpec((1, 1, mod.shape[2]), lambda b, m: (b, 0, 0))
    if nmid == 2:
        return pl.BlockSpec((1, tm, mod.shape[2]), lambda b, m, j: (b, m, 0))
    return pl.BlockSpec((1, tm, mod.shape[2]), lambda b, m: (b, m, 0))


def _ada_kernel(c_ref, w_ref, b_ref, o_ref):
    o_ref[0] = _dot(c_ref[...].astype(BF16), w_ref[0].astype(BF16)) + b_ref[0]


def ada_proj(c, w_ada, b_ada):
    nl, d, n = w_ada.shape
    r = c.shape[0]
    tn = 1536
    return pl.pallas_call(
        _ada_kernel,
        grid=(nl, n // tn),
        in_specs=[pl.BlockSpec((r, d), lambda l, j: (0, 0)),
                  pl.BlockSpec((1, d, tn), lambda l, j: (l, 0, j)),
                  pl.BlockSpec((1, 1, tn), lambda l, j: (l, 0, j))],
        out_specs=pl.BlockSpec((1, r, tn), lambda l, j: (l, 0, j)),
        out_shape=jax.ShapeDtypeStruct((nl, r, n), F32),
        compiler_params=_params(("parallel", "parallel")),
        name="ada_proj",
    )(c, w_ada, b_ada.reshape(nl, 1, n))


def _nmm_split_kernel(x_ref, g_ref, sh_ref, sc_ref, w_ref, *o_refs, splits):
    h = (_rms(x_ref[0], g_ref[...]) * (1.0 + sc_ref[0]) + sh_ref[0]).astype(BF16)
    off = 0
    for o_ref, n in zip(o_refs, splits):
        o_ref[0] = _dot(h, w_ref[:, off:off + n])
        off += n


def norm_mod_matmul_split(x, g, shift, scale, w, splits, tm):
    nb, m, d = x.shape
    n = w.shape[1]
    return pl.pallas_call(
        functools.partial(_nmm_split_kernel, splits=splits),
        grid=(nb, m // tm),
        in_specs=[pl.BlockSpec((1, tm, d), lambda b, i: (b, i, 0)),
                  pl.BlockSpec((1, d), lambda b, i: (0, 0)),
                  _mod_spec(shift, tm, 1), _mod_spec(scale, tm, 1),
                  pl.BlockSpec((d, n), lambda b, i: (0, 0))],
        out_specs=[pl.BlockSpec((1, tm, s), lambda b, i: (b, i, 0)) for s in splits],
        out_shape=[jax.ShapeDtypeStruct((nb, m, s), F32) for s in splits],
        compiler_params=_params(("parallel", "parallel")),
        name="norm_mod_matmul_split",
    )(x, g.reshape(1, d), _mod_arg(shift), _mod_arg(scale), w)


def _nmm_kernel(x_ref, g_ref, sh_ref, sc_ref, w_ref, o_ref):
    h = (_rms(x_ref[0], g_ref[...]) * (1.0 + sc_ref[0]) + sh_ref[0]).astype(BF16)
    o_ref[0] = _dot(h, w_ref[...])


def norm_mod_matmul(x, g, shift, scale, w, tm, tn):
    nb, m, d = x.shape
    n = w.shape[1]
    return pl.pallas_call(
        _nmm_kernel,
        grid=(nb, m // tm, n // tn),
        in_specs=[pl.BlockSpec((1, tm, d), lambda b, i, j: (b, i, 0)),
                  pl.BlockSpec((1, d), lambda b, i, j: (0, 0)),
                  _mod_spec(shift, tm, 2), _mod_spec(scale, tm, 2),
                  pl.BlockSpec((d, tn), lambda b, i, j: (0, j))],
        out_specs=pl.BlockSpec((1, tm, tn), lambda b, i, j: (b, i, j)),
        out_shape=jax.ShapeDtypeStruct((nb, m, n), F32),
        compiler_params=_params(("parallel", "parallel", "arbitrary")),
        name="norm_mod_matmul",
    )(x, g.reshape(1, d), _mod_arg(shift), _mod_arg(scale), w)


def _mgr_kernel(*refs, n_pairs, transposed):
    a_refs = refs[:n_pairs]
    w_refs = refs[n_pairs:2 * n_pairs]
    res_ref, gt_ref, o_ref = refs[2 * n_pairs:]
    acc = None
    for a_ref, w_ref, tr in zip(a_refs, w_refs, transposed):
        d = (_tn_dot if tr else _dot)(a_ref[0], w_ref[...])
        acc = d if acc is None else acc + d
    o_ref[0] = res_ref[0] + gt_ref[0] * acc


def matmul_gate_res(a_list, w_list, res, gate, tm, transposed=None):
    nb, m, d = res.shape
    n_pairs = len(a_list)
    transposed = tuple(transposed or (False,) * n_pairs)
    in_specs = [pl.BlockSpec((1, a.shape[1], tm), lambda b, i: (b, 0, i)) if tr else
                pl.BlockSpec((1, tm, a.shape[2]), lambda b, i: (b, i, 0)) for a, tr in zip(a_list, transposed)]
    in_specs += [pl.BlockSpec(w.shape, lambda b, i: (0, 0)) for w in w_list]
    in_specs += [pl.BlockSpec((1, tm, d), lambda b, i: (b, i, 0)), _mod_spec(gate, tm, 1)]
    return pl.pallas_call(
        functools.partial(_mgr_kernel, n_pairs=n_pairs, transposed=transposed),
        grid=(nb, m // tm),
        in_specs=in_specs,
        out_specs=pl.BlockSpec((1, tm, d), lambda b, i: (b, i, 0)),
        out_shape=jax.ShapeDtypeStruct((nb, m, d), F32),
        compiler_params=_params(("parallel", "parallel")),
        name="matmul_gate_res",
    )(*a_list, *w_list, res, _mod_arg(gate))


def _mlp_kernel(x_ref, g_ref, sh_ref, sc_ref, gt_ref, w1_ref, w2_ref, gf_ref, o_ref, h_scr, acc_scr, *, final):
    f = pl.program_id(2)

    @pl.when(f == 0)
    def _():
        h_scr[...] = (_rms(x_ref[0], g_ref[...]) * (1.0 + sc_ref[0]) + sh_ref[0]).astype(BF16)
        acc_scr[...] = jnp.zeros_like(acc_scr)

    a = _dot(h_scr[...], w1_ref[...])
    a = jnp.square(jnp.maximum(a, 0.0)).astype(BF16)
    acc_scr[...] += _dot(a, w2_ref[...])

    @pl.when(f == pl.num_programs(2) - 1)
    def _():
        y = x_ref[0] + gt_ref[0] * acc_scr[...]
        if final:
            y = _rms(y, gf_ref[...])
        o_ref[0] = y


def mlp_block(x, g, shift, scale, gate, w1, w2, g_final, final, tm, tf):
    nb, m, d = x.shape
    dff = w1.shape[1]
    return pl.pallas_call(
        functools.partial(_mlp_kernel, final=final),
        grid=(nb, m // tm, dff // tf),
        in_specs=[pl.BlockSpec((1, tm, d), lambda b, i, f: (b, i, 0)),
                  pl.BlockSpec((1, d), lambda b, i, f: (0, 0)),
                  _mod_spec(shift, tm, 2), _mod_spec(scale, tm, 2), _mod_spec(gate, tm, 2),
                  pl.BlockSpec((d, tf), lambda b, i, f: (0, f)),
                  pl.BlockSpec((tf, d), lambda b, i, f: (f, 0)),
                  pl.BlockSpec((1, d), lambda b, i, f: (0, 0))],
        out_specs=pl.BlockSpec((1, tm, d), lambda b, i, f: (b, i, 0)),
        out_shape=jax.ShapeDtypeStruct((nb, m, d), F32),
        scratch_shapes=[pltpu.VMEM((tm, d), BF16), pltpu.VMEM((tm, d), F32)],
        compiler_params=_params(("parallel", "parallel", "arbitrary"), VMEM_LIMIT_MLP_MIB),
        name="mlp_block",
    )(x, g.reshape(1, d), _mod_arg(shift), _mod_arg(scale), _mod_arg(gate), w1, w2, g_final.reshape(1, d))


def _rope32(x, cf, sf):
    half = MLA_ROPE // 2
    sw = jnp.concatenate([x[:, half:], x[:, :half]], axis=1)
    return x * cf + sw * sf


def _eye_bf16(n):
    return jnp.where(lax.broadcasted_iota(jnp.int32, (n, n), 0) == lax.broadcasted_iota(jnp.int32, (n, n), 1),
                     1.0, 0.0).astype(BF16)


def _kvprep_kernel(zkv_ref, g_ref, cf_ref, sf_ref, lat_ref, kr_ref, kcat_ref, latt_ref):
    z = zkv_ref[0]
    lat = _rms(z[:, :KV_RANK], g_ref[...])
    kr = _rope32(z[:, KV_RANK:], cf_ref[0], sf_ref[0])
    lat_ref[0] = lat
    kr_ref[0] = kr
    lat_b = lat.astype(BF16)
    kcat_ref[0, :, :KV_RANK] = lat_b
    kcat_ref[0, :, KV_RANK:] = kr.astype(BF16)
    latt_ref[0] = _nt_dot(_eye_bf16(KV_RANK), lat_b).astype(BF16)


def kv_prep(zkv, g_kv, cf, sf, tm):
    nb, m, _ = zkv.shape
    return pl.pallas_call(
        _kvprep_kernel,
        grid=(nb, m // tm),
        in_specs=[pl.BlockSpec((1, tm, MLA_QK), lambda b, i: (b, i, 0)),
                  pl.BlockSpec((1, KV_RANK), lambda b, i: (0, 0)),
                  pl.BlockSpec((1, tm, MLA_ROPE), lambda b, i: (0, i, 0)),
                  pl.BlockSpec((1, tm, MLA_ROPE), lambda b, i: (0, i, 0))],
        out_specs=[pl.BlockSpec((1, tm, KV_RANK), lambda b, i: (b, i, 0)),
                   pl.BlockSpec((1, tm, MLA_ROPE), lambda b, i: (b, i, 0)),
                   pl.BlockSpec((1, tm, MLA_QK), lambda b, i: (b, i, 0)),
                   pl.BlockSpec((1, KV_RANK, tm), lambda b, i: (b, 0, i))],
        out_shape=[jax.ShapeDtypeStruct((nb, m, KV_RANK), F32),
                   jax.ShapeDtypeStruct((nb, m, MLA_ROPE), F32),
                   jax.ShapeDtypeStruct((nb, m, MLA_QK), BF16),
                   jax.ShapeDtypeStruct((nb, KV_RANK, m), BF16)],
        compiler_params=_params(("parallel", "parallel")),
        name="kv_prep",
    )(zkv, g_kv.reshape(1, KV_RANK), cf, sf)


def _qprep_kernel(zq_ref, wuk_ref, cf_ref, sf_ref, o_ref):
    z = zq_ref[0]
    cf = cf_ref[0]
    sf = sf_ref[0]
    nope_w = MLA_HEADS * MLA_NOPE
    for h in range(MLA_HEADS):
        qn = z[:, h * MLA_NOPE:(h + 1) * MLA_NOPE].astype(BF16)
        ql = _dot(qn, wuk_ref[h]) * MLA_QSCALE
        qr = _rope32(z[:, nope_w + h * MLA_ROPE:nope_w + (h + 1) * MLA_ROPE], cf, sf) * MLA_QSCALE
        o_ref[0, h, :, :KV_RANK] = ql.astype(BF16)
        o_ref[0, h, :, KV_RANK:] = qr.astype(BF16)


def _qprep_t_kernel(zq_ref, wuk_ref, cf_ref, sf_ref, o_ref, *, tq):
    z = zq_ref[0]
    cf = cf_ref[0]
    sf = sf_ref[0]
    nope_w = MLA_HEADS * MLA_NOPE
    eye = _eye_bf16(MLA_ROPE)
    for h in range(MLA_HEADS):
        qn = z[:, h * MLA_NOPE:(h + 1) * MLA_NOPE].astype(BF16)
        ql_t = _nt_dot(wuk_ref[h], qn) * MLA_QSCALE
        qr = _rope32(z[:, nope_w + h * MLA_ROPE:nope_w + (h + 1) * MLA_ROPE], cf, sf) * MLA_QSCALE
        qr_t = _nt_dot(eye, qr.astype(BF16))
        o_ref[0, 0, :KV_RANK, h * tq:(h + 1) * tq] = ql_t.astype(BF16)
        o_ref[0, 0, KV_RANK:, h * tq:(h + 1) * tq] = qr_t.astype(BF16)


def q_prep_t(zq, wuk_r, cf, sf, tq):
    nb, m, w = zq.shape
    return pl.pallas_call(
        functools.partial(_qprep_t_kernel, tq=tq),
        grid=(nb, m // tq),
        in_specs=[pl.BlockSpec((1, tq, w), lambda b, i: (b, i, 0)),
                  pl.BlockSpec((MLA_HEADS, KV_RANK, MLA_NOPE), lambda b, i: (0, 0, 0)),
                  pl.BlockSpec((1, tq, MLA_ROPE), lambda b, i: (0, i, 0)),
                  pl.BlockSpec((1, tq, MLA_ROPE), lambda b, i: (0, i, 0))],
        out_specs=pl.BlockSpec((1, 1, MLA_QK, MLA_HEADS * tq), lambda b, i: (b, i, 0, 0)),
        out_shape=jax.ShapeDtypeStruct((nb, m // tq, MLA_QK, MLA_HEADS * tq), BF16),
        compiler_params=_params(("parallel", "parallel")),
        name="q_prep_t",
    )(zq, wuk_r, cf, sf)


def q_prep(zq, wuk_t, cf, sf, tm):
    nb, m, w = zq.shape
    return pl.pallas_call(
        _qprep_kernel,
        grid=(nb, m // tm),
        in_specs=[pl.BlockSpec((1, tm, w), lambda b, i: (b, i, 0)),
                  pl.BlockSpec((MLA_HEADS, MLA_NOPE, KV_RANK), lambda b, i: (0, 0, 0)),
                  pl.BlockSpec((1, tm, MLA_ROPE), lambda b, i: (0, i, 0)),
                  pl.BlockSpec((1, tm, MLA_ROPE), lambda b, i: (0, i, 0))],
        out_specs=pl.BlockSpec((1, MLA_HEADS, tm, MLA_QK), lambda b, i: (b, 0, i, 0)),
        out_shape=jax.ShapeDtypeStruct((nb, MLA_HEADS, m, MLA_QK), BF16),
        compiler_params=_params(("parallel", "parallel")),
        name="q_prep",
    )(zq, wuk_t, cf, sf)


MLA_PROMPT_COL_WIDTH = 512
MLA_PROMPT_TQ = 512


def _mla_prompt_kernel(qi_ref, ki_ref, qt_ref, k_ref, latt_ref, wuvt_ref, o_ref, m_scr, l_scr, acc_scr, *, tq, tk):
    step = pl.program_id(1)
    qi = qi_ref[step]
    ki = ki_ref[step]
    last_k = (qi * tq + (tq - 1)) // tk

    @pl.when(ki == 0)
    def _():
        m_scr[...] = jnp.full_like(m_scr, -jnp.inf)
        l_scr[...] = jnp.zeros_like(l_scr)
        acc_scr[...] = jnp.zeros_like(acc_scr)

    def update(masked):
        rows = MLA_HEADS * tq
        cw = min(MLA_PROMPT_COL_WIDTH, rows)
        groups = [slice(g * cw, (g + 1) * cw) for g in range(rows // cw)]
        st_next = _dot(k_ref[0], qt_ref[0, 0, :, groups[0]])
        for g, cs in enumerate(groups):
            st = st_next
            if masked:
                kpos = ki * tk + lax.broadcasted_iota(jnp.int32, st.shape, 0)
                col = cs.start + lax.broadcasted_iota(jnp.int32, st.shape, 1)
                qpos = qi * tq + jnp.bitwise_and(col, tq - 1)
                st = jnp.where(kpos <= qpos, st, -jnp.inf)
            m_prev = m_scr[:, cs]
            m_new = jnp.maximum(m_prev, jnp.max(st, axis=0, keepdims=True))
            alpha = jnp.exp2(m_prev - m_new)
            pt = jnp.exp2(st - m_new)
            l_scr[:, cs] = alpha * l_scr[:, cs] + jnp.sum(pt, axis=0, keepdims=True)
            m_scr[:, cs] = m_new
            if g + 1 < len(groups):
                st_next = _dot(k_ref[0], qt_ref[0, 0, :, groups[g + 1]])
            acc_scr[:, cs] = alpha * acc_scr[:, cs] + _dot(latt_ref[0], pt.astype(BF16))

    needs_mask = ki * tk + (tk - 1) > qi * tq

    @pl.when(needs_mask)
    def _():
        update(True)

    @pl.when(jnp.logical_not(needs_mask))
    def _():
        update(False)

    @pl.when(ki == last_k)
    def _():
        ot = (acc_scr[...] / l_scr[...]).astype(BF16)
        for h in range(MLA_HEADS):
            o_ref[0, h * MLA_V:(h + 1) * MLA_V, :] = _dot(wuvt_ref[h], ot[:, h * tq:(h + 1) * tq]).astype(BF16)


def mla_prompt(qt, kcat, latt, wuv_t, tk):
    nb, nq, _, rows = qt.shape
    tq = rows // MLA_HEADS
    t = nq * tq
    tk = min(tk, t)
    pairs =[(i, j) for i in range(nq) for j in range((i * tq + tq - 1) // tk + 1)]
    qi_tab = jnp.asarray([p[0] for p in pairs], jnp.int32)
    ki_tab = jnp.asarray([p[1] for p in pairs], jnp.int32)
    grid_spec = pltpu.PrefetchScalarGridSpec(
        num_scalar_prefetch=2,
        grid=(nb, len(pairs)),
        in_specs=[pl.BlockSpec((1, 1, MLA_QK, rows), lambda b, s, qi, ki: (b, qi[s], 0, 0)),
                  pl.BlockSpec((1, tk, MLA_QK), lambda b, s, qi, ki: (b, ki[s], 0)),
                  pl.BlockSpec((1, KV_RANK, tk), lambda b, s, qi, ki: (b, 0, ki[s])),
                  pl.BlockSpec((MLA_HEADS, MLA_V, KV_RANK), lambda b, s, qi, ki: (0, 0, 0))],
        out_specs=pl.BlockSpec((1, MLA_HEADS * MLA_V, tq), lambda b, s, qi, ki: (b, 0, qi[s])),
        scratch_shapes=[pltpu.VMEM((1, rows), F32), pltpu.VMEM((1, rows), F32), pltpu.VMEM((KV_RANK, rows), F32)],
    )
    return pl.pallas_call(
        functools.partial(_mla_prompt_kernel, tq=tq, tk=tk),
        grid_spec=grid_spec,
        out_shape=jax.ShapeDtypeStruct((nb, MLA_HEADS * MLA_V, t), BF16),
        compiler_params=_params(("parallel", "arbitrary")),
        name="mla_prompt",
    )(qi_tab, ki_tab, qt, kcat, latt, wuv_t)


def _mla_sample_kernel(pt_ref, q_ref, kn_ref, wuv_ref, lat_hbm, kr_hbm, o_ref, kl_buf, kp_buf, sem,
                       *, layer, n_pages, n_pg, n_grp, n_slots, t_new):
    b = pl.program_id(0)
    rows = t_new * MLA_HEADS
    n_chunks = n_pages // n_pg
    per = n_pg // n_grp

    def page_copies(bb, c, slot):
        cps = []
        for i in range(n_pg):
            page = pt_ref[bb * n_pages + c * n_pg + i]
            cps.append(pltpu.make_async_copy(lat_hbm.at[layer, page], kl_buf.at[slot, i], sem.at[slot, 0]))
            cps.append(pltpu.make_async_copy(kr_hbm.at[layer, page], kp_buf.at[slot, i], sem.at[slot, 1]))
        return cps

    ahead = n_slots - 1

    @pl.when(b == 0)
    def _():
        for c0 in range(ahead):
            for cp in page_copies(0, c0, c0):
                cp.start()

    q = q_ref[0]
    ql = q[:, :KV_RANK]
    qr = q[:, KV_RANK:]
    m_g = [jnp.full((rows, 1), -jnp.inf, F32) for _ in range(n_grp)]
    l_g = [jnp.zeros((rows, 1), F32) for _ in range(n_grp)]
    acc_g = [jnp.zeros((rows, KV_RANK), F32) for _ in range(n_grp)]
    for c in range(n_chunks):
        slot = c % n_slots
        nxt = c + ahead
        if nxt < n_chunks:
            for cp in page_copies(b, nxt, nxt % n_slots):
                cp.start()
        else:
            @pl.when(b + 1 < pl.num_programs(0))
            def _():
                for cp in page_copies(b + 1, nxt - n_chunks, nxt % n_slots):
                    cp.start()
        for cp in page_copies(b, c, slot):
            cp.wait()
        kls = [kl_buf[slot, i].astype(BF16) for i in range(n_pg)]
        ss = [_nt_dot(ql, kls[i]) + _dot(qr, kp_buf[slot, i].astype(BF16)) for i in range(n_pg)]
        alphas, ps = [], []
        for gi in range(n_grp):
            s = jnp.concatenate(ss[gi * per:(gi + 1) * per], axis=1)
            m_new = jnp.maximum(m_g[gi], jnp.max(s, axis=1, keepdims=True))
            alpha = jnp.exp2(m_g[gi] - m_new)
            p = jnp.exp2(s - m_new).astype(BF16)
            l_g[gi] = alpha * l_g[gi] + jnp.sum(p.astype(F32), axis=1, keepdims=True)
            m_g[gi] = m_new
            alphas.append(alpha)
            ps.append(p)
        for gi in range(n_grp):
            pv = _dot(ps[gi][:, :PAGE_SIZE], kls[gi * per])
            for i in range(1, per):
                pv = pv + _dot(ps[gi][:, i * PAGE_SIZE:(i + 1) * PAGE_SIZE], kls[gi * per + i])
            acc_g[gi] = alphas[gi] * acc_g[gi] + pv

    qf = q.astype(F32)
    kn = kn_ref[0]
    trow = lax.broadcasted_iota(jnp.int32, (rows, 1), 0) // MLA_HEADS
    cols = []
    for jj in range(t_new):
        sj = jnp.sum(qf * kn[jj:jj + 1, :], axis=1, keepdims=True)
        cols.append(jnp.where(trow >= jj, sj, -jnp.inf))
    m1 = m_g[0]
    for gi in range(1, n_grp):
        m1 = jnp.maximum(m1, m_g[gi])
    for sj in cols:
        m1 = jnp.maximum(m1, sj)
    l1 = jnp.zeros_like(m1)
    acc1 = jnp.zeros((rows, KV_RANK), F32)
    for gi in range(n_grp):
        ag = jnp.exp2(m_g[gi] - m1)
        l1 = l1 + ag * l_g[gi]
        acc1 = acc1 + ag * acc_g[gi]
    for jj, sj in enumerate(cols):
        pj = jnp.exp2(sj - m1)
        l1 = l1 + pj
        acc1 = acc1 + pj * kn[jj:jj + 1, :KV_RANK]
    o = (acc1 / l1).astype(BF16)
    proj = _dot(o, wuv_ref[...])
    rr = lax.broadcasted_iota(jnp.int32, proj.shape, 0)
    cc = lax.broadcasted_iota(jnp.int32, proj.shape, 1)
    proj = jnp.where(jnp.bitwise_and(rr, MLA_HEADS - 1) == cc // MLA_V, proj, 0.0)
    o_ref[0] = jnp.sum(proj.reshape(t_new, MLA_HEADS, MLA_HEADS * MLA_V), axis=1).astype(BF16)


MLA_SAMPLE_PAGES_PER_CHUNK = 32
MLA_SAMPLE_SLOTS = 4
MLA_SAMPLE_GROUPS = 8


def mla_sample(q_s, kn_s, wuv_all, cache_lat, cache_kr, layer, page_table):
    nb, rows, _ = q_s.shape
    t_new = rows // MLA_HEADS
    n_pages = page_table.shape[1]
    n_pg = min(MLA_SAMPLE_PAGES_PER_CHUNK, n_pages // 2)
    n_slots = min(MLA_SAMPLE_SLOTS, n_pages // n_pg)
    assert n_pages % (n_slots * n_pg) == 0
    n_grp = math.gcd(n_pg, MLA_SAMPLE_GROUPS)
    grid_spec = pltpu.PrefetchScalarGridSpec(
        num_scalar_prefetch=1,
        grid=(nb,),
        in_specs=[pl.BlockSpec((1, rows, MLA_QK), lambda b, pt: (b, 0, 0)),
                  pl.BlockSpec((1, t_new, MLA_QK), lambda b, pt: (b, 0, 0)),
                  pl.BlockSpec((KV_RANK, MLA_HEADS * MLA_V), lambda b, pt: (0, 0)),
                  pl.BlockSpec(memory_space=pl.ANY),
                  pl.BlockSpec(memory_space=pl.ANY)],
        out_specs=pl.BlockSpec((1, t_new, MLA_HEADS * MLA_V), lambda b, pt: (b, 0, 0)),
        scratch_shapes=[pltpu.VMEM((n_slots, n_pg, PAGE_SIZE, KV_RANK), F32),
                        pltpu.VMEM((n_slots, n_pg, MLA_ROPE, PAGE_SIZE), F32),
                        pltpu.SemaphoreType.DMA((n_slots, 2))],
    )
    return pl.pallas_call(
        functools.partial(_mla_sample_kernel, layer=layer, n_pages=n_pages, n_pg=n_pg, n_grp=n_grp,
                          n_slots=n_slots, t_new=t_new),
        grid_spec=grid_spec,
        out_shape=jax.ShapeDtypeStruct((nb, t_new, MLA_HEADS * MLA_V), BF16),
        compiler_params=_params(("arbitrary",)),
        name="mla_sample",
    )(page_table.reshape(-1), q_s, kn_s, wuv_all, cache_lat, cache_kr)


RW_TENSORS = 7


def _shifted_rows(zr, carry):
    first = lax.broadcasted_iota(jnp.int32, zr.shape, 0) == 0
    prev = jnp.where(first, carry[...], pltpu.roll(zr, 1, 0))
    carry[...] = zr[zr.shape[0] - 1:, :]
    return prev


def _rw_prep_heads(zr, prev, prm, put):
    mu_ref, w0_ref, ww2_ref, a0_ref, wa2_ref, wg2_ref, kk_ref, ka_ref = prm
    zs = zr + (prev - zr) * mu_ref[...]
    o3 = 3 * RW_W
    o4 = o3 + RW_DECAY_LORA
    o5 = o4 + RW_A_LORA
    xr, xk, xv = zs[:, :RW_W], zs[:, RW_W:2 * RW_W], zs[:, 2 * RW_W:o3]
    xw, xa, xg = zs[:, o3:o4], zs[:, o4:o5], zs[:, o5:]
    wl = w0_ref[...] + _dot(jnp.tanh(xw).astype(BF16), ww2_ref[...])
    w_log = -(jnp.maximum(-wl, 0.0) + jnp.log1p(jnp.exp(-jnp.abs(wl)))) - 0.5
    logw = -jnp.exp(w_log)
    a = jax.nn.sigmoid(a0_ref[...] + _dot(xa.astype(BF16), wa2_ref[...]))
    g = _dot(jax.nn.sigmoid(xg).astype(BF16), wg2_ref[...])
    kkf = xk * kk_ref[...]
    kf = xk * (1.0 + (a - 1.0) * ka_ref[...])
    for h in range(RW_HEADS):
        sl = slice(h * RW_N, (h + 1) * RW_N)
        kkh = kkf[:, sl]
        kkh = kkh / jnp.maximum(jnp.sqrt(jnp.sum(kkh * kkh, axis=1, keepdims=True)), 1e-12)
        for i, val in enumerate((xr[:, sl], kf[:, sl], xv[:, sl], kkh, kkh * a[:, sl], logw[:, sl], g[:, sl])):
            put(i, h, val)


def _rwprep_kernel(zr_ref, pv_ref, *rest):
    prm = rest[:8]
    outs = rest[8:8 + RW_TENSORS]

    def put(i, h, val):
        outs[i][0, h] = val

    _rw_prep_heads(zr_ref[0], pv_ref[0], prm, put)


def rwkv_prep(zr, prev, mu, w0, w_w2, a0, w_a2, w_g2, k_k, k_a, tm):
    nb, m, w = zr.shape
    vec = lambda n: pl.BlockSpec((1, n), lambda b, i: (0, 0))
    mat = lambda a: pl.BlockSpec(a.shape, lambda b, i: (0, 0))
    out_spec = pl.BlockSpec((1, RW_HEADS, tm, RW_N), lambda b, i: (b, 0, i, 0))
    out_sds = jax.ShapeDtypeStruct((nb, RW_HEADS, m, RW_N), F32)
    return pl.pallas_call(
        _rwprep_kernel,
        grid=(nb, m // tm),
        in_specs=[pl.BlockSpec((1, tm, w), lambda b, i: (b, i, 0)),
                  pl.BlockSpec((1, tm, w), lambda b, i: (b, i, 0)),
                  vec(w), vec(RW_W), mat(w_w2), vec(RW_W), mat(w_a2), mat(w_g2), vec(RW_W), vec(RW_W)],
        out_specs=[out_spec] * RW_TENSORS,
        out_shape=[out_sds] * RW_TENSORS,
        compiler_params=_params(("parallel", "parallel")),
        name="rwkv_prep",
    )(zr, prev, mu.reshape(1, w), w0.reshape(1, RW_W), w_w2, a0.reshape(1, RW_W), w_a2, w_g2,
      k_k.reshape(1, RW_W), k_a.reshape(1, RW_W))


def _split_bf16(x, terms):
    parts = []
    rem = x
    for i in range(terms):
        p = rem.astype(BF16)
        parts.append(p)
        if i + 1 < terms:
            rem = rem - p.astype(F32)
    return parts


def _mm(a, b, ta, tb, dot=_dot):
    ap = _split_bf16(a, ta)
    bp = _split_bf16(b, tb)
    n = max(ta, tb)
    acc = None
    for i, x in enumerate(ap):
        for j, y in enumerate(bp):
            if i + j < n:
                d = dot(x, y)
                acc = d if acc is None else acc + d
    return acc


RW_P_CUMSUM = 2
RW_P_INTRA = 1
RW_P_INV = 1
RW_P_STATE = 1


def _rwkv_chunk(get, rk_ref, lnw_ref, lnb_ref, st_ref, chunk, n_elem):
    heads = range(n_elem * RW_HEADS)
    st_at = lambda h: (h // RW_HEADS, h % RW_HEADS)
    r_h, k_h, v_h, kk_h, kka_h, lw_h, g_h = ([get(i, h) for h in heads] for i in range(RW_TENSORS))
    c2 = 2 * chunk
    row = lax.broadcasted_iota(jnp.int32, (chunk, chunk), 0)
    col = lax.broadcasted_iota(jnp.int32, (chunk, chunk), 1)
    tri = jnp.where(col <= row, 1.0, 0.0).astype(BF16)
    eye_c = jnp.where(row == col, 1.0, 0.0).astype(F32)
    row2 = lax.broadcasted_iota(jnp.int32, (c2, c2), 0)
    col2 = jnp.bitwise_and(lax.broadcasted_iota(jnp.int32, (c2, c2), 1), chunk - 1)
    mask2 = col2 < jnp.where(row2 < chunk, row2, row2 - (chunk - 1))
    eye_n = lax.broadcasted_iota(jnp.int32, (RW_N, RW_N), 0) == lax.broadcasted_iota(jnp.int32, (RW_N, RW_N), 1)
    zeros_cn = jnp.zeros((chunk, RW_N), F32)
    n_double = int(math.log2(chunk)) - 1
    cs = [_mm(tri, lw_h[h], 1, RW_P_CUMSUM) for h in heads]
    lhs, rhs, g_end, g_end_col = [], [], [], []
    for h in heads:
        g_incl = jnp.exp(cs[h])
        g_prev = jnp.exp(cs[h] - lw_h[h])
        g_inv = jnp.exp(-cs[h])
        cs_last = cs[h][chunk - 1:chunk, :]
        g_end.append(jnp.exp(cs_last))
        g_end_col.append(jnp.exp(jnp.sum(jnp.where(eye_n, jnp.broadcast_to(cs_last, (RW_N, RW_N)), 0.0),
                                         axis=1, keepdims=True)))
        lhs.append(jnp.concatenate([-kk_h[h] * g_prev, r_h[h] * g_incl], axis=0))
        rhs.append(jnp.concatenate([kka_h[h] * g_inv, k_h[h] * g_inv], axis=0))
    mx = [jnp.where(mask2, _mm(lhs[h], rhs[h], RW_P_INTRA, RW_P_INTRA, _nt_dot), 0.0) for h in heads]
    from_state = [_mm(lhs[h], st_ref[st_at(h)], RW_P_STATE, RW_P_STATE) for h in heads]
    from_v = [_mm(mx[h], jnp.concatenate([zeros_cn, v_h[h]], axis=0), RW_P_INTRA, RW_P_INTRA)
              for h in heads]
    l_ab = [mx[h][:chunk, :chunk] for h in heads]
    tinv = [eye_c + l_ab[h] for h in heads]
    pw = [_mm(l_ab[h], l_ab[h], RW_P_INV, RW_P_INV) for h in heads]
    for _ in range(n_double - 1):
        both = [_mm(jnp.concatenate([pw[h], tinv[h]], axis=0), pw[h], RW_P_INV, RW_P_INV) for h in heads]
        tinv = [tinv[h] + both[h][chunk:] for h in heads]
        pw = [both[h][:chunk] for h in heads]
    tinv = [tinv[h] + _mm(tinv[h], pw[h], RW_P_INV, RW_P_INV) for h in heads]
    u = [_mm(tinv[h], from_state[h][:chunk] + from_v[h][:chunk], RW_P_INV, RW_P_INV) for h in heads]
    y_u = [_mm(mx[h][chunk:, :chunk], u[h], RW_P_INTRA, RW_P_INTRA) for h in heads]
    st_add = [_mm(rhs[h] * g_end[h], jnp.concatenate([u[h], v_h[h]], axis=0), RW_P_STATE, RW_P_STATE, _tn_dot)
              for h in heads]
    outs = []
    for h in heads:
        hp = h % RW_HEADS
        st_ref[st_at(h)] = st_ref[st_at(h)] * g_end_col[h] + st_add[h]
        y = from_state[h][chunk:] + from_v[h][chunk:] + y_u[h]
        mean = jnp.mean(y, axis=1, keepdims=True)
        yc = y - mean
        var = jnp.mean(yc * yc, axis=1, keepdims=True)
        yn = yc * lax.rsqrt(var + RW_GN_EPS) * lnw_ref[hp:hp + 1, :] + lnb_ref[hp:hp + 1, :]
        bonus = jnp.sum(r_h[h] * k_h[h] * rk_ref[hp:hp + 1, :], axis=1, keepdims=True) * v_h[h]
        outs.append((yn + bonus) * g_h[h])
    return outs


def _rwkv_fused_kernel(zr0_ref, zra_ref, zrb_ref, sp_ref, *rest, chunk, n_elem):
    prm = rest[:8]
    rk_ref, lnw_ref, lnb_ref, s0_ref, o_ref, st_ref, buf, carry = rest[8:]
    j = pl.program_id(1)

    def prep(zr_ref, slot):
        for e in range(n_elem):
            zr = zr_ref[e]
            prev = _shifted_rows(zr, carry.at[e])

            def put(i, h, val):
                buf[slot, i, e * RW_HEADS + h] = val

            _rw_prep_heads(zr, prev, prm, put)

    @pl.when(j == 0)
    def _():
        st_ref[...] = s0_ref[...]
        carry[...] = sp_ref[...]
        prep(zr0_ref, 0)

    for slot, nxt_ref in ((0, zra_ref), (1, zrb_ref)):
        outs = _rwkv_chunk(lambda i, h: buf[slot, i, h], rk_ref, lnw_ref, lnb_ref, st_ref, chunk, n_elem)
        for e in range(n_elem):
            out = jnp.concatenate(outs[e * RW_HEADS:(e + 1) * RW_HEADS], axis=1)
            o_ref[e, slot * chunk:(slot + 1) * chunk, :] = out.astype(BF16)
        prep(nxt_ref, 1 - slot)


RW_ELEMS_PER_STEP = 2


def rwkv_fused(zr, shift_prev, mu, w0, w_w2, a0, w_a2, w_g2, k_k, k_a, r_k, ln_w, ln_b, s0_t, chunk):
    nb, t, w = zr.shape
    nc = t // chunk
    assert nc % 2 == 0
    ne = RW_ELEMS_PER_STEP if nb % RW_ELEMS_PER_STEP == 0 else 1
    vec = lambda n: pl.BlockSpec((1, n), lambda b, j: (0, 0))
    mat = lambda a: pl.BlockSpec(a.shape, lambda b, j: (0, 0))
    hspec = pl.BlockSpec((RW_HEADS, RW_N), lambda b, j: (0, 0))
    sspec = pl.BlockSpec((ne, RW_HEADS, RW_N, RW_N), lambda b, j: (b, 0, 0, 0))
    return pl.pallas_call(
        functools.partial(_rwkv_fused_kernel, chunk=chunk, n_elem=ne),
        grid=(nb // ne, nc // 2),
        in_specs=[pl.BlockSpec((ne, chunk, w), lambda b, j: (b, 0, 0)),
                  pl.BlockSpec((ne, chunk, w), lambda b, j: (b, 2 * j + 1, 0)),
                  pl.BlockSpec((ne, chunk, w), lambda b, j: (b, jnp.minimum(2 * j + 2, nc - 1), 0)),
                  pl.BlockSpec((ne, 1, w), lambda b, j: (b, 0, 0)),
                  vec(w), vec(RW_W), mat(w_w2), vec(RW_W), mat(w_a2), mat(w_g2), vec(RW_W), vec(RW_W),
                  hspec, hspec, hspec, sspec],
        out_specs=[pl.BlockSpec((ne, 2 * chunk, RW_W), lambda b, j: (b, j, 0)), sspec],
        out_shape=[jax.ShapeDtypeStruct((nb, t, RW_W), BF16),
                   jax.ShapeDtypeStruct((nb, RW_HEADS, RW_N, RW_N), F32)],
        scratch_shapes=[pltpu.VMEM((2, RW_TENSORS, ne * RW_HEADS, chunk, RW_N), F32), pltpu.VMEM((ne, 1, w), F32)],
        compiler_params=_params(("parallel", "arbitrary")),
        name="rwkv_fused",
    )(zr, zr, zr, shift_prev, mu.reshape(1, w), w0.reshape(1, RW_W), w_w2, a0.reshape(1, RW_W), w_a2, w_g2,
      k_k.reshape(1, RW_W), k_a.reshape(1, RW_W), r_k, ln_w.reshape(RW_HEADS, RW_N), ln_b.reshape(RW_HEADS, RW_N),
      s0_t)


def _rwkv_step_kernel(r_ref, k_ref, v_ref, kk_ref, kka_ref, lw_ref, g_ref, rk_ref, lnw_ref, lnb_ref, s0_ref,
                      o_ref, s_ref, w_scr, y_scr, *, t_new):
    for t in range(t_new):
        w_scr[t] = jnp.exp(lw_ref[t, 0])

    def value_row(vi, carry):
        s = s0_ref[0, vi]
        for t in range(t_new):
            sa = -jnp.sum(s * kk_ref[t, 0], axis=0, keepdims=True)
            s = s * w_scr[t] + sa * kka_ref[t, 0] + v_ref[t, 0, pl.ds(vi, 1), :] * k_ref[t, 0]
            y_scr[t, pl.ds(vi, 1), :] = jnp.sum(s * r_ref[t, 0], axis=0, keepdims=True)
        s_ref[0, vi] = s
        return carry

    lax.fori_loop(0, RW_N, value_row, 0)
    for t in range(t_new):
        y = y_scr[t]
        mean = jnp.mean(y, axis=0, keepdims=True)
        yc = y - mean
        var = jnp.mean(yc * yc, axis=0, keepdims=True)
        yn = yc * lax.rsqrt(var + RW_GN_EPS) * lnw_ref[0] + lnb_ref[0]
        bonus = jnp.sum(r_ref[t, 0] * k_ref[t, 0] * rk_ref[0], axis=0, keepdims=True) * v_ref[t, 0]
        o_ref[t, 0] = (yn + bonus) * g_ref[t, 0]


def rwkv_step(r, k, v, kk, kka, lw, g, r_k, ln_w, ln_b, s0):
    t_new, nh, n, nb = r.shape
    tspec = pl.BlockSpec((t_new, 1, n, nb), lambda h: (0, h, 0, 0))
    hspec = pl.BlockSpec((1, n, nb), lambda h: (h, 0, 0))
    sspec = pl.BlockSpec((1, n, n, nb), lambda h: (h, 0, 0, 0))
    return pl.pallas_call(
        functools.partial(_rwkv_step_kernel, t_new=t_new),
        grid=(nh,),
        in_specs=[tspec] * 7 + [hspec] * 3 + [sspec],
        out_specs=[tspec, sspec],
        out_shape=[jax.ShapeDtypeStruct((t_new, nh, n, nb), F32), jax.ShapeDtypeStruct((nh, n, n, nb), F32)],
        scratch_shapes=[pltpu.VMEM((t_new, n, nb), F32), pltpu.VMEM((t_new, n, nb), F32)],
        compiler_params=_params(("parallel",)),
        name="rwkv_step",
    )(r, k, v, kk, kka, lw, g, r_k, ln_w, ln_b, s0)


def _retention_kernel(lg_ref, qk_ref, v_ref, g_ref, cos_ref, sin_ref, s0_ref, o_ref, s_ref, *, lb, l_true):
    c = pl.program_id(1)

    @pl.when(c == 0)
    def _():
        s_ref[0] = s0_ref[0]

    lp = max(lb, 16)
    cos = cos_ref[0]
    sin = sin_ref[0]
    half = RET_DK // 2
    qk_w = RET_HEADS * RET_DK

    def rope(x):
        x1, x2 = x[:, :half], x[:, half:]
        return jnp.concatenate([x1 * cos - x2 * sin, x1 * sin + x2 * cos], axis=1)

    def rows(x):
        if lp == lb:
            return x
        return jnp.concatenate([x, jnp.zeros((lp - lb, x.shape[1]), x.dtype)], axis=0)

    row = lax.broadcasted_iota(jnp.int32, (lp, lp), 0)
    col = lax.broadcasted_iota(jnp.int32, (lp, lp), 1)
    diff = (row - col).astype(F32)
    idx = lax.broadcasted_iota(jnp.int32, (lp, 1), 0).astype(F32)
    heads = range(RET_HEADS)
    qm, km, kdm, vm, dmask, row_dec = [], [], [], [], [], []
    for h in heads:
        lg = lg_ref[h]
        q = rows(rope(qk_ref[0, :, h * RET_DK:(h + 1) * RET_DK]))
        k = rows(rope(qk_ref[0, :, qk_w + h * RET_DK:qk_w + (h + 1) * RET_DK]) * (RET_DK ** -0.5))
        qm.append(q.astype(BF16))
        km.append(k.astype(BF16))
        kdm.append((k * jnp.exp((l_true - 1.0 - idx) * lg)).astype(BF16))
        vm.append(rows(v_ref[0, :, h * RET_DV:(h + 1) * RET_DV]).astype(BF16))
        dmask.append(jnp.where(diff >= 0, jnp.exp(jnp.maximum(diff, 0.0) * lg), 0.0))
        row_dec.append(jnp.exp((idx + 1.0) * lg))
    sc = [(_nt_dot(qm[h], km[h]) * dmask[h]).astype(BF16) for h in heads]
    cross = [_dot(qm[h], s_ref[0, h].astype(BF16)) * row_dec[h] for h in heads]
    s_add = [_tn_dot(kdm[h], vm[h]) for h in heads]
    inner = [_dot(sc[h], vm[h]) for h in heads]
    outs = []
    for h in heads:
        s_dec = jnp.exp(jnp.zeros((1, RET_DV), F32) + l_true * lg_ref[h])
        s_ref[0, h] = s_ref[0, h] * s_dec + s_add[h]
        o = (inner[h] + cross[h])[:lb]
        o = o * lax.rsqrt(jnp.mean(o * o, axis=1, keepdims=True) + NORM_EPS)
        gv = g_ref[0, :, h * RET_DV:(h + 1) * RET_DV]
        outs.append(o * (gv * jax.nn.sigmoid(gv)))
    o_ref[0] = jnp.concatenate(outs, axis=1).astype(BF16)


def retention(z, cos, sin, lg, s0, lb, l_true):
    nb, m, _ = z.shape
    vw = RET_HEADS * RET_DV
    assert 2 * RET_HEADS * RET_DK == vw
    sspec = pl.BlockSpec((1, RET_HEADS, RET_DK, RET_DV), lambda b, c: (b, 0, 0, 0))
    return pl.pallas_call(
        functools.partial(_retention_kernel, lb=lb, l_true=float(l_true)),
        grid=(nb, m // lb),
        in_specs=[pl.BlockSpec(memory_space=pltpu.SMEM),
                  pl.BlockSpec((1, lb, vw), lambda b, c: (b, c, 0)),
                  pl.BlockSpec((1, lb, vw), lambda b, c: (b, c, 1)),
                  pl.BlockSpec((1, lb, vw), lambda b, c: (b, c, 2)),
                  pl.BlockSpec((1, lb, RET_DK // 2), lambda b, c: (0, c, 0)),
                  pl.BlockSpec((1, lb, RET_DK // 2), lambda b, c: (0, c, 0)),
                  sspec],
        out_specs=[pl.BlockSpec((1, lb, vw), lambda b, c: (b, c, 0)), sspec],
        out_shape=[jax.ShapeDtypeStruct((nb, m, vw), BF16),
                   jax.ShapeDtypeStruct((nb, RET_HEADS, RET_DK, RET_DV), F32)],
        compiler_params=_params(("parallel", "arbitrary")),
        name="retention",
    )(lg, z, z, z, cos, sin, s0)


def _rope_tables(pos, half):
    inv = ROPE_BASE ** (-jnp.arange(half, dtype=F32) / half)
    ang = pos.astype(F32)[:, None] * inv[None, :]
    return jnp.cos(ang), jnp.sin(ang)


def _mla_tables(pos):
    cos, sin = _rope_tables(pos, MLA_ROPE // 2)
    return jnp.concatenate([cos, cos], axis=1), jnp.concatenate([-sin, sin], axis=1)


def _even_layer(x, mods, pos_tabs, prm, past, tm):
    (w_in_p, g_mix, g_kv, wuk_t, wuk_r, wuv_t, wuv_all, mu, w0, w_w2, a0, w_a2, w_g2, k_k, k_a, r_k, ln_w, ln_b,
     w_out_mla, w_out_rw) = prm
    sh1, sc1, gt1 = mods
    cf, sf = pos_tabs
    nb, m, _ = x.shape
    zr, zq, zkv = norm_mod_matmul_split(x, g_mix, sh1, sc1, w_in_p, (RW_SHIFT_W, MLA_HEADS * (MLA_NOPE + MLA_ROPE), MLA_QK), tm)
    lat, kr, kcat, latt = kv_prep(zkv, g_kv, cf, sf, tm)
    if past is None:
        qt = q_prep_t(zq, wuk_r, cf, sf, min(MLA_PROMPT_TQ, m))
        mla_out = mla_prompt(qt, kcat, latt, wuv_t, 512)
        mla_transposed = True
        s0_t = jnp.zeros((nb, RW_HEADS, RW_N, RW_N), F32)
        rw_out, s_t = rwkv_fused(zr, jnp.zeros((nb, 1, RW_SHIFT_W), F32), mu, w0, w_w2, a0, w_a2, w_g2, k_k, k_a,
                                 r_k, ln_w, ln_b, s0_t, RW_CHUNK)
        s_new = jnp.swapaxes(s_t, -1, -2)
        shift_new = zr[:, -1]
    else:
        cache_lat, cache_kr, layer, page_table, s0, shift_prev, t_new = past
        nbs = m // t_new
        mla_transposed = False
        qcat = q_prep(zq, wuk_t, cf, sf, tm)
        q_s = qcat.reshape(MLA_HEADS, nbs, t_new, MLA_QK).transpose(1, 2, 0, 3).reshape(nbs, t_new * MLA_HEADS, MLA_QK)
        kn_s = jnp.concatenate([lat, kr], axis=-1).reshape(nbs, t_new, MLA_QK)
        mla_out = mla_sample(q_s, kn_s, wuv_all, cache_lat, cache_kr, layer, page_table).reshape(1, m, MLA_HEADS * MLA_V)
        zr_b = zr.reshape(nbs, t_new, RW_SHIFT_W)
        prev = jnp.concatenate([shift_prev[:, None, :], zr_b[:, :-1]], axis=1).reshape(1, m, RW_SHIFT_W)
        tens = rwkv_prep(zr, prev, mu, w0, w_w2, a0, w_a2, w_g2, k_k, k_a, min(tm, 256))
        tens = [u.reshape(RW_HEADS, nbs, t_new, RW_N).transpose(2, 0, 3, 1) for u in tens]
        lanes = lambda p: jnp.broadcast_to(p.reshape(RW_HEADS, RW_N, 1), (RW_HEADS, RW_N, nbs))
        rw_l, s_l = rwkv_step(*tens, lanes(r_k), lanes(ln_w), lanes(ln_b), jnp.transpose(s0, (1, 2, 3, 0)))
        rw_out = rw_l.transpose(3, 0, 1, 2).reshape(1, m, RW_W).astype(BF16)
        s_new = jnp.transpose(s_l, (3, 0, 1, 2))
        shift_new = zr_b[:, -1]
    x_new = matmul_gate_res([mla_out, rw_out], [w_out_mla, w_out_rw], x, gt1, min(2 * tm, m), (mla_transposed, False))
    return x_new, (lat, kr, s_new, shift_new)


def _odd_layer(x, mods, ret_tabs, prm, s0, t_new, tm):
    w_in, g_mix, w_out, lg = prm
    sh1, sc1, gt1 = mods
    cos, sin = ret_tabs
    nb, m, _ = x.shape
    z = norm_mod_matmul(x, g_mix, sh1, sc1, w_in, min(2 * tm, m), 2048)
    if s0 is None:
        s0 = jnp.zeros((nb, RET_HEADS, RET_DK, RET_DV), F32)
        lb = min(RET_BLOCK, m)
        o, s_new = retention(z, cos, sin, lg, s0, lb, lb)
    else:
        nbs = m // t_new
        lpad = 8
        z_b = jnp.pad(z.reshape(nbs, t_new, -1), ((0, 0), (0, lpad - t_new), (0, 0)))
        o, s_new = retention(z_b, cos, sin, lg, s0, lpad, t_new)
        o = o[:, :t_new].reshape(1, m, RET_HEADS * RET_DV)
    x_new = matmul_gate_res([o], [w_out], x, gt1, min(2 * tm, m))
    return x_new, s_new


def kernel(x_prompt, x_sample, c_prompt, c_sample, cache_kv_latent, cache_k_rope, page_table, state_rwkv, state_rwkv_shift, state_ret, w_ada, b_ada, g_norm_mix, g_norm_mlp, g_final, w_in_even, g_kv, w_uk, w_uv, rw_mu, rw_w0, rw_w2, rw_a0, rw_a2, rw_g2, rw_k_k, rw_k_a, rw_r_k, rw_ln_w, rw_ln_b, w_out_even, w_in_odd, w_out_odd, w_ff1, w_ff2):
    nbp, t_p, d = x_prompt.shape
    nbs, t_s, _ = x_sample.shape
    depth = w_ada.shape[0]
    past_len = page_table.shape[1] * PAGE_SIZE
    m_s = nbs * t_s
    tm_p = min(512, t_p)
    tm_s = m_s

    c_all = jnp.concatenate([jnp.repeat(c_sample, t_s, axis=0), c_prompt], axis=0)
    c_all = jnp.pad(c_all, ((0, -c_all.shape[0] % 16), (0, 0)))
    mods_all = ada_proj(c_all, w_ada, b_ada)

    def group_mods(l):
        mp = mods_all[l, m_s:m_s + nbp].reshape(nbp, 1, 6, d)
        return [mp[:, :, i] for i in range(6)], [(mods_all, l, i) for i in range(6)]

    pos_p = jnp.arange(t_p)
    pos_s = past_len + jnp.arange(t_s)
    cf_p, sf_p = _mla_tables(pos_p)
    cf_s, sf_s = _mla_tables(pos_s)
    mla_tabs_p = (cf_p[None], sf_p[None])
    mla_tabs_s = (jnp.tile(cf_s, (nbs, 1))[None], jnp.tile(sf_s, (nbs, 1))[None])
    cr_p, sr_p = _rope_tables(pos_p, RET_DK // 2)
    cr_s, sr_s = _rope_tables(pos_s, RET_DK // 2)
    ret_tabs_p = (cr_p[None], sr_p[None])
    ret_tabs_s = (jnp.pad(cr_s, ((0, 8 - t_s), (0, 0)))[None], jnp.pad(sr_s, ((0, 8 - t_s), (0, 0)))[None])
    lg = jnp.log(1 - 2.0 ** (-5.0 - jnp.arange(RET_HEADS, dtype=F32)))

    xp = x_prompt
    xs = x_sample.reshape(1, m_s, d)
    lat_p, kr_p, rw_p, sh_p, ret_p = [], [], [], [], []
    lat_s, kr_s, rw_s, sh_s, ret_s = [], [], [], [], []
    q_w = MLA_HEADS * (MLA_NOPE + MLA_ROPE)
    for l in range(depth):
        (sh1p, sc1p, gt1p, sh2p, sc2p, gt2p), (sh1s, sc1s, gt1s, sh2s, sc2s, gt2s) = group_mods(l)
        i = l // 2
        if l % 2 == 0:
            w_in = w_in_even[i]
            wq = w_in[:, :q_w].reshape(d, MLA_HEADS, MLA_NOPE + MLA_ROPE)
            w_in_p = jnp.concatenate([w_in[:, q_w + MLA_QK:],
                                      wq[:, :, :MLA_NOPE].reshape(d, -1), wq[:, :, MLA_NOPE:].reshape(d, -1),
                                      w_in[:, q_w:q_w + MLA_QK]], axis=1).astype(BF16)
            wuv = w_uv[i]
            mla_w = MLA_HEADS * MLA_V
            prm = (w_in_p, g_norm_mix[l], g_kv[i], jnp.transpose(w_uk[i], (1, 2, 0)).astype(BF16),
                   jnp.transpose(w_uk[i], (1, 0, 2)).astype(BF16), jnp.transpose(wuv, (1, 2, 0)).astype(BF16), wuv.reshape(KV_RANK, mla_w).astype(BF16),
                   rw_mu[i], rw_w0[i], rw_w2[i].astype(BF16), rw_a0[i], rw_a2[i].astype(BF16), rw_g2[i].astype(BF16),
                   rw_k_k[i], rw_k_a[i], rw_r_k[i], rw_ln_w[i], rw_ln_b[i],
                   w_out_even[i, :mla_w].astype(BF16), w_out_even[i, mla_w:].astype(BF16))
            xp, (la, kr, st, sh) = _even_layer(xp, (sh1p, sc1p, gt1p), mla_tabs_p, prm, None, tm_p)
            lat_p.append(la); kr_p.append(kr); rw_p.append(st); sh_p.append(sh)
            cache_kr_t = jnp.swapaxes(cache_k_rope, 2, 3)
            past = (cache_kv_latent, cache_kr_t, i, page_table, state_rwkv[i], state_rwkv_shift[i], t_s)
            xs, (la, kr, st, sh) = _even_layer(xs, (sh1s, sc1s, gt1s), mla_tabs_s, prm, past, tm_s)
            lat_s.append(la.reshape(nbs, t_s, KV_RANK)); kr_s.append(kr.reshape(nbs, t_s, MLA_ROPE))
            rw_s.append(st); sh_s.append(sh)
        else:
            prm = (w_in_odd[i].astype(BF16), g_norm_mix[l], w_out_odd[i].astype(BF16), lg)
            xp, st = _odd_layer(xp, (sh1p, sc1p, gt1p), ret_tabs_p, prm, None, t_s, tm_p)
            ret_p.append(st)
            xs, st = _odd_layer(xs, (sh1s, sc1s, gt1s), ret_tabs_s, prm, state_ret[i], t_s, tm_s)
            ret_s.append(st)
        final = l == depth - 1
        w1 = w_ff1[l].astype(BF16)
        w2 = w_ff2[l].astype(BF16)
        xp = mlp_block(xp, g_norm_mlp[l], sh2p, sc2p, gt2p, w1, w2, g_final, final, min(2 * tm_p, t_p), 1024)
        xs = mlp_block(xs, g_norm_mlp[l], sh2s, sc2s, gt2s, w1, w2, g_final, final, tm_s, 1024)
    return (xp, xs.reshape(nbs, t_s, d),
            jnp.stack(lat_p), jnp.stack(kr_p), jnp.stack(rw_p), jnp.stack(sh_p), jnp.stack(ret_p),
            jnp.stack(lat_s), jnp.stack(kr_s), jnp.stack(rw_s), jnp.stack(sh_s), jnp.stack(ret_s))
```

```python
import functools
import math

import jax
import jax.numpy as jnp
from jax import lax
from jax.experimental import pallas as pl
from jax.experimental.pallas import tpu as pltpu

F32 = jnp.float32
BF16 = jnp.bfloat16

D_MODEL = 1024
PAGE_SIZE = 128
MLA_HEADS = 8
MLA_NOPE = 64
MLA_ROPE = 32
MLA_V = 64
KV_RANK = 256
MLA_QK = KV_RANK + MLA_ROPE
MLA_SCALE = (MLA_NOPE + MLA_ROPE) ** -0.5
MLA_QSCALE = MLA_SCALE * math.log2(math.e)
RW_HEADS = 8
RW_N = 64
RW_W = RW_HEADS * RW_N
RW_DECAY_LORA = 64
RW_A_LORA = 64
RW_G_LORA = 128
RW_SHIFT_W = 3 * RW_W + RW_DECAY_LORA + RW_A_LORA + RW_G_LORA
RW_GN_EPS = 64e-5
RW_CHUNK = 64
RET_HEADS = 4
RET_DK = 256
RET_DV = 512
RET_BLOCK = 256
D_FF = 4 * D_MODEL
ROPE_BASE = 10000.0
NORM_EPS = 1e-6
MIB = 1024 * 1024


V7X_VMEM_MIB = 64
VMEM_LIMIT_MIB = V7X_VMEM_MIB - 16
VMEM_LIMIT_MLP_MIB = V7X_VMEM_MIB - 8


def _params(sem, vmem_mib=VMEM_LIMIT_MIB):
    return pltpu.CompilerParams(dimension_semantics=sem, vmem_limit_bytes=vmem_mib * MIB)


def _rms(x, g):
    return x * lax.rsqrt(jnp.mean(x * x, axis=-1, keepdims=True) + NORM_EPS) * g


def _nt_dot(a, b, precision=None):
    return lax.dot_general(a, b, (((1,), (1,)), ((), ())), precision=precision, preferred_element_type=F32)


def _tn_dot(a, b, precision=None):
    return lax.dot_general(a, b, (((0,), (0,)), ((), ())), precision=precision, preferred_element_type=F32)


def _dot(a, b, precision=None):
    return jnp.dot(a, b, precision=precision, preferred_element_type=F32)


def _mod_arg(mod):
    return mod[0] if isinstance(mod, tuple) else mod


def _mod_spec(mod, tm, nmid):
    if isinstance(mod, tuple):
        _, layer, which = mod
        if nmid == 2:
            return pl.BlockSpec((1, tm, D_MODEL), lambda b, m, j: (layer, m, which))
        return pl.BlockSpec((1, tm, D_MODEL), lambda b, m: (layer, m, which))
    if mod.shape[1] == 1:
        if nmid == 2:
            return pl.BlockSpec((1, 1, mod.shape[2]), lambda b, m, j: (b, 0, 0))
        return pl.BlockSpec((1, 1, mod.shape[2]), lambda b, m: (b, 0, 0))
    if nmid == 2:
        return pl.BlockSpec((1, tm, mod.shape[2]), lambda b, m, j: (b, m, 0))
    return pl.BlockSpec((1, tm, mod.shape[2]), lambda b, m: (b, m, 0))


def _ada_kernel(c_ref, w_ref, b_ref, o_ref):
    o_ref[0] = _dot(c_ref[...].astype(BF16), w_ref[0].astype(BF16)) + b_ref[0]


def ada_proj(c, w_ada, b_ada):
    nl, d, n = w_ada.shape
    r = c.shape[0]
    tn = 1536
    return pl.pallas_call(
        _ada_kernel,
        grid=(nl, n // tn),
        in_specs=[pl.BlockSpec((r, d), lambda l, j: (0, 0)),
                  pl.BlockSpec((1, d, tn), lambda l, j: (l, 0, j)),
                  pl.BlockSpec((1, 1, tn), lambda l, j: (l, 0, j))],
        out_specs=pl.BlockSpec((1, r, tn), lambda l, j: (l, 0, j)),
        out_shape=jax.ShapeDtypeStruct((nl, r, n), F32),
        compiler_params=_params(("parallel", "parallel")),
        name="ada_proj",
    )(c, w_ada, b_ada.reshape(nl, 1, n))


def _nmm_split_kernel(x_ref, g_ref, sh_ref, sc_ref, w_ref, *o_refs, splits):
    h = (_rms(x_ref[0], g_ref[...]) * (1.0 + sc_ref[0]) + sh_ref[0]).astype(BF16)
    off = 0
    for o_ref, n in zip(o_refs, splits):
        o_ref[0] = _dot(h, w_ref[:, off:off + n])
        off += n


def norm_mod_matmul_split(x, g, shift, scale, w, splits, tm):
    nb, m, d = x.shape
    n = w.shape[1]
    return pl.pallas_call(
        functools.partial(_nmm_split_kernel, splits=splits),
        grid=(nb, m // tm),
        in_specs=[pl.BlockSpec((1, tm, d), lambda b, i: (b, i, 0)),
                  pl.BlockSpec((1, d), lambda b, i: (0, 0)),
                  _mod_spec(shift, tm, 1), _mod_spec(scale, tm, 1),
                  pl.BlockSpec((d, n), lambda b, i: (0, 0))],
        out_specs=[pl.BlockSpec((1, tm, s), lambda b, i: (b, i, 0)) for s in splits],
        out_shape=[jax.ShapeDtypeStruct((nb, m, s), F32) for s in splits],
        compiler_params=_params(("parallel", "parallel")),
        name="norm_mod_matmul_split",
    )(x, g.reshape(1, d), _mod_arg(shift), _mod_arg(scale), w)


def _nmm_kernel(x_ref, g_ref, sh_ref, sc_ref, w_ref, o_ref):
    h = (_rms(x_ref[0], g_ref[...]) * (1.0 + sc_ref[0]) + sh_ref[0]).astype(BF16)
    o_ref[0] = _dot(h, w_ref[...])


def norm_mod_matmul(x, g, shift, scale, w, tm, tn):
    nb, m, d = x.shape
    n = w.shape[1]
    return pl.pallas_call(
        _nmm_kernel,
        grid=(nb, m // tm, n // tn),
        in_specs=[pl.BlockSpec((1, tm, d), lambda b, i, j: (b, i, 0)),
                  pl.BlockSpec((1, d), lambda b, i, j: (0, 0)),
                  _mod_spec(shift, tm, 2), _mod_spec(scale, tm, 2),
                  pl.BlockSpec((d, tn), lambda b, i, j: (0, j))],
        out_specs=pl.BlockSpec((1, tm, tn), lambda b, i, j: (b, i, j)),
        out_shape=jax.ShapeDtypeStruct((nb, m, n), F32),
        compiler_params=_params(("parallel", "parallel", "arbitrary")),
        name="norm_mod_matmul",
    )(x, g.reshape(1, d), _mod_arg(shift), _mod_arg(scale), w)


def _mgr_kernel(*refs, n_pairs, transposed):
    a_refs = refs[:n_pairs]
    w_refs = refs[n_pairs:2 * n_pairs]
    res_ref, gt_ref, o_ref = refs[2 * n_pairs:]
    acc = None
    for a_ref, w_ref, tr in zip(a_refs, w_refs, transposed):
        d = (_tn_dot if tr else _dot)(a_ref[0], w_ref[...])
        acc = d if acc is None else acc + d
    o_ref[0] = res_ref[0] + gt_ref[0] * acc


def matmul_gate_res(a_list, w_list, res, gate, tm, transposed=None):
    nb, m, d = res.shape
    n_pairs = len(a_list)
    transposed = tuple(transposed or (False,) * n_pairs)
    in_specs = [pl.BlockSpec((1, a.shape[1], tm), lambda b, i: (b, 0, i)) if tr else
                pl.BlockSpec((1, tm, a.shape[2]), lambda b, i: (b, i, 0)) for a, tr in zip(a_list, transposed)]
    in_specs += [pl.BlockSpec(w.shape, lambda b, i: (0, 0)) for w in w_list]
    in_specs += [pl.BlockSpec((1, tm, d), lambda b, i: (b, i, 0)), _mod_spec(gate, tm, 1)]
    return pl.pallas_call(
        functools.partial(_mgr_kernel, n_pairs=n_pairs, transposed=transposed),
        grid=(nb, m // tm),
        in_specs=in_specs,
        out_specs=pl.BlockSpec((1, tm, d), lambda b, i: (b, i, 0)),
        out_shape=jax.ShapeDtypeStruct((nb, m, d), F32),
        compiler_params=_params(("parallel", "parallel")),
        name="matmul_gate_res",
    )(*a_list, *w_list, res, _mod_arg(gate))


def _mlp_kernel(x_ref, g_ref, sh_ref, sc_ref, gt_ref, w1_ref, w2_ref, gf_ref, o_ref, h_scr, acc_scr, *, final):
    f = pl.program_id(2)

    @pl.when(f == 0)
    def _():
        h_scr[...] = (_rms(x_ref[0], g_ref[...]) * (1.0 + sc_ref[0]) + sh_ref[0]).astype(BF16)
        acc_scr[...] = jnp.zeros_like(acc_scr)

    a = _dot(h_scr[...], w1_ref[...])
    a = jnp.square(jnp.maximum(a, 0.0)).astype(BF16)
    acc_scr[...] += _dot(a, w2_ref[...])

    @pl.when(f == pl.num_programs(2) - 1)
    def _():
        y = x_ref[0] + gt_ref[0] * acc_scr[...]
        if final:
            y = _rms(y, gf_ref[...])
        o_ref[0] = y


def mlp_block(x, g, shift, scale, gate, w1, w2, g_final, final, tm, tf):
    nb, m, d = x.shape
    dff = w1.shape[1]
    return pl.pallas_call(
        functools.partial(_mlp_kernel, final=final),
        grid=(nb, m // tm, dff // tf),
        in_specs=[pl.BlockSpec((1, tm, d), lambda b, i, f: (b, i, 0)),
                  pl.BlockSpec((1, d), lambda b, i, f: (0, 0)),
                  _mod_spec(shift, tm, 2), _mod_spec(scale, tm, 2), _mod_spec(gate, tm, 2),
                  pl.BlockSpec((d, tf), lambda b, i, f: (0, f)),
                  pl.BlockSpec((tf, d), lambda b, i, f: (f, 0)),
                  pl.BlockSpec((1, d), lambda b, i, f: (0, 0))],
        out_specs=pl.BlockSpec((1, tm, d), lambda b, i, f: (b, i, 0)),
        out_shape=jax.ShapeDtypeStruct((nb, m, d), F32),
        scratch_shapes=[pltpu.VMEM((tm, d), BF16), pltpu.VMEM((tm, d), F32)],
        compiler_params=_params(("parallel", "parallel", "arbitrary"), VMEM_LIMIT_MLP_MIB),
        name="mlp_block",
    )(x, g.reshape(1, d), _mod_arg(shift), _mod_arg(scale), _mod_arg(gate), w1, w2, g_final.reshape(1, d))


def _rope32(x, cf, sf):
    half = MLA_ROPE // 2
    sw = jnp.concatenate([x[:, half:], x[:, :half]], axis=1)
    return x * cf + sw * sf


def _eye_bf16(n):
    return jnp.where(lax.broadcasted_iota(jnp.int32, (n, n), 0) == lax.broadcasted_iota(jnp.int32, (n, n), 1),
                     1.0, 0.0).astype(BF16)


def _kvprep_kernel(zkv_ref, g_ref, cf_ref, sf_ref, lat_ref, kr_ref, kcat_ref, latt_ref):
    z = zkv_ref[0]
    lat = _rms(z[:, :KV_RANK], g_ref[...])
    kr = _rope32(z[:, KV_RANK:], cf_ref[0], sf_ref[0])
    lat_ref[0] = lat
    kr_ref[0] = kr
    lat_b = lat.astype(BF16)
    kcat_ref[0, :, :KV_RANK] = lat_b
    kcat_ref[0, :, KV_RANK:] = kr.astype(BF16)
    latt_ref[0] = _nt_dot(_eye_bf16(KV_RANK), lat_b).astype(BF16)


def kv_prep(zkv, g_kv, cf, sf, tm):
    nb, m, _ = zkv.shape
    return pl.pallas_call(
        _kvprep_kernel,
        grid=(nb, m // tm),
        in_specs=[pl.BlockSpec((1, tm, MLA_QK), lambda b, i: (b, i, 0)),
                  pl.BlockSpec((1, KV_RANK), lambda b, i: (0, 0)),
                  pl.BlockSpec((1, tm, MLA_ROPE), lambda b, i: (0, i, 0)),
                  pl.BlockSpec((1, tm, MLA_ROPE), lambda b, i: (0, i, 0))],
        out_specs=[pl.BlockSpec((1, tm, KV_RANK), lambda b, i: (b, i, 0)),
                   pl.BlockSpec((1, tm, MLA_ROPE), lambda b, i: (b, i, 0)),
                   pl.BlockSpec((1, tm, MLA_QK), lambda b, i: (b, i, 0)),
                   pl.BlockSpec((1, KV_RANK, tm), lambda b, i: (b, 0, i))],
        out_shape=[jax.ShapeDtypeStruct((nb, m, KV_RANK), F32),
                   jax.ShapeDtypeStruct((nb, m, MLA_ROPE), F32),
                   jax.ShapeDtypeStruct((nb, m, MLA_QK), BF16),
                   jax.ShapeDtypeStruct((nb, KV_RANK, m), BF16)],
        compiler_params=_params(("parallel", "parallel")),
        name="kv_prep",
    )(zkv, g_kv.reshape(1, KV_RANK), cf, sf)


def _qprep_kernel(zq_ref, wuk_ref, cf_ref, sf_ref, o_ref):
    z = zq_ref[0]
    cf = cf_ref[0]
    sf = sf_ref[0]
    nope_w = MLA_HEADS * MLA_NOPE
    for h in range(MLA_HEADS):
        qn = z[:, h * MLA_NOPE:(h + 1) * MLA_NOPE].astype(BF16)
        ql = _dot(qn, wuk_ref[h]) * MLA_QSCALE
        qr = _rope32(z[:, nope_w + h * MLA_ROPE:nope_w + (h + 1) * MLA_ROPE], cf, sf) * MLA_QSCALE
        o_ref[0, h, :, :KV_RANK] = ql.astype(BF16)
        o_ref[0, h, :, KV_RANK:] = qr.astype(BF16)


def _qprep_t_kernel(zq_ref, wuk_ref, cf_ref, sf_ref, o_ref, *, tq):
    z = zq_ref[0]
    cf = cf_ref[0]
    sf = sf_ref[0]
    nope_w = MLA_HEADS * MLA_NOPE
    eye = _eye_bf16(MLA_ROPE)
    for h in range(MLA_HEADS):
        qn = z[:, h * MLA_NOPE:(h + 1) * MLA_NOPE].astype(BF16)
        ql_t = _nt_dot(wuk_ref[h], qn) * MLA_QSCALE
        qr = _rope32(z[:, nope_w + h * MLA_ROPE:nope_w + (h + 1) * MLA_ROPE], cf, sf) * MLA_QSCALE
        qr_t = _nt_dot(eye, qr.astype(BF16))
        o_ref[0, 0, :KV_RANK, h * tq:(h + 1) * tq] = ql_t.astype(BF16)
        o_ref[0, 0, KV_RANK:, h * tq:(h + 1) * tq] = qr_t.astype(BF16)


def q_prep_t(zq, wuk_r, cf, sf, tq):
    nb, m, w = zq.shape
    return pl.pallas_call(
        functools.partial(_qprep_t_kernel, tq=tq),
        grid=(nb, m // tq),
        in_specs=[pl.BlockSpec((1, tq, w), lambda b, i: (b, i, 0)),
                  pl.BlockSpec((MLA_HEADS, KV_RANK, MLA_NOPE), lambda b, i: (0, 0, 0)),
                  pl.BlockSpec((1, tq, MLA_ROPE), lambda b, i: (0, i, 0)),
                  pl.BlockSpec((1, tq, MLA_ROPE), lambda b, i: (0, i, 0))],
        out_specs=pl.BlockSpec((1, 1, MLA_QK, MLA_HEADS * tq), lambda b, i: (b, i, 0, 0)),
        out_shape=jax.ShapeDtypeStruct((nb, m // tq, MLA_QK, MLA_HEADS * tq), BF16),
        compiler_params=_params(("parallel", "parallel")),
        name="q_prep_t",
    )(zq, wuk_r, cf, sf)


def q_prep(zq, wuk_t, cf, sf, tm):
    nb, m, w = zq.shape
    return pl.pallas_call(
        _qprep_kernel,
        grid=(nb, m // tm),
        in_specs=[pl.BlockSpec((1, tm, w), lambda b, i: (b, i, 0)),
                  pl.BlockSpec((MLA_HEADS, MLA_NOPE, KV_RANK), lambda b, i: (0, 0, 0)),
                  pl.BlockSpec((1, tm, MLA_ROPE), lambda b, i: (0, i, 0)),
                  pl.BlockSpec((1, tm, MLA_ROPE), lambda b, i: (0, i, 0))],
        out_specs=pl.BlockSpec((1, MLA_HEADS, tm, MLA_QK), lambda b, i: (b, 0, i, 0)),
        out_shape=jax.ShapeDtypeStruct((nb, MLA_HEADS, m, MLA_QK), BF16),
        compiler_params=_params(("parallel", "parallel")),
        name="q_prep",
    )(zq, wuk_t, cf, sf)


MLA_PROMPT_COL_WIDTH = 512
MLA_PROMPT_TQ = 512


def _mla_prompt_kernel(qi_ref, ki_ref, qt_ref, k_ref, latt_ref, wuvt_ref, o_ref, m_scr, l_scr, acc_scr, *, tq, tk):
    step = pl.program_id(1)
    qi = qi_ref[step]
    ki = ki_ref[step]
    last_k = (qi * tq + (tq - 1)) // tk

    @pl.when(ki == 0)
    def _():
        m_scr[...] = jnp.full_like(m_scr, -jnp.inf)
        l_scr[...] = jnp.zeros_like(l_scr)
        acc_scr[...] = jnp.zeros_like(acc_scr)

    def update(masked):
        rows = MLA_HEADS * tq
        cw = min(MLA_PROMPT_COL_WIDTH, rows)
        groups = [slice(g * cw, (g + 1) * cw) for g in range(rows // cw)]
        st_next = _dot(k_ref[0], qt_ref[0, 0, :, groups[0]])
        for g, cs in enumerate(groups):
            st = st_next
            if masked:
                kpos = ki * tk + lax.broadcasted_iota(jnp.int32, st.shape, 0)
                col = cs.start + lax.broadcasted_iota(jnp.int32, st.shape, 1)
                qpos = qi * tq + jnp.bitwise_and(col, tq - 1)
                st = jnp.where(kpos <= qpos, st, -jnp.inf)
            m_prev = m_scr[:, cs]
            m_new = jnp.maximum(m_prev, jnp.max(st, axis=0, keepdims=True))
            alpha = jnp.exp2(m_prev - m_new)
            pt = jnp.exp2(st - m_new)
            l_scr[:, cs] = alpha * l_scr[:, cs] + jnp.sum(pt, axis=0, keepdims=True)
            m_scr[:, cs] = m_new
            if g + 1 < len(groups):
                st_next = _dot(k_ref[0], qt_ref[0, 0, :, groups[g + 1]])
            acc_scr[:, cs] = alpha * acc_scr[:, cs] + _dot(latt_ref[0], pt.astype(BF16))

    needs_mask = ki * tk + (tk - 1) > qi * tq

    @pl.when(needs_mask)
    def _():
        update(True)

    @pl.when(jnp.logical_not(needs_mask))
    def _():
        update(False)

    @pl.when(ki == last_k)
    def _():
        ot = (acc_scr[...] / l_scr[...]).astype(BF16)
        for h in range(MLA_HEADS):
            o_ref[0, h * MLA_V:(h + 1) * MLA_V, :] = _dot(wuvt_ref[h], ot[:, h * tq:(h + 1) * tq]).astype(BF16)


def mla_prompt(qt, kcat, latt, wuv_t, tk):
    nb, nq, _, rows = qt.shape
    tq = rows // MLA_HEADS
    t = nq * tq
    tk = min(tk, t)
    pairs =[(i, j) for i in range(nq) for j in range((i * tq + tq - 1) // tk + 1)]
    qi_tab = jnp.asarray([p[0] for p in pairs], jnp.int32)
    ki_tab = jnp.asarray([p[1] for p in pairs], jnp.int32)
    grid_spec = pltpu.PrefetchScalarGridSpec(
        num_scalar_prefetch=2,
        grid=(nb, len(pairs)),
        in_specs=[pl.BlockSpec((1, 1, MLA_QK, rows), lambda b, s, qi, ki: (b, qi[s], 0, 0)),
                  pl.BlockSpec((1, tk, MLA_QK), lambda b, s, qi, ki: (b, ki[s], 0)),
                  pl.BlockSpec((1, KV_RANK, tk), lambda b, s, qi, ki: (b, 0, ki[s])),
                  pl.BlockSpec((MLA_HEADS, MLA_V, KV_RANK), lambda b, s, qi, ki: (0, 0, 0))],
        out_specs=pl.BlockSpec((1, MLA_HEADS * MLA_V, tq), lambda b, s, qi, ki: (b, 0, qi[s])),
        scratch_shapes=[pltpu.VMEM((1, rows), F32), pltpu.VMEM((1, rows), F32), pltpu.VMEM((KV_RANK, rows), F32)],
    )
    return pl.pallas_call(
        functools.partial(_mla_prompt_kernel, tq=tq, tk=tk),
        grid_spec=grid_spec,
        out_shape=jax.ShapeDtypeStruct((nb, MLA_HEADS * MLA_V, t), BF16),
        compiler_params=_params(("parallel", "arbitrary")),
        name="mla_prompt",
    )(qi_tab, ki_tab, qt, kcat, latt, wuv_t)


def _mla_sample_kernel(pt_ref, q_ref, kn_ref, wuv_ref, lat_hbm, kr_hbm, o_ref, kl_buf, kp_buf, sem,
                       *, layer, n_pages, n_pg, n_grp, n_slots, t_new):
    b = pl.program_id(0)
    rows = t_new * MLA_HEADS
    n_chunks = n_pages // n_pg
    per = n_pg // n_grp

    def page_copies(bb, c, slot):
        cps = []
        for i in range(n_pg):
            page = pt_ref[bb * n_pages + c * n_pg + i]
            cps.append(pltpu.make_async_copy(lat_hbm.at[layer, page], kl_buf.at[slot, i], sem.at[slot, 0]))
            cps.append(pltpu.make_async_copy(kr_hbm.at[layer, page], kp_buf.at[slot, i], sem.at[slot, 1]))
        return cps

    def start_all(cps):
        for n, cp in enumerate(cps):
            cp.start(priority=n % 2)

    ahead = n_slots - 1

    @pl.when(b == 0)
    def _():
        for c0 in range(ahead):
            start_all(page_copies(0, c0, c0))

    q = q_ref[0]
    ql = q[:, :KV_RANK]
    qr = q[:, KV_RANK:]
    m_g = [jnp.full((rows, 1), -jnp.inf, F32) for _ in range(n_grp)]
    l_g = [jnp.zeros((rows, 1), F32) for _ in range(n_grp)]
    acc_g = [jnp.zeros((rows, KV_RANK), F32) for _ in range(n_grp)]
    for c in range(n_chunks):
        slot = c % n_slots
        nxt = c + ahead
        if nxt < n_chunks:
            start_all(page_copies(b, nxt, nxt % n_slots))
        else:
            @pl.when(b + 1 < pl.num_programs(0))
            def _():
                start_all(page_copies(b + 1, nxt - n_chunks, nxt % n_slots))
        for cp in page_copies(b, c, slot):
            cp.wait()
        kls = [kl_buf[slot, i].astype(BF16) for i in range(n_pg)]
        ss = [_nt_dot(ql, kls[i]) + _dot(qr, kp_buf[slot, i].astype(BF16)) for i in range(n_pg)]
        alphas, ps = [], []
        for gi in range(n_grp):
            s = jnp.concatenate(ss[gi * per:(gi + 1) * per], axis=1)
            m_new = jnp.maximum(m_g[gi], jnp.max(s, axis=1, keepdims=True))
            alpha = jnp.exp2(m_g[gi] - m_new)
            p = jnp.exp2(s - m_new).astype(BF16)
            l_g[gi] = alpha * l_g[gi] + jnp.sum(p.astype(F32), axis=1, keepdims=True)
            m_g[gi] = m_new
            alphas.append(alpha)
            ps.append(p)
        for gi in range(n_grp):
            pv = _dot(ps[gi][:, :PAGE_SIZE], kls[gi * per])
            for i in range(1, per):
                pv = pv + _dot(ps[gi][:, i * PAGE_SIZE:(i + 1) * PAGE_SIZE], kls[gi * per + i])
            acc_g[gi] = alphas[gi] * acc_g[gi] + pv

    qf = q.astype(F32)
    kn = kn_ref[0]
    trow = lax.broadcasted_iota(jnp.int32, (rows, 1), 0) // MLA_HEADS
    cols = []
    for jj in range(t_new):
        sj = jnp.sum(qf * kn[jj:jj + 1, :], axis=1, keepdims=True)
        cols.append(jnp.where(trow >= jj, sj, -jnp.inf))
    m1 = m_g[0]
    for gi in range(1, n_grp):
        m1 = jnp.maximum(m1, m_g[gi])
    for sj in cols:
        m1 = jnp.maximum(m1, sj)
    l1 = jnp.zeros_like(m1)
    acc1 = jnp.zeros((rows, KV_RANK), F32)
    for gi in range(n_grp):
        ag = jnp.exp2(m_g[gi] - m1)
        l1 = l1 + ag * l_g[gi]
        acc1 = acc1 + ag * acc_g[gi]
    for jj, sj in enumerate(cols):
        pj = jnp.exp2(sj - m1)
        l1 = l1 + pj
        acc1 = acc1 + pj * kn[jj:jj + 1, :KV_RANK]
    o = (acc1 / l1).astype(BF16)
    proj = _dot(o, wuv_ref[...])
    rr = lax.broadcasted_iota(jnp.int32, proj.shape, 0)
    cc = lax.broadcasted_iota(jnp.int32, proj.shape, 1)
    proj = jnp.where(jnp.bitwise_and(rr, MLA_HEADS - 1) == cc // MLA_V, proj, 0.0)
    o_ref[0] = jnp.sum(proj.reshape(t_new, MLA_HEADS, MLA_HEADS * MLA_V), axis=1).astype(BF16)


MLA_SAMPLE_PAGES_PER_CHUNK = 32
MLA_SAMPLE_SLOTS = 4
MLA_SAMPLE_GROUPS = 8


def mla_sample(q_s, kn_s, wuv_all, cache_lat, cache_kr, layer, page_table):
    nb, rows, _ = q_s.shape
    t_new = rows // MLA_HEADS
    n_pages = page_table.shape[1]
    n_pg = min(MLA_SAMPLE_PAGES_PER_CHUNK, n_pages // 2)
    n_slots = min(MLA_SAMPLE_SLOTS, n_pages // n_pg)
    assert n_pages % (n_slots * n_pg) == 0
    n_grp = math.gcd(n_pg, MLA_SAMPLE_GROUPS)
    grid_spec = pltpu.PrefetchScalarGridSpec(
        num_scalar_prefetch=1,
        grid=(nb,),
        in_specs=[pl.BlockSpec((1, rows, MLA_QK), lambda b, pt: (b, 0, 0)),
                  pl.BlockSpec((1, t_new, MLA_QK), lambda b, pt: (b, 0, 0)),
                  pl.BlockSpec((KV_RANK, MLA_HEADS * MLA_V), lambda b, pt: (0, 0)),
                  pl.BlockSpec(memory_space=pl.ANY),
                  pl.BlockSpec(memory_space=pl.ANY)],
        out_specs=pl.BlockSpec((1, t_new, MLA_HEADS * MLA_V), lambda b, pt: (b, 0, 0)),
        scratch_shapes=[pltpu.VMEM((n_slots, n_pg, PAGE_SIZE, KV_RANK), F32),
                        pltpu.VMEM((n_slots, n_pg, MLA_ROPE, PAGE_SIZE), F32),
                        pltpu.SemaphoreType.DMA((n_slots, 2))],
    )
    return pl.pallas_call(
        functools.partial(_mla_sample_kernel, layer=layer, n_pages=n_pages, n_pg=n_pg, n_grp=n_grp,
                          n_slots=n_slots, t_new=t_new),
        grid_spec=grid_spec,
        out_shape=jax.ShapeDtypeStruct((nb, t_new, MLA_HEADS * MLA_V), BF16),
        compiler_params=_params(("arbitrary",)),
        name="mla_sample",
    )(page_table.reshape(-1), q_s, kn_s, wuv_all, cache_lat, cache_kr)


RW_TENSORS = 7


def _shifted_rows(zr, carry):
    first = lax.broadcasted_iota(jnp.int32, zr.shape, 0) == 0
    prev = jnp.where(first, carry[...], pltpu.roll(zr, 1, 0))
    carry[...] = zr[zr.shape[0] - 1:, :]
    return prev


def _rw_prep_heads(zr, prev, prm, put):
    mu_ref, w0_ref, ww2_ref, a0_ref, wa2_ref, wg2_ref, kk_ref, ka_ref = prm
    zs = zr + (prev - zr) * mu_ref[...]
    o3 = 3 * RW_W
    o4 = o3 + RW_DECAY_LORA
    o5 = o4 + RW_A_LORA
    xr, xk, xv = zs[:, :RW_W], zs[:, RW_W:2 * RW_W], zs[:, 2 * RW_W:o3]
    xw, xa, xg = zs[:, o3:o4], zs[:, o4:o5], zs[:, o5:]
    wl = w0_ref[...] + _dot(jnp.tanh(xw).astype(BF16), ww2_ref[...])
    w_log = -(jnp.maximum(-wl, 0.0) + jnp.log1p(jnp.exp(-jnp.abs(wl)))) - 0.5
    logw = -jnp.exp(w_log)
    a = jax.nn.sigmoid(a0_ref[...] + _dot(xa.astype(BF16), wa2_ref[...]))
    g = _dot(jax.nn.sigmoid(xg).astype(BF16), wg2_ref[...])
    kkf = xk * kk_ref[...]
    kf = xk * (1.0 + (a - 1.0) * ka_ref[...])
    for h in range(RW_HEADS):
        sl = slice(h * RW_N, (h + 1) * RW_N)
        kkh = kkf[:, sl]
        kkh = kkh / jnp.maximum(jnp.sqrt(jnp.sum(kkh * kkh, axis=1, keepdims=True)), 1e-12)
        for i, val in enumerate((xr[:, sl], kf[:, sl], xv[:, sl], kkh, kkh * a[:, sl], logw[:, sl], g[:, sl])):
            put(i, h, val)


def _rwprep_kernel(zr_ref, pv_ref, *rest):
    prm = rest[:8]
    outs = rest[8:8 + RW_TENSORS]

    def put(i, h, val):
        outs[i][0, h] = val

    _rw_prep_heads(zr_ref[0], pv_ref[0], prm, put)


def rwkv_prep(zr, prev, mu, w0, w_w2, a0, w_a2, w_g2, k_k, k_a, tm):
    nb, m, w = zr.shape
    vec = lambda n: pl.BlockSpec((1, n), lambda b, i: (0, 0))
    mat = lambda a: pl.BlockSpec(a.shape, lambda b, i: (0, 0))
    out_spec = pl.BlockSpec((1, RW_HEADS, tm, RW_N), lambda b, i: (b, 0, i, 0))
    out_sds = jax.ShapeDtypeStruct((nb, RW_HEADS, m, RW_N), F32)
    return pl.pallas_call(
        _rwprep_kernel,
        grid=(nb, m // tm),
        in_specs=[pl.BlockSpec((1, tm, w), lambda b, i: (b, i, 0)),
                  pl.BlockSpec((1, tm, w), lambda b, i: (b, i, 0)),
                  vec(w), vec(RW_W), mat(w_w2), vec(RW_W), mat(w_a2), mat(w_g2), vec(RW_W), vec(RW_W)],
        out_specs=[out_spec] * RW_TENSORS,
        out_shape=[out_sds] * RW_TENSORS,
        compiler_params=_params(("parallel", "parallel")),
        name="rwkv_prep",
    )(zr, prev, mu.reshape(1, w), w0.reshape(1, RW_W), w_w2, a0.reshape(1, RW_W), w_a2, w_g2,
      k_k.reshape(1, RW_W), k_a.reshape(1, RW_W))


def _split_bf16(x, terms):
    parts = []
    rem = x
    for i in range(terms):
        p = rem.astype(BF16)
        parts.append(p)
        if i + 1 < terms:
            rem = rem - p.astype(F32)
    return parts


def _mm(a, b, ta, tb, dot=_dot):
    ap = _split_bf16(a, ta)
    bp = _split_bf16(b, tb)
    n = max(ta, tb)
    acc = None
    for i, x in enumerate(ap):
        for j, y in enumerate(bp):
            if i + j < n:
                d = dot(x, y)
                acc = d if acc is None else acc + d
    return acc


RW_P_CUMSUM = 2
RW_P_INTRA = 1
RW_P_INV = 1
RW_P_STATE = 1


def _rwkv_chunk(get, rk_ref, lnw_ref, lnb_ref, st_ref, chunk, n_elem):
    heads = range(n_elem * RW_HEADS)
    st_at = lambda h: (h // RW_HEADS, h % RW_HEADS)
    r_h, k_h, v_h, kk_h, kka_h, lw_h, g_h = ([get(i, h) for h in heads] for i in range(RW_TENSORS))
    c2 = 2 * chunk
    row = lax.broadcasted_iota(jnp.int32, (chunk, chunk), 0)
    col = lax.broadcasted_iota(jnp.int32, (chunk, chunk), 1)
    tri = jnp.where(col <= row, 1.0, 0.0).astype(BF16)
    eye_c = jnp.where(row == col, 1.0, 0.0).astype(F32)
    row2 = lax.broadcasted_iota(jnp.int32, (c2, c2), 0)
    col2 = jnp.bitwise_and(lax.broadcasted_iota(jnp.int32, (c2, c2), 1), chunk - 1)
    mask2 = col2 < jnp.where(row2 < chunk, row2, row2 - (chunk - 1))
    eye_n = lax.broadcasted_iota(jnp.int32, (RW_N, RW_N), 0) == lax.broadcasted_iota(jnp.int32, (RW_N, RW_N), 1)
    zeros_cn = jnp.zeros((chunk, RW_N), F32)
    n_double = int(math.log2(chunk)) - 1
    cs = [_mm(tri, lw_h[h], 1, RW_P_CUMSUM) for h in heads]
    lhs, rhs, g_end, g_end_col = [], [], [], []
    for h in heads:
        g_incl = jnp.exp(cs[h])
        g_prev = jnp.exp(cs[h] - lw_h[h])
        g_inv = jnp.exp(-cs[h])
        cs_last = cs[h][chunk - 1:chunk, :]
        g_end.append(jnp.exp(cs_last))
        g_end_col.append(jnp.exp(jnp.sum(jnp.where(eye_n, jnp.broadcast_to(cs_last, (RW_N, RW_N)), 0.0),
                                         axis=1, keepdims=True)))
        lhs.append(jnp.concatenate([-kk_h[h] * g_prev, r_h[h] * g_incl], axis=0))
        rhs.append(jnp.concatenate([kka_h[h] * g_inv, k_h[h] * g_inv], axis=0))
    mx = [jnp.where(mask2, _mm(lhs[h], rhs[h], RW_P_INTRA, RW_P_INTRA, _nt_dot), 0.0) for h in heads]
    from_state = [_mm(lhs[h], st_ref[st_at(h)], RW_P_STATE, RW_P_STATE) for h in heads]
    from_v = [_mm(mx[h], jnp.concatenate([zeros_cn, v_h[h]], axis=0), RW_P_INTRA, RW_P_INTRA)
              for h in heads]
    l_ab = [mx[h][:chunk, :chunk] for h in heads]
    tinv = [eye_c + l_ab[h] for h in heads]
    pw = [_mm(l_ab[h], l_ab[h], RW_P_INV, RW_P_INV) for h in heads]
    for _ in range(n_double - 1):
        both = [_mm(jnp.concatenate([pw[h], tinv[h]], axis=0), pw[h], RW_P_INV, RW_P_INV) for h in heads]
        tinv = [tinv[h] + both[h][chunk:] for h in heads]
        pw = [both[h][:chunk] for h in heads]
    tinv = [tinv[h] + _mm(tinv[h], pw[h], RW_P_INV, RW_P_INV) for h in heads]
    u = [_mm(tinv[h], from_state[h][:chunk] + from_v[h][:chunk], RW_P_INV, RW_P_INV) for h in heads]
    y_u = [_mm(mx[h][chunk:, :chunk], u[h], RW_P_INTRA, RW_P_INTRA) for h in heads]
    st_add = [_mm(rhs[h] * g_end[h], jnp.concatenate([u[h], v_h[h]], axis=0), RW_P_STATE, RW_P_STATE, _tn_dot)
              for h in heads]
    outs = []
    for h in heads:
        hp = h % RW_HEADS
        st_ref[st_at(h)] = st_ref[st_at(h)] * g_end_col[h] + st_add[h]
        y = from_state[h][chunk:] + from_v[h][chunk:] + y_u[h]
        mean = jnp.mean(y, axis=1, keepdims=True)
        yc = y - mean
        var = jnp.mean(yc * yc, axis=1, keepdims=True)
        yn = yc * lax.rsqrt(var + RW_GN_EPS) * lnw_ref[hp:hp + 1, :] + lnb_ref[hp:hp + 1, :]
        bonus = jnp.sum(r_h[h] * k_h[h] * rk_ref[hp:hp + 1, :], axis=1, keepdims=True) * v_h[h]
        outs.append((yn + bonus) * g_h[h])
    return outs


def _rwkv_fused_kernel(zr0_ref, zra_ref, zrb_ref, sp_ref, *rest, chunk, n_elem):
    prm = rest[:8]
    rk_ref, lnw_ref, lnb_ref, s0_ref, o_ref, st_ref, buf, carry = rest[8:]
    j = pl.program_id(1)

    def prep(zr_ref, slot):
        for e in range(n_elem):
            zr = zr_ref[e]
            prev = _shifted_rows(zr, carry.at[e])

            def put(i, h, val):
                buf[slot, i, e * RW_HEADS + h] = val

            _rw_prep_heads(zr, prev, prm, put)

    @pl.when(j == 0)
    def _():
        st_ref[...] = s0_ref[...]
        carry[...] = sp_ref[...]
        prep(zr0_ref, 0)

    for slot, nxt_ref in ((0, zra_ref), (1, zrb_ref)):
        outs = _rwkv_chunk(lambda i, h: buf[slot, i, h], rk_ref, lnw_ref, lnb_ref, st_ref, chunk, n_elem)
        for e in range(n_elem):
            out = jnp.concatenate(outs[e * RW_HEADS:(e + 1) * RW_HEADS], axis=1)
            o_ref[e, slot * chunk:(slot + 1) * chunk, :] = out.astype(BF16)
        prep(nxt_ref, 1 - slot)


RW_ELEMS_PER_STEP = 2


def rwkv_fused(zr, shift_prev, mu, w0, w_w2, a0, w_a2, w_g2, k_k, k_a, r_k, ln_w, ln_b, s0_t, chunk):
    nb, t, w = zr.shape
    nc = t // chunk
    assert nc % 2 == 0
    ne = RW_ELEMS_PER_STEP if nb % RW_ELEMS_PER_STEP == 0 else 1
    vec = lambda n: pl.BlockSpec((1, n), lambda b, j: (0, 0))
    mat = lambda a: pl.BlockSpec(a.shape, lambda b, j: (0, 0))
    hspec = pl.BlockSpec((RW_HEADS, RW_N), lambda b, j: (0, 0))
    sspec = pl.BlockSpec((ne, RW_HEADS, RW_N, RW_N), lambda b, j: (b, 0, 0, 0))
    return pl.pallas_call(
        functools.partial(_rwkv_fused_kernel, chunk=chunk, n_elem=ne),
        grid=(nb // ne, nc // 2),
        in_specs=[pl.BlockSpec((ne, chunk, w), lambda b, j: (b, 0, 0)),
                  pl.BlockSpec((ne, chunk, w), lambda b, j: (b, 2 * j + 1, 0)),
                  pl.BlockSpec((ne, chunk, w), lambda b, j: (b, jnp.minimum(2 * j + 2, nc - 1), 0)),
                  pl.BlockSpec((ne, 1, w), lambda b, j: (b, 0, 0)),
                  vec(w), vec(RW_W), mat(w_w2), vec(RW_W), mat(w_a2), mat(w_g2), vec(RW_W), vec(RW_W),
                  hspec, hspec, hspec, sspec],
        out_specs=[pl.BlockSpec((ne, 2 * chunk, RW_W), lambda b, j: (b, j, 0)), sspec],
        out_shape=[jax.ShapeDtypeStruct((nb, t, RW_W), BF16),
                   jax.ShapeDtypeStruct((nb, RW_HEADS, RW_N, RW_N), F32)],
        scratch_shapes=[pltpu.VMEM((2, RW_TENSORS, ne * RW_HEADS, chunk, RW_N), F32), pltpu.VMEM((ne, 1, w), F32)],
        compiler_params=_params(("parallel", "arbitrary")),
        name="rwkv_fused",
    )(zr, zr, zr, shift_prev, mu.reshape(1, w), w0.reshape(1, RW_W), w_w2, a0.reshape(1, RW_W), w_a2, w_g2,
      k_k.reshape(1, RW_W), k_a.reshape(1, RW_W), r_k, ln_w.reshape(RW_HEADS, RW_N), ln_b.reshape(RW_HEADS, RW_N),
      s0_t)


def _rwkv_step_kernel(r_ref, k_ref, v_ref, kk_ref, kka_ref, lw_ref, g_ref, rk_ref, lnw_ref, lnb_ref, s0_ref,
                      o_ref, s_ref, w_scr, y_scr, *, t_new):
    for t in range(t_new):
        w_scr[t] = jnp.exp(lw_ref[t, 0])

    def value_row(vi, carry):
        s = s0_ref[0, vi]
        for t in range(t_new):
            sa = -jnp.sum(s * kk_ref[t, 0], axis=0, keepdims=True)
            s = s * w_scr[t] + sa * kka_ref[t, 0] + v_ref[t, 0, pl.ds(vi, 1), :] * k_ref[t, 0]
            y_scr[t, pl.ds(vi, 1), :] = jnp.sum(s * r_ref[t, 0], axis=0, keepdims=True)
        s_ref[0, vi] = s
        return carry

    lax.fori_loop(0, RW_N, value_row, 0, unroll=8)
    for t in range(t_new):
        y = y_scr[t]
        mean = jnp.mean(y, axis=0, keepdims=True)
        yc = y - mean
        var = jnp.mean(yc * yc, axis=0, keepdims=True)
        yn = yc * lax.rsqrt(var + RW_GN_EPS) * lnw_ref[0] + lnb_ref[0]
        bonus = jnp.sum(r_ref[t, 0] * k_ref[t, 0] * rk_ref[0], axis=0, keepdims=True) * v_ref[t, 0]
        o_ref[t, 0] = (yn + bonus) * g_ref[t, 0]


def rwkv_step(r, k, v, kk, kka, lw, g, r_k, ln_w, ln_b, s0):
    t_new, nh, n, nb = r.shape
    tspec = pl.BlockSpec((t_new, 1, n, nb), lambda h: (0, h, 0, 0))
    hspec = pl.BlockSpec((1, n, nb), lambda h: (h, 0, 0))
    sspec = pl.BlockSpec((1, n, n, nb), lambda h: (h, 0, 0, 0))
    return pl.pallas_call(
        functools.partial(_rwkv_step_kernel, t_new=t_new),
        grid=(nh,),
        in_specs=[tspec] * 7 + [hspec] * 3 + [sspec],
        out_specs=[tspec, sspec],
        out_shape=[jax.ShapeDtypeStruct((t_new, nh, n, nb), F32), jax.ShapeDtypeStruct((nh, n, n, nb), F32)],
        scratch_shapes=[pltpu.VMEM((t_new, n, nb), F32), pltpu.VMEM((t_new, n, nb), F32)],
        compiler_params=_params(("parallel",)),
        name="rwkv_step",
    )(r, k, v, kk, kka, lw, g, r_k, ln_w, ln_b, s0)


def _retention_kernel(lg_ref, qk_ref, v_ref, g_ref, cos_ref, sin_ref, s0_ref, o_ref, s_ref, *, lb, l_true):
    c = pl.program_id(1)

    @pl.when(c == 0)
    def _():
        s_ref[0] = s0_ref[0]

    lp = max(lb, 16)
    cos = cos_ref[0]
    sin = sin_ref[0]
    half = RET_DK // 2
    qk_w = RET_HEADS * RET_DK

    def rope(x):
        x1, x2 = x[:, :half], x[:, half:]
        return jnp.concatenate([x1 * cos - x2 * sin, x1 * sin + x2 * cos], axis=1)

    def rows(x):
        if lp == lb:
            return x
        return jnp.concatenate([x, jnp.zeros((lp - lb, x.shape[1]), x.dtype)], axis=0)

    row = lax.broadcasted_iota(jnp.int32, (lp, lp), 0)
    col = lax.broadcasted_iota(jnp.int32, (lp, lp), 1)
    diff = (row - col).astype(F32)
    idx = lax.broadcasted_iota(jnp.int32, (lp, 1), 0).astype(F32)
    heads = range(RET_HEADS)
    qm, km, kdm, vm, dmask, row_dec = [], [], [], [], [], []
    for h in heads:
        lg = lg_ref[h]
        q = rows(rope(qk_ref[0, :, h * RET_DK:(h + 1) * RET_DK]))
        k = rows(rope(qk_ref[0, :, qk_w + h * RET_DK:qk_w + (h + 1) * RET_DK]) * (RET_DK ** -0.5))
        qm.append(q.astype(BF16))
        km.append(k.astype(BF16))
        kdm.append((k * jnp.exp((l_true - 1.0 - idx) * lg)).astype(BF16))
        vm.append(rows(v_ref[0, :, h * RET_DV:(h + 1) * RET_DV]).astype(BF16))
        dmask.append(jnp.where(diff >= 0, jnp.exp(jnp.maximum(diff, 0.0) * lg), 0.0))
        row_dec.append(jnp.exp((idx + 1.0) * lg))
    sc = [(_nt_dot(qm[h], km[h]) * dmask[h]).astype(BF16) for h in heads]
    cross = [_dot(qm[h], s_ref[0, h].astype(BF16)) * row_dec[h] for h in heads]
    s_add = [_tn_dot(kdm[h], vm[h]) for h in heads]
    inner = [_dot(sc[h], vm[h]) for h in heads]
    outs = []
    for h in heads:
        s_dec = jnp.exp(jnp.zeros((1, RET_DV), F32) + l_true * lg_ref[h])
        s_ref[0, h] = s_ref[0, h] * s_dec + s_add[h]
        o = (inner[h] + cross[h])[:lb]
        o = o * lax.rsqrt(jnp.mean(o * o, axis=1, keepdims=True) + NORM_EPS)
        gv = g_ref[0, :, h * RET_DV:(h + 1) * RET_DV]
        outs.append(o * (gv * jax.nn.sigmoid(gv)))
    o_ref[0] = jnp.concatenate(outs, axis=1).astype(BF16)


def retention(z, cos, sin, lg, s0, lb, l_true):
    nb, m, _ = z.shape
    vw = RET_HEADS * RET_DV
    assert 2 * RET_HEADS * RET_DK == vw
    sspec = pl.BlockSpec((1, RET_HEADS, RET_DK, RET_DV), lambda b, c: (b, 0, 0, 0))
    return pl.pallas_call(
        functools.partial(_retention_kernel, lb=lb, l_true=float(l_true)),
        grid=(nb, m // lb),
        in_specs=[pl.BlockSpec(memory_space=pltpu.SMEM),
                  pl.BlockSpec((1, lb, vw), lambda b, c: (b, c, 0)),
                  pl.BlockSpec((1, lb, vw), lambda b, c: (b, c, 1)),
                  pl.BlockSpec((1, lb, vw), lambda b, c: (b, c, 2)),
                  pl.BlockSpec((1, lb, RET_DK // 2), lambda b, c: (0, c, 0)),
                  pl.BlockSpec((1, lb, RET_DK // 2), lambda b, c: (0, c, 0)),
                  sspec],
        out_specs=[pl.BlockSpec((1, lb, vw), lambda b, c: (b, c, 0)), sspec],
        out_shape=[jax.ShapeDtypeStruct((nb, m, vw), BF16),
                   jax.ShapeDtypeStruct((nb, RET_HEADS, RET_DK, RET_DV), F32)],
        compiler_params=_params(("parallel", "arbitrary")),
        name="retention",
    )(lg, z, z, z, cos, sin, s0)


def _rope_tables(pos, half):
    inv = ROPE_BASE ** (-jnp.arange(half, dtype=F32) / half)
    ang = pos.astype(F32)[:, None] * inv[None, :]
    return jnp.cos(ang), jnp.sin(ang)


def _mla_tables(pos):
    cos, sin = _rope_tables(pos, MLA_ROPE // 2)
    return jnp.concatenate([cos, cos], axis=1), jnp.concatenate([-sin, sin], axis=1)


def _even_layer(x, mods, pos_tabs, prm, past, tm):
    (w_in_p, g_mix, g_kv, wuk_t, wuk_r, wuv_t, wuv_all, mu, w0, w_w2, a0, w_a2, w_g2, k_k, k_a, r_k, ln_w, ln_b,
     w_out_mla, w_out_rw) = prm
    sh1, sc1, gt1 = mods
    cf, sf = pos_tabs
    nb, m, _ = x.shape
    zr, zq, zkv = norm_mod_matmul_split(x, g_mix, sh1, sc1, w_in_p, (RW_SHIFT_W, MLA_HEADS * (MLA_NOPE + MLA_ROPE), MLA_QK), tm)
    lat, kr, kcat, latt = kv_prep(zkv, g_kv, cf, sf, tm)
    if past is None:
        qt = q_prep_t(zq, wuk_r, cf, sf, min(MLA_PROMPT_TQ, m))
        mla_out = mla_prompt(qt, kcat, latt, wuv_t, 512)
        mla_transposed = True
        s0_t = jnp.zeros((nb, RW_HEADS, RW_N, RW_N), F32)
        rw_out, s_t = rwkv_fused(zr, jnp.zeros((nb, 1, RW_SHIFT_W), F32), mu, w0, w_w2, a0, w_a2, w_g2, k_k, k_a,
                                 r_k, ln_w, ln_b, s0_t, RW_CHUNK)
        s_new = jnp.swapaxes(s_t, -1, -2)
        shift_new = zr[:, -1]
    else:
        cache_lat, cache_kr, layer, page_table, s0, shift_prev, t_new = past
        nbs = m // t_new
        mla_transposed = False
        qcat = q_prep(zq, wuk_t, cf, sf, tm)
        q_s = qcat.reshape(MLA_HEADS, nbs, t_new, MLA_QK).transpose(1, 2, 0, 3).reshape(nbs, t_new * MLA_HEADS, MLA_QK)
        kn_s = jnp.concatenate([lat, kr], axis=-1).reshape(nbs, t_new, MLA_QK)
        mla_out = mla_sample(q_s, kn_s, wuv_all, cache_lat, cache_kr, layer, page_table).reshape(1, m, MLA_HEADS * MLA_V)
        zr_b = zr.reshape(nbs, t_new, RW_SHIFT_W)
        prev = jnp.concatenate([shift_prev[:, None, :], zr_b[:, :-1]], axis=1).reshape(1, m, RW_SHIFT_W)
        tens = rwkv_prep(zr, prev, mu, w0, w_w2, a0, w_a2, w_g2, k_k, k_a, min(tm, 256))
        tens = [u.reshape(RW_HEADS, nbs, t_new, RW_N).transpose(2, 0, 3, 1) for u in tens]
        lanes = lambda p: jnp.broadcast_to(p.reshape(RW_HEADS, RW_N, 1), (RW_HEADS, RW_N, nbs))
        rw_l, s_l = rwkv_step(*tens, lanes(r_k), lanes(ln_w), lanes(ln_b), jnp.transpose(s0, (1, 2, 3, 0)))
        rw_out = rw_l.transpose(3, 0, 1, 2).reshape(1, m, RW_W).astype(BF16)
        s_new = jnp.transpose(s_l, (3, 0, 1, 2))
        shift_new = zr_b[:, -1]
    x_new = matmul_gate_res([mla_out, rw_out], [w_out_mla, w_out_rw], x, gt1, min(2 * tm, m), (mla_transposed, False))
    return x_new, (lat, kr, s_new, shift_new)


def _odd_layer(x, mods, ret_tabs, prm, s0, t_new, tm):
    w_in, g_mix, w_out, lg = prm
    sh1, sc1, gt1 = mods
    cos, sin = ret_tabs
    nb, m, _ = x.shape
    z = norm_mod_matmul(x, g_mix, sh1, sc1, w_in, min(2 * tm, m), 2048)
    if s0 is None:
        s0 = jnp.zeros((nb, RET_HEADS, RET_DK, RET_DV), F32)
        lb = min(RET_BLOCK, m)
        o, s_new = retention(z, cos, sin, lg, s0, lb, lb)
    else:
        nbs = m // t_new
        lpad = 8
        z_b = jnp.pad(z.reshape(nbs, t_new, -1), ((0, 0), (0, lpad - t_new), (0, 0)))
        o, s_new = retention(z_b, cos, sin, lg, s0, lpad, t_new)
        o = o[:, :t_new].reshape(1, m, RET_HEADS * RET_DV)
    x_new = matmul_gate_res([o], [w_out], x, gt1, min(2 * tm, m))
    return x_new, s_new


def kernel(x_prompt, x_sample, c_prompt, c_sample, cache_kv_latent, cache_k_rope, page_table, state_rwkv, state_rwkv_shift, state_ret, w_ada, b_ada, g_norm_mix, g_norm_mlp, g_final, w_in_even, g_kv, w_uk, w_uv, rw_mu, rw_w0, rw_w2, rw_a0, rw_a2, rw_g2, rw_k_k, rw_k_a, rw_r_k, rw_ln_w, rw_ln_b, w_out_even, w_in_odd, w_out_odd, w_ff1, w_ff2):
    nbp, t_p, d = x_prompt.shape
    nbs, t_s, _ = x_sample.shape
    depth = w_ada.shape[0]
    past_len = page_table.shape[1] * PAGE_SIZE
    m_s = nbs * t_s
    tm_p = min(512, t_p)
    tm_s = m_s

    c_all = jnp.concatenate([jnp.repeat(c_sample, t_s, axis=0), c_prompt], axis=0)
    c_all = jnp.pad(c_all, ((0, -c_all.shape[0] % 16), (0, 0)))
    mods_all = ada_proj(c_all, w_ada, b_ada)

    def group_mods(l):
        mp = mods_all[l, m_s:m_s + nbp].reshape(nbp, 1, 6, d)
        return [mp[:, :, i] for i in range(6)], [(mods_all, l, i) for i in range(6)]

    pos_p = jnp.arange(t_p)
    pos_s = past_len + jnp.arange(t_s)
    cf_p, sf_p = _mla_tables(pos_p)
    cf_s, sf_s = _mla_tables(pos_s)
    mla_tabs_p = (cf_p[None], sf_p[None])
    mla_tabs_s = (jnp.tile(cf_s, (nbs, 1))[None], jnp.tile(sf_s, (nbs, 1))[None])
    cr_p, sr_p = _rope_tables(pos_p, RET_DK // 2)
    cr_s, sr_s = _rope_tables(pos_s, RET_DK // 2)
    ret_tabs_p = (cr_p[None], sr_p[None])
    ret_tabs_s = (jnp.pad(cr_s, ((0, 8 - t_s), (0, 0)))[None], jnp.pad(sr_s, ((0, 8 - t_s), (0, 0)))[None])
    lg = jnp.log(1 - 2.0 ** (-5.0 - jnp.arange(RET_HEADS, dtype=F32)))

    xp = x_prompt
    xs = x_sample.reshape(1, m_s, d)
    lat_p, kr_p, rw_p, sh_p, ret_p = [], [], [], [], []
    lat_s, kr_s, rw_s, sh_s, ret_s = [], [], [], [], []
    q_w = MLA_HEADS * (MLA_NOPE + MLA_ROPE)
    for l in range(depth):
        (sh1p, sc1p, gt1p, sh2p, sc2p, gt2p), (sh1s, sc1s, gt1s, sh2s, sc2s, gt2s) = group_mods(l)
        i = l // 2
        if l % 2 == 0:
            w_in = w_in_even[i]
            wq = w_in[:, :q_w].reshape(d, MLA_HEADS, MLA_NOPE + MLA_ROPE)
            w_in_p = jnp.concatenate([w_in[:, q_w + MLA_QK:],
                                      wq[:, :, :MLA_NOPE].reshape(d, -1), wq[:, :, MLA_NOPE:].reshape(d, -1),
                                      w_in[:, q_w:q_w + MLA_QK]], axis=1).astype(BF16)
            wuv = w_uv[i]
            mla_w = MLA_HEADS * MLA_V
            prm = (w_in_p, g_norm_mix[l], g_kv[i], jnp.transpose(w_uk[i], (1, 2, 0)).astype(BF16),
                   jnp.transpose(w_uk[i], (1, 0, 2)).astype(BF16), jnp.transpose(wuv, (1, 2, 0)).astype(BF16), wuv.reshape(KV_RANK, mla_w).astype(BF16),
                   rw_mu[i], rw_w0[i], rw_w2[i].astype(BF16), rw_a0[i], rw_a2[i].astype(BF16), rw_g2[i].astype(BF16),
                   rw_k_k[i], rw_k_a[i], rw_r_k[i], rw_ln_w[i], rw_ln_b[i],
                   w_out_even[i, :mla_w].astype(BF16), w_out_even[i, mla_w:].astype(BF16))
            xp, (la, kr, st, sh) = _even_layer(xp, (sh1p, sc1p, gt1p), mla_tabs_p, prm, None, tm_p)
            lat_p.append(la); kr_p.append(kr); rw_p.append(st); sh_p.append(sh)
            cache_kr_t = jnp.swapaxes(cache_k_rope, 2, 3)
            past = (cache_kv_latent, cache_kr_t, i, page_table, state_rwkv[i], state_rwkv_shift[i], t_s)
            xs, (la, kr, st, sh) = _even_layer(xs, (sh1s, sc1s, gt1s), mla_tabs_s, prm, past, tm_s)
            lat_s.append(la.reshape(nbs, t_s, KV_RANK)); kr_s.append(kr.reshape(nbs, t_s, MLA_ROPE))
            rw_s.append(st); sh_s.append(sh)
        else:
            prm = (w_in_odd[i].astype(BF16), g_norm_mix[l], w_out_odd[i].astype(BF16), lg)
            xp, st = _odd_layer(xp, (sh1p, sc1p, gt1p), ret_tabs_p, prm, None, t_s, tm_p)
            ret_p.append(st)
            xs, st = _odd_layer(xs, (sh1s, sc1s, gt1s), ret_tabs_s, prm, state_ret[i], t_s, tm_s)
            ret_s.append(st)
        final = l == depth - 1
        w1 = w_ff1[l].astype(BF16)
        w2 = w_ff2[l].astype(BF16)
        xp = mlp_block(xp, g_norm_mlp[l], sh2p, sc2p, gt2p, w1, w2, g_final, final, min(2 * tm_p, t_p), 1024)
        xs = mlp_block(xs, g_norm_mlp[l], sh2s, sc2s, gt2s, w1, w2, g_final, final, tm_s, 1024)
    return (xp, xs.reshape(nbs, t_s, d),
            jnp.stack(lat_p), jnp.stack(kr_p), jnp.stack(rw_p), jnp.stack(sh_p), jnp.stack(ret_p),
            jnp.stack(lat_s), jnp.stack(kr_s), jnp.stack(rw_s), jnp.stack(sh_s), jnp.stack(ret_s))
```
